```python
import math
import jax, jax.numpy as jnp
from jax import lax
import numpy as np

D_MODEL = 2048
BATCH = 1
SEQ = 16384
DEPTH = 1
DEC_BATCH = 32
DEC_SEQ = 8
PAST_LEN = 16384
PAGE_SIZE = 128

NSA_HEADS = 16
NSA_GROUPS = 4
NSA_HPG = NSA_HEADS // NSA_GROUPS
NSA_DK = 192
NSA_DV = 128
NSA_BLOCK = 64
NSA_N_SEL = 16
NSA_WINDOW = 512
NSA_CMP_HIDDEN = 256
Q_BLOCK = 128
GLA_HEADS = 4
GLA_DK = D_MODEL // (2 * GLA_HEADS)
GLA_DV = D_MODEL // GLA_HEADS
GLA_RANK = 16
GLA_TAU = 16.0
GLA_CHUNK = 64
NSA_WIDTH = NSA_HEADS * NSA_DV
GLA_WIDTH = GLA_HEADS * GLA_DV
NORM_EPS = 1e-6
SPLIT_SIZES = (NSA_HEADS * NSA_DK,
               NSA_GROUPS * NSA_DK, NSA_GROUPS * NSA_DV,
               NSA_GROUPS * NSA_DK, NSA_GROUPS * NSA_DV,
               NSA_GROUPS * NSA_DK, NSA_GROUPS * NSA_DV,
               3 * NSA_HEADS,
               NSA_WIDTH,
               GLA_HEADS * GLA_DK, GLA_HEADS * GLA_DK,
               GLA_WIDTH, GLA_WIDTH,
               GLA_RANK,
               2 * D_MODEL)
D_IN = sum(SPLIT_SIZES)

kernel_name = 'nsa_gla_parallel_hybrid_step'


def rms_norm(x, g):
    xf = x.astype(jnp.float32)
    y = xf * lax.rsqrt(jnp.mean(xf * xf, axis=-1, keepdims=True) + NORM_EPS)
    return (y * g.astype(jnp.float32)).astype(x.dtype)


def masked_softmax(s, mask):
    s = jnp.where(mask, s.astype(jnp.float32), -jnp.inf)
    m = jnp.max(s, axis=-1, keepdims=True)
    m = jnp.where(jnp.isfinite(m), m, 0.0)
    e = jnp.where(mask, jnp.exp(s - m), 0.0)
    return e / jnp.maximum(jnp.sum(e, axis=-1, keepdims=True), 1e-30)


def mixer_streams(x, c, ln_g, w_ada, b_ada, w_in, q_norm_g, k_slc_norm_g, k_win_norm_g, w_a2, b_a):
    B, T, _ = x.shape
    G, P, DK, DV = NSA_GROUPS, NSA_HPG, NSA_DK, NSA_DV
    mod = jax.nn.silu(c) @ w_ada + b_ada
    shift, scale, gate = jnp.split(mod, 3, axis=-1)
    h = rms_norm(x, ln_g) * (1.0 + scale[:, None, :]) + shift[:, None, :]
    cuts = np.cumsum(SPLIT_SIZES)[:-1].tolist()
    (q, k_cmp, v_cmp, k_slc, v_slc, k_win, v_win, g_nsa, z_nsa,
     q_g, k_g, v_g, z_g, a_g, m_gate) = jnp.split(h @ w_in, cuts, axis=-1)
    streams = dict(
        q=rms_norm(q.reshape(B, T, NSA_HEADS, DK), q_norm_g).reshape(B, T, G, P, DK),
        k_cmp=k_cmp.reshape(B, T, G, DK),
        v_cmp=v_cmp.reshape(B, T, G, DV),
        k_slc=rms_norm(k_slc.reshape(B, T, G, DK), k_slc_norm_g),
        v_slc=v_slc.reshape(B, T, G, DV),
        k_win=rms_norm(k_win.reshape(B, T, G, DK), k_win_norm_g),
        v_win=v_win.reshape(B, T, G, DV),
        g_nsa=jax.nn.sigmoid(g_nsa.reshape(B, T, G, P, 3)),
        z_nsa=z_nsa,
        q_g=q_g.reshape(B, T, GLA_HEADS, GLA_DK) * GLA_DK ** -0.5,
        k_g=k_g.reshape(B, T, GLA_HEADS, GLA_DK),
        v_g=v_g.reshape(B, T, GLA_HEADS, GLA_DV),
        log_a=(jax.nn.log_sigmoid((a_g @ w_a2 + b_a).astype(jnp.float32)) / GLA_TAU).reshape(B, T, GLA_HEADS, GLA_DK),
        z_g=z_g,
        m_gate=m_gate)
    return streams, gate


def nsa_compress(rows, pe, w1, w2):
    lead = rows.shape[:-3]
    L, G, Dh = rows.shape[-3:]
    blk = rows.reshape(lead + (L // NSA_BLOCK, NSA_BLOCK, G, Dh)) + pe[:, None, :]
    hid = jax.nn.silu(jnp.einsum('...nsgd,sdh->...ngh', blk, w1))
    return jnp.einsum('...ngh,hd->...ngd', hid, w2)


def cmp_attend(q, kc, vc, t):
    s = jnp.einsum('btgpd,bngd->btgpn', q, kc) * NSA_DK ** -0.5
    mask = jnp.arange(kc.shape[1])[None, :] < ((t + 1) // NSA_BLOCK)[:, None]
    p = masked_softmax(s, mask[None, :, None, None, :])
    o = jnp.einsum('btgpn,bngd->btgpd', p.astype(vc.dtype), vc)
    return o, jnp.sum(p, axis=3)


def select_blocks(score, t, k):
    j = jnp.arange(score.shape[-1])[None, :]
    cb = (t // NSA_BLOCK)[:, None]
    allowed = j <= cb
    forced = allowed & ((j == 0) | (j == cb) | (j == cb - 1))
    val = jnp.where(forced[None, :, None, :], jnp.inf,
                    jnp.where(allowed[None, :, None, :], score, -jnp.inf))
    top, idx = lax.top_k(val, k)
    return idx, top > -jnp.inf


def window_mask(t, s):
    dt = t[:, None] - s[None, :]
    return (dt >= 0) & (dt <= NSA_WINDOW) & (s[None, :] >= 0)


def shared_attend(q, k, v, mask):
    s = jnp.einsum('btgpd,bngd->btgpn', q, k) * NSA_DK ** -0.5
    p = masked_softmax(s, mask[None, :, None, None, :]).astype(v.dtype)
    return jnp.einsum('btgpn,bngd->btgpd', p, v)


def nsa_prompt(q, g_nsa, k_cmp, v_cmp, k_slc, v_slc, k_win, v_win,
               pe_k, w1_k, w2_k, pe_v, w1_v, w2_v, k_cmp_norm_g):
    B, S = q.shape[:2]
    G, P, DK, DV, SB = NSA_GROUPS, NSA_HPG, NSA_DK, NSA_DV, NSA_BLOCK
    kc = rms_norm(nsa_compress(k_cmp, pe_k, w1_k, w2_k), k_cmp_norm_g)
    vc = nsa_compress(v_cmp, pe_v, w1_v, w2_v)
    nb = S // SB
    n_keep = min(NSA_N_SEL, nb)
    kb = k_slc.reshape(B, nb, SB, G, DK).transpose(0, 1, 3, 2, 4)
    vb = v_slc.reshape(B, nb, SB, G, DV).transpose(0, 1, 3, 2, 4)
    k_pad = jnp.pad(k_win, ((0, 0), (NSA_WINDOW, 0), (0, 0), (0, 0)))
    v_pad = jnp.pad(v_win, ((0, 0), (NSA_WINDOW, 0), (0, 0), (0, 0)))
    bi = jnp.arange(B)[:, None, None, None]
    gi = jnp.arange(G)[None, None, :, None]
    qb = min(Q_BLOCK, S)

    def block(i):
        t0 = i * qb
        t = t0 + jnp.arange(qb)
        q_i = lax.dynamic_slice_in_dim(q, t0, qb, axis=1)
        g_i = lax.dynamic_slice_in_dim(g_nsa, t0, qb, axis=1)
        o_cmp, score = cmp_attend(q_i, kc, vc, t)
        idx, valid = select_blocks(score, t, n_keep)
        k_sel = kb[bi, idx, gi].reshape(B, qb, G, n_keep * SB, DK)
        v_sel = vb[bi, idx, gi].reshape(B, qb, G, n_keep * SB, DV)
        pos = idx[..., None] * SB + jnp.arange(SB)
        m_sel = (valid[..., None] & (pos <= t[None, :, None, None, None])).reshape(B, qb, G, n_keep * SB)
        s = jnp.einsum('btgpd,btgnd->btgpn', q_i, k_sel) * DK ** -0.5
        p = masked_softmax(s, m_sel[:, :, :, None, :]).astype(v_sel.dtype)
        o_slc = jnp.einsum('btgpn,btgnd->btgpd', p, v_sel)
        kw = lax.dynamic_slice_in_dim(k_pad, t0, NSA_WINDOW + qb, axis=1)
        vw = lax.dynamic_slice_in_dim(v_pad, t0, NSA_WINDOW + qb, axis=1)
        s_pos = t0 - NSA_WINDOW + jnp.arange(NSA_WINDOW + qb)
        o_win = shared_attend(q_i, kw, vw, window_mask(t, s_pos))
        return g_i[..., 0:1] * o_cmp + g_i[..., 1:2] * o_slc + g_i[..., 2:3] * o_win

    out = lax.map(block, jnp.arange(S // qb))
    return jnp.moveaxis(out, 0, 1).reshape(B, S, G, P, DV)


def nsa_sample(layer, q, g_nsa, k_slc_new, v_slc_new, k_win_new, v_win_new,
               cache_k_cmp, cache_v_cmp, cache_k_slc, cache_v_slc, k_win_buf, v_win_buf, page_table,
               pe_k, w1_k, w2_k, pe_v, w1_v, w2_v, k_cmp_norm_g):
    DB, T = q.shape[:2]
    G, P, DK, DV, SB = NSA_GROUPS, NSA_HPG, NSA_DK, NSA_DV, NSA_BLOCK
    past = page_table.shape[1] * PAGE_SIZE
    t = past + jnp.arange(T)

    def compress_seq(pt_row):
        kr = cache_k_cmp[layer, pt_row].reshape(past, G, DK)
        vr = cache_v_cmp[layer, pt_row].reshape(past, G, DV)
        return (rms_norm(nsa_compress(kr, pe_k, w1_k, w2_k), k_cmp_norm_g),
                nsa_compress(vr, pe_v, w1_v, w2_v))

    kc, vc = lax.map(compress_seq, page_table)
    o_cmp, score = cmp_attend(q, kc, vc, t)

    n_keep = min(NSA_N_SEL - 1, past // SB)
    idx, valid = select_blocks(score, t, n_keep)
    bpp = PAGE_SIZE // SB
    page = page_table[jnp.arange(DB)[:, None, None, None], idx // bpp][..., None]
    off = ((idx % bpp) * SB)[..., None] + jnp.arange(SB)
    gi = jnp.arange(G)[None, None, :, None, None]
    n_sel = n_keep * SB
    k_sel = cache_k_slc[layer, page, off, gi].reshape(DB, T, G, n_sel, DK)
    v_sel = cache_v_slc[layer, page, off, gi].reshape(DB, T, G, n_sel, DV)
    s = jnp.concatenate([jnp.einsum('btgpd,btgnd->btgpn', q, k_sel),
                         jnp.einsum('btgpd,bngd->btgpn', q, k_slc_new)], axis=-1) * DK ** -0.5
    m_past = jnp.broadcast_to(valid[:, :, :, None, :, None], (DB, T, G, P, n_keep, SB)).reshape(DB, T, G, P, n_sel)
    causal = jnp.arange(T)[None, :] <= jnp.arange(T)[:, None]
    m_new = jnp.broadcast_to(causal[None, :, None, None, :], (DB, T, G, P, T))
    p = masked_softmax(s, jnp.concatenate([m_past, m_new], axis=-1)).astype(v_sel.dtype)
    o_slc = (jnp.einsum('btgpn,btgnd->btgpd', p[..., :n_sel], v_sel)
             + jnp.einsum('btgpn,bngd->btgpd', p[..., n_sel:], v_slc_new))

    W = k_win_buf.shape[1]
    kw = jnp.concatenate([k_win_buf, k_win_new], axis=1)
    vw = jnp.concatenate([v_win_buf, v_win_new], axis=1)
    o_win = shared_attend(q, kw, vw, window_mask(t, past - W + jnp.arange(W + T)))
    o = g_nsa[..., 0:1] * o_cmp + g_nsa[..., 1:2] * o_slc + g_nsa[..., 2:3] * o_win
    return o, kw[:, T:], vw[:, T:]


def gla_chunked(q, k, v, log_a, s0):
    B, T, H, _ = q.shape
    DVh = v.shape[-1]
    C = math.gcd(T, GLA_CHUNK)
    n = T // C
    f32 = jnp.float32

    def to_chunks(a):
        return jnp.moveaxis(a.astype(f32).reshape(B, n, C, H, a.shape[-1]), 1, 0)

    causal = jnp.tril(jnp.ones((C, C), dtype=bool))

    def step(S, xs):
        qc, kc, vc, ac = xs
        b = jnp.cumsum(ac, axis=1)
        o_inter = jnp.einsum('bthk,bhkv->bthv', qc * jnp.exp(b), S)
        rel = jnp.where(causal[None, :, :, None, None], b[:, :, None] - b[:, None, :], -jnp.inf)
        A = jnp.einsum('bthk,bshk,btshk->bhts', qc, kc, jnp.exp(rel))
        o_intra = jnp.einsum('bhts,bshv->bthv', A, vc)
        b_last = b[:, -1]
        S_new = (jnp.exp(b_last)[..., None] * S
                 + jnp.einsum('bshk,bshv->bhkv', kc * jnp.exp(b_last[:, None] - b), vc))
        return S_new, o_inter + o_intra

    S, o = lax.scan(step, s0.astype(f32), (to_chunks(q), to_chunks(k), to_chunks(v), to_chunks(log_a)))
    o = jnp.moveaxis(o, 0, 1).reshape(B, T, H, DVh).astype(v.dtype)
    return o, S.astype(s0.dtype)


def merge_output(x, gate, o_nsa, z_nsa, o_gla, z_g, m_gate, gla_norm_g, w_branch, w_out):
    B, T = x.shape[:2]
    u_a = (o_nsa.reshape(B, T, NSA_WIDTH) * jax.nn.silu(z_nsa)) @ w_branch[:NSA_WIDTH]
    u_b = (rms_norm(o_gla, gla_norm_g).reshape(B, T, GLA_WIDTH) * jax.nn.silu(z_g)) @ w_branch[NSA_WIDTH:]
    mg_a, mg_b = jnp.split(m_gate, 2, axis=-1)
    merged = jax.nn.sigmoid(mg_a) * u_a + jax.nn.sigmoid(mg_b) * u_b
    return x + gate[:, None, :] * (merged @ w_out)


def decoder_layer(layer, x_p, x_s, c_p, c_s, cache_k_cmp, cache_v_cmp, cache_k_slc, cache_v_slc,
                  k_win_buf, v_win_buf, gla_state, page_table,
                  ln_g, w_ada, b_ada, w_in, q_norm_g, k_cmp_norm_g, k_slc_norm_g, k_win_norm_g,
                  pe_k, w1_k, w2_k, pe_v, w1_v, w2_v, w_a2, b_a, gla_norm_g, w_branch, w_out):
    proj = (ln_g, w_ada, b_ada, w_in, q_norm_g, k_slc_norm_g, k_win_norm_g, w_a2, b_a)
    cmp_w = (pe_k, w1_k, w2_k, pe_v, w1_v, w2_v, k_cmp_norm_g)
    sp, gate_p = mixer_streams(x_p, c_p, *proj)
    o_nsa_p = nsa_prompt(sp['q'], sp['g_nsa'], sp['k_cmp'], sp['v_cmp'], sp['k_slc'], sp['v_slc'],
                         sp['k_win'], sp['v_win'], *cmp_w)
    s0 = jnp.zeros((x_p.shape[0], GLA_HEADS, GLA_DK, GLA_DV), x_p.dtype)
    o_gla_p, gla_p = gla_chunked(sp['q_g'], sp['k_g'], sp['v_g'], sp['log_a'], s0)
    y_p = merge_output(x_p, gate_p, o_nsa_p, sp['z_nsa'], o_gla_p, sp['z_g'], sp['m_gate'], gla_norm_g, w_branch, w_out)
    wp = min(NSA_WINDOW, x_p.shape[1])
    ss, gate_s = mixer_streams(x_s, c_s, *proj)
    o_nsa_s, kw_s, vw_s = nsa_sample(layer, ss['q'], ss['g_nsa'], ss['k_slc'], ss['v_slc'], ss['k_win'], ss['v_win'],
                                     cache_k_cmp, cache_v_cmp, cache_k_slc, cache_v_slc, k_win_buf, v_win_buf,
                                     page_table, *cmp_w)
    o_gla_s, gla_s = gla_chunked(ss['q_g'], ss['k_g'], ss['v_g'], ss['log_a'], gla_state)
    y_s = merge_output(x_s, gate_s, o_nsa_s, ss['z_nsa'], o_gla_s, ss['z_g'], ss['m_gate'], gla_norm_g, w_branch, w_out)
    new_state = (sp['k_cmp'], sp['v_cmp'], sp['k_slc'], sp['v_slc'], sp['k_win'][:, -wp:], sp['v_win'][:, -wp:], gla_p,
                 ss['k_cmp'], ss['v_cmp'], ss['k_slc'], ss['v_slc'], kw_s, vw_s, gla_s)
    return y_p, y_s, new_state


def setup_inputs(seed: int = 0) -> dict:
    key = jax.random.key(seed)
    ks = iter(jax.random.split(key, 48))
    G = NSA_GROUPS

    def nrm(shape, scale=1.0):
        return jax.random.normal(next(ks), shape, jnp.float32) * scale

    def gain(n):
        return 1.0 + nrm((DEPTH, n), 0.01)

    n_pages = PAST_LEN // PAGE_SIZE
    n_used = DEC_BATCH * n_pages
    n_pool = n_used + max(1, n_used // 4)
    win = min(NSA_WINDOW, PAST_LEN)
    x_prompt = nrm((BATCH, SEQ, D_MODEL))
    x_sample = nrm((DEC_BATCH, DEC_SEQ, D_MODEL))
    cache_k_cmp = nrm((DEPTH, n_pool, PAGE_SIZE, G, NSA_DK))
    cache_v_cmp = nrm((DEPTH, n_pool, PAGE_SIZE, G, NSA_DV))
    cache_k_slc = nrm((DEPTH, n_pool, PAGE_SIZE, G, NSA_DK))
    cache_v_slc = nrm((DEPTH, n_pool, PAGE_SIZE, G, NSA_DV))
    state_k_win = nrm((DEPTH, DEC_BATCH, win, G, NSA_DK))
    state_v_win = nrm((DEPTH, DEC_BATCH, win, G, NSA_DV))
    state_gla = nrm((DEPTH, DEC_BATCH, GLA_HEADS, GLA_DK, GLA_DV), 0.5)
    page_table = jax.random.permutation(next(ks), n_pool)[:n_used].reshape(DEC_BATCH, n_pages).astype(jnp.int32)
    c_prompt = nrm((BATCH, D_MODEL))
    c_sample = nrm((DEC_BATCH, D_MODEL))
    return {
        'x_prompt': x_prompt, 'x_sample': x_sample,
        'cache_k_cmp': cache_k_cmp, 'cache_v_cmp': cache_v_cmp,
        'cache_k_slc': cache_k_slc, 'cache_v_slc': cache_v_slc,
        'state_k_win': state_k_win, 'state_v_win': state_v_win, 'state_gla': state_gla,
        'page_table': page_table, 'c_prompt': c_prompt, 'c_sample': c_sample,
        'ln_g': gain(D_MODEL),
        'w_ada': nrm((DEPTH, D_MODEL, 3 * D_MODEL), 0.5 * D_MODEL ** -0.5),
        'b_ada': nrm((DEPTH, 3 * D_MODEL), 0.01),
        'w_in': nrm((DEPTH, D_MODEL, D_IN), D_MODEL ** -0.5),
        'q_norm_g': gain(NSA_DK), 'k_cmp_norm_g': gain(NSA_DK),
        'k_slc_norm_g': gain(NSA_DK), 'k_win_norm_g': gain(NSA_DK),
        'pe_k': nrm((DEPTH, NSA_BLOCK, NSA_DK), 0.1),
        'w1_k': nrm((DEPTH, NSA_BLOCK, NSA_DK, NSA_CMP_HIDDEN), (NSA_BLOCK * NSA_DK) ** -0.5),
        'w2_k': nrm((DEPTH, NSA_CMP_HIDDEN, NSA_DK), NSA_CMP_HIDDEN ** -0.5),
        'pe_v': nrm((DEPTH, NSA_BLOCK, NSA_DV), 0.1),
        'w1_v': nrm((DEPTH, NSA_BLOCK, NSA_DV, NSA_CMP_HIDDEN), (NSA_BLOCK * NSA_DV) ** -0.5),
        'w2_v': nrm((DEPTH, NSA_CMP_HIDDEN, NSA_DV), NSA_CMP_HIDDEN ** -0.5),
        'w_a2': nrm((DEPTH, GLA_RANK, GLA_HEADS * GLA_DK), GLA_RANK ** -0.5),
        'b_a': nrm((DEPTH, GLA_HEADS * GLA_DK), 0.1),
        'gla_norm_g': gain(GLA_DV),
        'w_branch': nrm((DEPTH, NSA_WIDTH + GLA_WIDTH, D_MODEL), NSA_WIDTH ** -0.5),
        'w_out': nrm((DEPTH, D_MODEL, D_MODEL), D_MODEL ** -0.5),
    }


def reference(x_prompt, x_sample, cache_k_cmp, cache_v_cmp, cache_k_slc, cache_v_slc,
              state_k_win, state_v_win, state_gla, page_table, c_prompt, c_sample,
              ln_g, w_ada, b_ada, w_in, q_norm_g, k_cmp_norm_g, k_slc_norm_g, k_win_norm_g,
              pe_k, w1_k, w2_k, pe_v, w1_v, w2_v, w_a2, b_a, gla_norm_g, w_branch, w_out):
    y_prompt, y_sample = x_prompt, x_sample
    per_layer = []
    for layer in range(DEPTH):
        y_prompt, y_sample, new_state = decoder_layer(
            layer, y_prompt, y_sample, c_prompt, c_sample,
            cache_k_cmp, cache_v_cmp, cache_k_slc, cache_v_slc,
            state_k_win[layer], state_v_win[layer], state_gla[layer], page_table,
            ln_g[layer], w_ada[layer], b_ada[layer], w_in[layer], q_norm_g[layer], k_cmp_norm_g[layer],
            k_slc_norm_g[layer], k_win_norm_g[layer], pe_k[layer], w1_k[layer], w2_k[layer],
            pe_v[layer], w1_v[layer], w2_v[layer], w_a2[layer], b_a[layer], gla_norm_g[layer],
            w_branch[layer], w_out[layer])
        per_layer.append(new_state)
    (k_cmp_p, v_cmp_p, k_slc_p, v_slc_p, k_win_p, v_win_p, gla_p,
     k_cmp_s, v_cmp_s, k_slc_s, v_slc_s, k_win_s, v_win_s, gla_s) = [jnp.stack(a) for a in zip(*per_layer)]
    return (y_prompt, y_sample, k_cmp_p, v_cmp_p, k_slc_p, v_slc_p, k_win_p, v_win_p, gla_p,
            k_cmp_s, v_cmp_s, k_slc_s, v_slc_s, k_win_s, v_win_s, gla_s)
```

```python
import functools
import math

import jax
import jax.numpy as jnp
from jax import lax
from jax.experimental import pallas as pl
from jax.experimental.pallas import tpu as pltpu

F32 = jnp.float32
BF16 = jnp.bfloat16

NSA_HEADS = 16
NSA_GROUPS = 4
NSA_HPG = NSA_HEADS // NSA_GROUPS
NSA_DK = 192
NSA_DV = 128
NSA_BLOCK = 64
NSA_N_SEL = 16
NSA_WINDOW = 512
NSA_CMP_HIDDEN = 256
PAGE_SIZE = 128
GLA_HEADS = 4
GLA_DK = 256
GLA_DV = 512
GLA_RANK = 16
GLA_TAU = 16.0
GLA_CHUNK = 64
NORM_EPS = 1e-6
D_MODEL = 2048
NSA_WIDTH = NSA_HEADS * NSA_DV
GLA_WIDTH = GLA_HEADS * GLA_DV

DKP = 256
LOG2E = 1.4426950408889634
NEG = -1e30
VMEM_LIMIT = 56 * 1024 * 1024

_SEG = {}
_off = 0
for _name, _w in (("z_nsa", NSA_WIDTH), ("z_g", GLA_WIDTH), ("v_g", GLA_WIDTH), ("m_gate", 2 * D_MODEL),
                  ("q", NSA_HEADS * DKP), ("k_cmp", NSA_GROUPS * DKP), ("v_cmp", NSA_GROUPS * NSA_DV),
                  ("k_slc", NSA_GROUPS * DKP), ("v_slc", NSA_GROUPS * NSA_DV),
                  ("k_win", NSA_GROUPS * DKP), ("v_win", NSA_GROUPS * NSA_DV),
                  ("q_g", GLA_HEADS * GLA_DK), ("k_g", GLA_HEADS * GLA_DK),
                  ("g_nsa", 256), ("a_g", 256)):
    _SEG[_name] = _off
    _off += _w
D_PAD = _off


def _cparams(sem):
    return pltpu.CompilerParams(dimension_semantics=sem, vmem_limit_bytes=VMEM_LIMIT)


def _silu(x):
    return x * jax.nn.sigmoid(x)


def _ada_kernel(c_ref, w_ref, b_ref, o_ref):
    a = _silu(c_ref[...]).astype(BF16)
    o_ref[...] = jnp.dot(a, w_ref[...].astype(BF16), preferred_element_type=F32) + b_ref[...]


def _ada(c_all, w_ada, b_ada):
    m, d = c_all.shape
    n = w_ada.shape[1]
    tn = 768
    return pl.pallas_call(
        _ada_kernel,
        grid=(n // tn,),
        in_specs=[pl.BlockSpec((m, d), lambda j: (0, 0)),
                  pl.BlockSpec((d, tn), lambda j: (0, j)),
                  pl.BlockSpec((1, tn), lambda j: (0, j))],
        out_specs=pl.BlockSpec((m, tn), lambda j: (0, j)),
        out_shape=jax.ShapeDtypeStruct((m, n), F32),
        compiler_params=_cparams(("arbitrary",)),
        name="ada",
    )(c_all, w_ada, b_ada)


def _proj_kernel(x_ref, sc_ref, sh_ref, g_ref, w_ref, o_ref, h_ref):
    @pl.when(pl.program_id(1) == 0)
    def _():
        x = x_ref[...]
        y = x * lax.rsqrt(jnp.mean(x * x, axis=-1, keepdims=True) + NORM_EPS) * g_ref[...]
        h_ref[...] = (y * (1.0 + sc_ref[...]) + sh_ref[...]).astype(BF16)

    o_ref[...] = jnp.dot(h_ref[...], w_ref[...], preferred_element_type=F32)


def _proj(x, scale, shift, ln_g, w_pad, tm):
    m, d = x.shape
    n = w_pad.shape[1]
    tn = 1536
    per_row = scale.shape[0] != 1
    mod_spec = (pl.BlockSpec((tm, d), lambda i, j: (i, 0)) if per_row
                else pl.BlockSpec((1, d), lambda i, j: (0, 0)))
    return pl.pallas_call(
        _proj_kernel,
        grid=(m // tm, n // tn),
        in_specs=[pl.BlockSpec((tm, d), lambda i, j: (i, 0)), mod_spec, mod_spec,
                  pl.BlockSpec((1, d), lambda i, j: (0, 0)),
                  pl.BlockSpec((d, tn), lambda i, j: (0, j))],
        out_specs=pl.BlockSpec((tm, tn), lambda i, j: (i, j)),
        out_shape=jax.ShapeDtypeStruct((m, n), F32),
        scratch_shapes=[pltpu.VMEM((tm, d), BF16)],
        compiler_params=_cparams(("arbitrary", "arbitrary")),
        name="proj",
    )(x, scale, shift, ln_g, w_pad)


def _head_norm(x, g):
    ms = jnp.sum(x * x, axis=-1, keepdims=True) * (1.0 / NSA_DK)
    return x * lax.rsqrt(ms + NORM_EPS) * g


def _prep_q_kernel(x_ref, g_ref, o_ref):
    o_ref[0] = _head_norm(x_ref[...], g_ref[...]).T.astype(BF16)


def _prep_q(p, g_pad, tm):
    t = p.shape[0]
    base = _SEG["q"] // DKP
    return pl.pallas_call(
        _prep_q_kernel,
        grid=(t // tm, NSA_HEADS),
        in_specs=[pl.BlockSpec((tm, DKP), lambda i, h: (i, base + h)),
                  pl.BlockSpec((1, DKP), lambda i, h: (0, 0))],
        out_specs=pl.BlockSpec((1, DKP, tm), lambda i, h: (h, 0, i)),
        out_shape=jax.ShapeDtypeStruct((NSA_HEADS, DKP, t), BF16),
        compiler_params=_cparams(("arbitrary", "arbitrary")),
        name="prep_q",
    )(p, g_pad)


def _prep_k_kernel(x_ref, g_ref, o_ref, ob_ref):
    y = _head_norm(x_ref[...], g_ref[...])
    o_ref[...] = y
    ob_ref[...] = y.astype(BF16)


def _prep_k(p, seg, g_pad, tm):
    t = p.shape[0]
    base = _SEG[seg] // DKP
    n = NSA_GROUPS * DKP
    return pl.pallas_call(
        _prep_k_kernel,
        grid=(t // tm, NSA_GROUPS),
        in_specs=[pl.BlockSpec((tm, DKP), lambda i, g: (i, base + g)),
                  pl.BlockSpec((1, DKP), lambda i, g: (0, 0))],
        out_specs=[pl.BlockSpec((tm, DKP), lambda i, g: (i, g)),
                   pl.BlockSpec((tm, DKP), lambda i, g: (i, g))],
        out_shape=[jax.ShapeDtypeStruct((t, n), F32), jax.ShapeDtypeStruct((t, n), BF16)],
        compiler_params=_cparams(("arbitrary", "arbitrary")),
        name="prep_k_" + seg,
    )(p, g_pad)


def _prep_vt_kernel(x_ref, o_ref):
    o_ref[0, 0] = x_ref[...].T.astype(BF16)


def _prep_vt(p, seg, tile):
    t = p.shape[0]
    base = _SEG[seg] // NSA_DV
    return pl.pallas_call(
        _prep_vt_kernel,
        grid=(t // tile, NSA_GROUPS),
        in_specs=[pl.BlockSpec((tile, NSA_DV), lambda i, g: (i, base + g))],
        out_specs=pl.BlockSpec((1, 1, NSA_DV, tile), lambda i, g: (g, i, 0, 0)),
        out_shape=jax.ShapeDtypeStruct((NSA_GROUPS, t // tile, NSA_DV, tile), BF16),
        compiler_params=_cparams(("arbitrary", "arbitrary")),
        name="prep_vt_" + seg,
    )(p)


def _prep_gate_kernel(x_ref, o_ref):
    o_ref[...] = jax.nn.sigmoid(x_ref[...]).T[:4 * NSA_HEADS, :]


def _prep_gate(p, tm):
    t = p.shape[0]
    base = _SEG["g_nsa"] // 256
    return pl.pallas_call(
        _prep_gate_kernel,
        grid=(t // tm,),
        in_specs=[pl.BlockSpec((tm, 256), lambda i: (i, base))],
        out_specs=pl.BlockSpec((4 * NSA_HEADS, tm), lambda i: (0, i)),
        out_shape=jax.ShapeDtypeStruct((4 * NSA_HEADS, t), F32),
        compiler_params=_cparams(("arbitrary",)),
        name="prep_gate",
    )(p)


def _cmp_kernel(x_ref, pe_ref, w1_ref, w2_ref, g_ref, o_ref, acc_ref, *, norm, transpose_out):
    s = pl.program_id(1)

    @pl.when(s == 0)
    def _():
        acc_ref[...] = jnp.zeros_like(acc_ref)

    xb = (x_ref[...] + pe_ref[0]).astype(BF16)
    acc_ref[...] += jnp.dot(xb, w1_ref[0], preferred_element_type=F32)

    @pl.when(s == pl.num_programs(1) - 1)
    def _():
        hid = _silu(acc_ref[...]).astype(BF16)
        y = jnp.dot(hid, w2_ref[...], preferred_element_type=F32)
        if norm:
            y = _head_norm(y, g_ref[...])
        if transpose_out:
            y = y.T
        o_ref[0] = y.astype(o_ref.dtype)


def _compress_prompt(p, seg, dp, pe_pad, w1_pad, w2_pad, g_pad, norm, transpose_out):
    t = p.shape[0]
    nb = t // NSA_BLOCK
    pv = p.reshape(nb, NSA_BLOCK * D_PAD)
    per_row = D_PAD // dp
    base = _SEG[seg] // dp
    out_shape = (NSA_GROUPS, dp, nb) if transpose_out else (NSA_GROUPS, nb, dp)
    return pl.pallas_call(
        functools.partial(_cmp_kernel, norm=norm, transpose_out=transpose_out),
        grid=(NSA_GROUPS, NSA_BLOCK),
        in_specs=[pl.BlockSpec((nb, dp), lambda g, s: (0, s * per_row + base + g)),
                  pl.BlockSpec((1, 1, dp), lambda g, s: (s, 0, 0)),
                  pl.BlockSpec((1, dp, NSA_CMP_HIDDEN), lambda g, s: (s, 0, 0)),
                  pl.BlockSpec((NSA_CMP_HIDDEN, dp), lambda g, s: (0, 0)),
                  pl.BlockSpec((1, dp), lambda g, s: (0, 0))],
        out_specs=pl.BlockSpec((1,) + out_shape[1:], lambda g, s: (g, 0, 0)),
        out_shape=jax.ShapeDtypeStruct(out_shape, BF16),
        scratch_shapes=[pltpu.VMEM((nb, NSA_CMP_HIDDEN), F32)],
        compiler_params=_cparams(("arbitrary", "arbitrary")),
        name="cmp_" + seg,
    )(pv, pe_pad, w1_pad, w2_pad, g_pad)


def _masked_softmax_cols(s, valid):
    s = jnp.where(valid, s, NEG)
    m = jnp.max(s, axis=0, keepdims=True)
    e = jnp.where(valid, jnp.exp2(s - m), 0.0)
    return e / jnp.maximum(jnp.sum(e, axis=0, keepdims=True), 1e-30)


def _nsa_prompt_kernel(qt_ref, kc_ref, vct_ref, ks_ref, vst_ref, kw_ref, vwt_ref, gt_ref,
                       o_ref, sel_ref, acc_ref, *, tq, tk, n_keep):
    i = pl.program_id(1)
    r = NSA_HPG * tq
    t0 = i * tq
    c2 = (NSA_DK ** -0.5) * LOG2E
    qt = jnp.concatenate([qt_ref[h] for h in range(NSA_HPG)], axis=1)
    t_row = t0 + (lax.broadcasted_iota(jnp.int32, (1, r), 1) & (tq - 1))

    nb = kc_ref.shape[1]
    s = jnp.dot(kc_ref[0], qt, preferred_element_type=F32) * c2
    n_io = lax.broadcasted_iota(jnp.int32, (nb, r), 0)
    p = _masked_softmax_cols(s, n_io < ((t_row + 1) >> 6))
    o_cmp = jnp.dot(vct_ref[0], p.astype(BF16), preferred_element_type=F32)
    score = p[:, 0:tq]
    for h in range(1, NSA_HPG):
        score = score + p[:, h * tq:(h + 1) * tq]

    n_q = lax.broadcasted_iota(jnp.int32, (nb, tq), 0)
    n_f = n_q.astype(F32)
    cb = (t0 + lax.broadcasted_iota(jnp.int32, (1, tq), 1)) >> 6
    allowed = n_q <= cb
    forced = allowed & ((n_q == 0) | (n_q == cb) | (n_q == cb - 1))
    val0 = jnp.where(forced, jnp.inf, jnp.where(allowed, score, -jnp.inf))

    def pick(_, carry):
        val, sel = carry
        mx = jnp.max(val, axis=0, keepdims=True)
        idx = jnp.min(jnp.where(val == mx, n_f, float(nb)), axis=0, keepdims=True)
        hit = n_f == idx
        sel = jnp.where(hit & (mx > -jnp.inf), 1.0, sel)
        return jnp.where(hit, -jnp.inf, val), sel

    _, sel = lax.fori_loop(0, n_keep, pick, (val0, jnp.zeros((nb, tq), F32)))
    sel_ref[...] = jnp.concatenate([sel] * NSA_HPG, axis=1)

    nbk = tk // NSA_BLOCK
    acc_ref[...] = jnp.zeros_like(acc_ref)

    def slc_tile(j, carry, causal):
        m, l = carry
        k = ks_ref[pl.ds(pl.multiple_of(j * tk, tk), tk), :]
        sc = jnp.dot(k, qt, preferred_element_type=F32) * c2
        selt = sel_ref[pl.ds(pl.multiple_of(j * nbk, nbk), nbk), :]
        mask = jnp.broadcast_to(selt[:, None, :], (nbk, NSA_BLOCK, r)).reshape(tk, r) > 0.0
        if causal:
            pos = j * tk + lax.broadcasted_iota(jnp.int32, (tk, r), 0)
            mask = mask & (pos <= t_row)
        sc = jnp.where(mask, sc, NEG)
        m_new = jnp.maximum(m, jnp.max(sc, axis=0, keepdims=True))
        alpha = jnp.exp2(m - m_new)
        pp = jnp.exp2(sc - m_new)
        l_new = l * alpha + jnp.sum(pp, axis=0, keepdims=True)
        pv = jnp.dot(vst_ref[0, j], pp.astype(BF16), preferred_element_type=F32)
        acc_ref[...] = acc_ref[...] * alpha + pv
        return m_new, l_new

    n_full = t0 // tk
    init = (jnp.full((1, r), NEG, F32), jnp.zeros((1, r), F32))
    carry = lax.fori_loop(0, n_full, functools.partial(slc_tile, causal=False), init)
    m, l = slc_tile(n_full, carry, True)
    o_slc = jnp.where(m > 0.5 * NEG, acc_ref[...] / l, 0.0)

    n_w = NSA_WINDOW // tq + 1
    k_parts, v_parts = [], []
    for d in range(n_w):
        start = jnp.maximum(t0 - NSA_WINDOW + d * tq, 0)
        k_parts.append(kw_ref[pl.ds(pl.multiple_of(start, tq), tq), :])
        v_parts.append(vwt_ref[0, jnp.maximum(i - (n_w - 1) + d, 0)])
    kcat = jnp.concatenate(k_parts, axis=0)
    sw = jnp.dot(kcat, qt, preferred_element_type=F32) * c2
    spos = t0 - NSA_WINDOW + lax.broadcasted_iota(jnp.int32, (n_w * tq, r), 0)
    dt = t_row - spos
    pw = _masked_softmax_cols(sw, (dt >= 0) & (dt <= NSA_WINDOW) & (spos >= 0))
    o_win = jnp.dot(jnp.concatenate(v_parts, axis=1), pw.astype(BF16), preferred_element_type=F32)

    for h in range(NSA_HPG):
        sl = slice(h * tq, (h + 1) * tq)
        o = (gt_ref[3 * h:3 * h + 1, :] * o_cmp[:, sl] + gt_ref[3 * h + 1:3 * h + 2, :] * o_slc[:, sl]
             + gt_ref[3 * h + 2:3 * h + 3, :] * o_win[:, sl])
        o_ref[:, h * NSA_DV:(h + 1) * NSA_DV] = o.T


def _nsa_prompt(qt, kc, vct, ks, vst, kw, vwt, gt, tq, tk):
    t = ks.shape[0]
    nb = t // NSA_BLOCK
    r = NSA_HPG * tq
    resident = dict(pipeline_mode=pl.Buffered(1))
    return pl.pallas_call(
        functools.partial(_nsa_prompt_kernel, tq=tq, tk=tk, n_keep=min(NSA_N_SEL, nb)),
        grid=(NSA_GROUPS, t // tq),
        in_specs=[pl.BlockSpec((NSA_HPG, DKP, tq), lambda g, i: (g, 0, i)),
                  pl.BlockSpec((1, nb, DKP), lambda g, i: (g, 0, 0)),
                  pl.BlockSpec((1, NSA_DV, nb), lambda g, i: (g, 0, 0)),
                  pl.BlockSpec((t, DKP), lambda g, i: (0, g), **resident),
                  pl.BlockSpec((1, t // tk, NSA_DV, tk), lambda g, i: (g, 0, 0, 0), **resident),
                  pl.BlockSpec((t, DKP), lambda g, i: (0, g), **resident),
                  pl.BlockSpec((1, t // tq, NSA_DV, tq), lambda g, i: (g, 0, 0, 0), **resident),
                  pl.BlockSpec((16, tq), lambda g, i: (g, i))],
        out_specs=pl.BlockSpec((tq, NSA_HPG * NSA_DV), lambda g, i: (i, g)),
        out_shape=jax.ShapeDtypeStruct((t, NSA_WIDTH), F32),
        scratch_shapes=[pltpu.VMEM((nb, r), F32), pltpu.VMEM((NSA_DV, r), F32)],
        compiler_params=_cparams(("arbitrary", "arbitrary")),
        name="nsa_prompt",
    )(qt, kc, vct, ks, vst, kw, vwt, gt)


def _split3(x):
    hi = x.astype(BF16)
    r1 = x - hi.astype(F32)
    mid = r1.astype(BF16)
    lo = (r1 - mid.astype(F32)).astype(BF16)
    return jnp.concatenate([hi, mid, lo], axis=1)


def _gla_kernel(q_ref, k_ref, v_ref, a_ref, wa_ref, ba_ref, s0_ref, o_ref, sout_ref, st_ref,
                *, c, n_sub, n_valid):
    ci = pl.program_id(2)
    levels = int(math.log2(c))

    @pl.when(ci == 0)
    def _():
        st_ref[...] = s0_ref[0, 0].T

    t_io = lax.broadcasted_iota(jnp.int32, (c, c), 0)
    s_io = lax.broadcasted_iota(jnp.int32, (c, c), 1)
    row = lax.broadcasted_iota(jnp.int32, (c, 1), 0)
    sels = [s_io <= t_io]
    for lv in range(1, levels + 1):
        sels.append(s_io <= ((t_io >> lv) << lv) + (1 << (lv - 1)) - 1)
    sel_all = jnp.concatenate(sels, axis=0).astype(F32).astype(BF16)
    dk = GLA_DK
    for u in range(n_sub):
        rows = slice(u * c, (u + 1) * c)
        q = q_ref[rows, :] * (GLA_DK ** -0.5)
        k = k_ref[rows, :]
        v = v_ref[rows, :].astype(BF16)
        z = jnp.dot(a_ref[rows, :].astype(BF16), wa_ref[...], preferred_element_type=F32) + ba_ref[...]
        la = (jnp.minimum(z, 0.0) - jnp.log1p(jnp.exp(-jnp.abs(z)))) * (1.0 / GLA_TAU)
        if n_valid < c:
            la = jnp.where(row < n_valid, la, 0.0)
        big = jnp.dot(sel_all, _split3(la), preferred_element_type=F32)
        big = big[:, 0:dk] + big[:, dk:2 * dk] + big[:, 2 * dk:3 * dk]
        b = big[0:c]
        st = st_ref[...]
        o = lax.dot_general((q * jnp.exp(b)).astype(BF16), st.astype(BF16),
                            (((1,), (1,)), ((), ())), preferred_element_type=F32)
        a_mat = jnp.where(t_io == s_io, jnp.sum(q * k, axis=1, keepdims=True), 0.0)
        for lv in range(1, levels + 1):
            bref = big[lv * c:(lv + 1) * c]
            upper = ((row >> (lv - 1)) & 1) == 1
            ql = jnp.where(upper, q * jnp.exp(b - bref), 0.0).astype(BF16)
            kl = jnp.where(upper, 0.0, k * jnp.exp(bref - b)).astype(BF16)
            al = lax.dot_general(ql, kl, (((1,), (1,)), ((), ())), preferred_element_type=F32)
            a_mat = a_mat + jnp.where((t_io >> lv) == (s_io >> lv), al, 0.0)
        o = o + jnp.dot(a_mat.astype(BF16), v, preferred_element_type=F32)
        o_ref[rows, :] = o
        b_last = b[c - 1:c, :]
        kd = (k * jnp.exp(b_last - b)).astype(BF16)
        st_ref[...] = st * jnp.exp(b_last) + lax.dot_general(
            v, kd, (((0,), (0,)), ((), ())), preferred_element_type=F32)

    @pl.when(ci == pl.num_programs(2) - 1)
    def _():
        sout_ref[0, 0] = st_ref[...].T


def _gla(p, n_batch, t_len, s0, wa_pad, ba, c, n_sub, n_valid):
    step = c * n_sub
    ncs = t_len // step
    qb, kb = _SEG["q_g"] // GLA_DK, _SEG["k_g"] // GLA_DK
    vb, ab = _SEG["v_g"] // GLA_DV, _SEG["a_g"] // 256
    return pl.pallas_call(
        functools.partial(_gla_kernel, c=c, n_sub=n_sub, n_valid=n_valid),
        grid=(n_batch, GLA_HEADS, ncs),
        in_specs=[pl.BlockSpec((step, GLA_DK), lambda b, h, ci: (b * ncs + ci, qb + h)),
                  pl.BlockSpec((step, GLA_DK), lambda b, h, ci: (b * ncs + ci, kb + h)),
                  pl.BlockSpec((step, GLA_DV), lambda b, h, ci: (b * ncs + ci, vb + h)),
                  pl.BlockSpec((step, 256), lambda b, h, ci: (b * ncs + ci, ab)),
                  pl.BlockSpec((256, GLA_DK), lambda b, h, ci: (0, h)),
                  pl.BlockSpec((1, GLA_DK), lambda b, h, ci: (0, h)),
                  pl.BlockSpec((1, 1, GLA_DK, GLA_DV), lambda b, h, ci: (b, h, 0, 0))],
        out_specs=[pl.BlockSpec((step, GLA_DV), lambda b, h, ci: (b * ncs + ci, h)),
                   pl.BlockSpec((1, 1, GLA_DK, GLA_DV), lambda b, h, ci: (b, h, 0, 0))],
        out_shape=[jax.ShapeDtypeStruct((n_batch * t_len, GLA_WIDTH), F32),
                   jax.ShapeDtypeStruct((n_batch, GLA_HEADS, GLA_DK, GLA_DV), F32)],
        scratch_shapes=[pltpu.VMEM((GLA_DV, GLA_DK), F32)],
        compiler_params=_cparams(("arbitrary", "arbitrary", "arbitrary")),
        name="gla",
    )(p, p, p, p, wa_pad, ba, s0)


def _merge1_kernel(on_ref, zn_ref, og_ref, zg_ref, gg_ref, wa_ref, wb_ref, ma_ref, mb_ref, o_ref,
                   a_ref, b_ref):
    @pl.when(pl.program_id(1) == 0)
    def _():
        a_ref[...] = (on_ref[...] * _silu(zn_ref[...])).astype(BF16)
        og = og_ref[...]
        zg = zg_ref[...]
        for h in range(GLA_HEADS):
            sl = slice(h * GLA_DV, (h + 1) * GLA_DV)
            x = og[:, sl]
            y = x * lax.rsqrt(jnp.mean(x * x, axis=-1, keepdims=True) + NORM_EPS) * gg_ref[...]
            b_ref[:, sl] = (y * _silu(zg[:, sl])).astype(BF16)

    ua = jnp.dot(a_ref[...], wa_ref[...], preferred_element_type=F32)
    ub = jnp.dot(b_ref[...], wb_ref[...], preferred_element_type=F32)
    o_ref[...] = (jax.nn.sigmoid(ma_ref[...]) * ua + jax.nn.sigmoid(mb_ref[...]) * ub).astype(BF16)


def _merge1(o_nsa, o_gla, p, gla_g, wb_a, wb_b, tm):
    m = p.shape[0]
    tn = 512
    zn, zg, mg = _SEG["z_nsa"] // NSA_WIDTH, _SEG["z_g"] // GLA_WIDTH, _SEG["m_gate"] // tn
    row = lambda i, j: (i, 0)
    return pl.pallas_call(
        _merge1_kernel,
        grid=(m // tm, D_MODEL // tn),
        in_specs=[pl.BlockSpec((tm, NSA_WIDTH), row),
                  pl.BlockSpec((tm, NSA_WIDTH), lambda i, j: (i, zn)),
                  pl.BlockSpec((tm, GLA_WIDTH), row),
                  pl.BlockSpec((tm, GLA_WIDTH), lambda i, j: (i, zg)),
                  pl.BlockSpec((1, GLA_DV), lambda i, j: (0, 0)),
                  pl.BlockSpec((NSA_WIDTH, tn), lambda i, j: (0, j)),
                  pl.BlockSpec((GLA_WIDTH, tn), lambda i, j: (0, j)),
                  pl.BlockSpec((tm, tn), lambda i, j: (i, mg + j)),
                  pl.BlockSpec((tm, tn), lambda i, j: (i, mg + D_MODEL // tn + j))],
        out_specs=pl.BlockSpec((tm, tn), lambda i, j: (i, j)),
        out_shape=jax.ShapeDtypeStruct((m, D_MODEL), BF16),
        scratch_shapes=[pltpu.VMEM((tm, NSA_WIDTH), BF16), pltpu.VMEM((tm, GLA_WIDTH), BF16)],
        compiler_params=_cparams(("arbitrary", "arbitrary")),
        name="merge1",
    )(o_nsa, p, o_gla, p, gla_g, wb_a, wb_b, p, p)


def _merge2_kernel(x_ref, gate_ref, mg_ref, w_ref, o_ref):
    o_ref[...] = x_ref[...] + gate_ref[...] * jnp.dot(mg_ref[...], w_ref[...], preferred_element_type=F32)


def _merge2(x, gate, merged, w_out, tm):
    m, d = x.shape
    tn = 512
    per_row = gate.shape[0] != 1
    gate_spec = (pl.BlockSpec((tm, tn), lambda i, j: (i, j)) if per_row
                 else pl.BlockSpec((1, tn), lambda i, j: (0, j)))
    return pl.pallas_call(
        _merge2_kernel,
        grid=(m // tm, d // tn),
        in_specs=[pl.BlockSpec((tm, tn), lambda i, j: (i, j)), gate_spec,
                  pl.BlockSpec((tm, d), lambda i, j: (i, 0)),
                  pl.BlockSpec((d, tn), lambda i, j: (0, j))],
        out_specs=pl.BlockSpec((tm, tn), lambda i, j: (i, j)),
        out_shape=jax.ShapeDtypeStruct((m, d), F32),
        compiler_params=_cparams(("arbitrary", "arbitrary")),
        name="merge2",
    )(x, gate, merged, w_out)


def _pad_last(a, width):
    return jnp.pad(a, [(0, 0)] * (a.ndim - 1) + [(0, width - a.shape[-1])])


def _pad_heads(w, n, src, dst):
    return _pad_last(w.reshape(w.shape[0], n, src), dst).reshape(w.shape[0], n * dst)


def _pad_w_in(w_in):
    sizes = (NSA_HEADS * NSA_DK, NSA_GROUPS * NSA_DK, NSA_GROUPS * NSA_DV, NSA_GROUPS * NSA_DK,
             NSA_GROUPS * NSA_DV, NSA_GROUPS * NSA_DK, NSA_GROUPS * NSA_DV, 3 * NSA_HEADS, NSA_WIDTH,
             GLA_HEADS * GLA_DK, GLA_HEADS * GLA_DK, GLA_WIDTH, GLA_WIDTH, GLA_RANK, 2 * D_MODEL)
    cuts = [0]
    for s in sizes:
        cuts.append(cuts[-1] + s)
    names = ("q", "k_cmp", "v_cmp", "k_slc", "v_slc", "k_win", "v_win", "g_nsa", "z_nsa",
             "q_g", "k_g", "v_g", "z_g", "a_g", "m_gate")
    parts = {nm: w_in[:, cuts[n]:cuts[n + 1]] for n, nm in enumerate(names)}
    parts["q"] = _pad_heads(parts["q"], NSA_HEADS, NSA_DK, DKP)
    for nm in ("k_cmp", "k_slc", "k_win"):
        parts[nm] = _pad_heads(parts[nm], NSA_GROUPS, NSA_DK, DKP)
    parts["g_nsa"] = _pad_last(_pad_heads(parts["g_nsa"], NSA_GROUPS, 3 * NSA_HPG, 16), 256)
    parts["a_g"] = _pad_last(parts["a_g"], 256)
    return jnp.concatenate([parts[nm] for nm in _SEG], axis=1).astype(BF16)


def _unpad_heads(a, n):
    return a.reshape(a.shape[0], n, DKP)[:, :, :NSA_DK]


def _rms_norm(x, g):
    xf = x.astype(F32)
    return xf * lax.rsqrt(jnp.mean(xf * xf, axis=-1, keepdims=True) + NORM_EPS) * g


def _masked_softmax(s, mask):
    s = jnp.where(mask, s.astype(F32), -jnp.inf)
    m = jnp.max(s, axis=-1, keepdims=True)
    m = jnp.where(jnp.isfinite(m), m, 0.0)
    e = jnp.where(mask, jnp.exp(s - m), 0.0)
    return e / jnp.maximum(jnp.sum(e, axis=-1, keepdims=True), 1e-30)


def _nsa_compress(rows, pe, w1, w2):
    lead = rows.shape[:-3]
    L, G, Dh = rows.shape[-3:]
    blk = rows.reshape(lead + (L // NSA_BLOCK, NSA_BLOCK, G, Dh)) + pe[:, None, :]
    hid = jax.nn.silu(jnp.einsum('...nsgd,sdh->...ngh', blk, w1))
    return jnp.einsum('...ngh,hd->...ngd', hid, w2)


def _nsa_sample_jnp(q, g_nsa, k_slc_new, v_slc_new, k_win_new, v_win_new,
                    cache_k_cmp, cache_v_cmp, cache_k_slc, cache_v_slc, k_win_buf, v_win_buf, page_table,
                    pe_k, w1_k, w2_k, pe_v, w1_v, w2_v, k_cmp_norm_g):
    DB, T = q.shape[:2]
    G, P, DK, DV, SB = NSA_GROUPS, NSA_HPG, NSA_DK, NSA_DV, NSA_BLOCK
    past = page_table.shape[1] * PAGE_SIZE
    t = past + jnp.arange(T)

    def compress_seq(pt_row):
        kr = cache_k_cmp[0, pt_row].reshape(past, G, DK)
        vr = cache_v_cmp[0, pt_row].reshape(past, G, DV)
        return (_rms_norm(_nsa_compress(kr, pe_k, w1_k, w2_k), k_cmp_norm_g),
                _nsa_compress(vr, pe_v, w1_v, w2_v))

    kc, vc = lax.map(compress_seq, page_table)
    s = jnp.einsum('btgpd,bngd->btgpn', q, kc) * DK ** -0.5
    mask = jnp.arange(kc.shape[1])[None, :] < ((t + 1) // SB)[:, None]
    p = _masked_softmax(s, mask[None, :, None, None, :])
    o_cmp = jnp.einsum('btgpn,bngd->btgpd', p, vc)
    score = jnp.sum(p, axis=3)
    n_keep = min(NSA_N_SEL - 1, past // SB)
    j = jnp.arange(score.shape[-1])[None, :]
    cb = (t // SB)[:, None]
    allowed = j <= cb
    forced = allowed & ((j == 0) | (j == cb) | (j == cb - 1))
    val = jnp.where(forced[None, :, None, :], jnp.inf, jnp.where(allowed[None, :, None, :], score, -jnp.inf))
    top, idx = lax.top_k(val, n_keep)
    valid = top > -jnp.inf
    bpp = PAGE_SIZE // SB
    page = page_table[jnp.arange(DB)[:, None, None, None], idx // bpp][..., None]
    off = ((idx % bpp) * SB)[..., None] + jnp.arange(SB)
    gi = jnp.arange(G)[None, None, :, None, None]
    n_sel = n_keep * SB
    k_sel = cache_k_slc[0, page, off, gi].reshape(DB, T, G, n_sel, DK)
    v_sel = cache_v_slc[0, page, off, gi].reshape(DB, T, G, n_sel, DV)
    s = jnp.concatenate([jnp.einsum('btgpd,btgnd->btgpn', q, k_sel),
                         jnp.einsum('btgpd,bngd->btgpn', q, k_slc_new)], axis=-1) * DK ** -0.5
    m_past = jnp.broadcast_to(valid[:, :, :, None, :, None], (DB, T, G, P, n_keep, SB)).reshape(DB, T, G, P, n_sel)
    causal = jnp.arange(T)[None, :] <= jnp.arange(T)[:, None]
    m_new = jnp.broadcast_to(causal[None, :, None, None, :], (DB, T, G, P, T))
    p = _masked_softmax(s, jnp.concatenate([m_past, m_new], axis=-1))
    o_slc = (jnp.einsum('btgpn,btgnd->btgpd', p[..., :n_sel], v_sel)
             + jnp.einsum('btgpn,bngd->btgpd', p[..., n_sel:], v_slc_new))
    W = k_win_buf.shape[1]
    kw = jnp.concatenate([k_win_buf, k_win_new], axis=1)
    vw = jnp.concatenate([v_win_buf, v_win_new], axis=1)
    sp = past - W + jnp.arange(W + T)
    dt = t[:, None] - sp[None, :]
    wmask = (dt >= 0) & (dt <= NSA_WINDOW) & (sp[None, :] >= 0)
    s = jnp.einsum('btgpd,bngd->btgpn', q, kw) * DK ** -0.5
    p = _masked_softmax(s, wmask[None, :, None, None, :])
    o_win = jnp.einsum('btgpn,bngd->btgpd', p, vw)
    o = g_nsa[..., 0:1] * o_cmp + g_nsa[..., 1:2] * o_slc + g_nsa[..., 2:3] * o_win
    return o, kw[:, T:], vw[:, T:]


def kernel(x_prompt, x_sample, cache_k_cmp, cache_v_cmp, cache_k_slc, cache_v_slc, state_k_win, state_v_win, state_gla, page_table, c_prompt, c_sample, ln_g, w_ada, b_ada, w_in, q_norm_g, k_cmp_norm_g, k_slc_norm_g, k_win_norm_g, pe_k, w1_k, w2_k, pe_v, w1_v, w2_v, w_a2, b_a, gla_norm_g, w_branch, w_out):
    assert ln_g.shape[0] == 1, "single layer"
    bp, t, d = x_prompt.shape
    db, ts, _ = x_sample.shape
    assert bp == 1 and d == D_MODEL
    G = NSA_GROUPS

    w_pad = _pad_w_in(w_in[0])
    g_q = _pad_last(q_norm_g, DKP)
    g_kc = _pad_last(k_cmp_norm_g, DKP)
    g_ks = _pad_last(k_slc_norm_g, DKP)
    g_kw = _pad_last(k_win_norm_g, DKP)
    pe_k_p = _pad_last(pe_k[0], DKP)[:, None, :]
    w1_k_p = jnp.pad(w1_k[0], ((0, 0), (0, DKP - NSA_DK), (0, 0))).astype(BF16)
    w2_k_p = _pad_last(w2_k[0], DKP).astype(BF16)
    pe_v_p = pe_v[0][:, None, :]
    w1_v_p = w1_v[0].astype(BF16)
    w2_v_p = w2_v[0].astype(BF16)
    wa_pad = jnp.pad(w_a2[0], ((0, 256 - GLA_RANK), (0, 0))).astype(BF16)
    wb_a = w_branch[0, :NSA_WIDTH].astype(BF16)
    wb_b = w_branch[0, NSA_WIDTH:].astype(BF16)
    w_o = w_out[0].astype(BF16)
    ones_v = jnp.ones((1, NSA_DV), F32)

    n_c = bp + db
    c_all = jnp.pad(jnp.concatenate([c_prompt, c_sample], axis=0), ((0, (-n_c) % 8), (0, 0)))
    mod = _ada(c_all, w_ada[0], b_ada)
    shift, scale, gate = mod[:, :d], mod[:, d:2 * d], mod[:, 2 * d:]

    tm = min(1024, t)
    xp = x_prompt.reshape(t, d)
    pp = _proj(xp, scale[0:1], shift[0:1], ln_g, w_pad, tm)
    tq = 128
    tk = min(512, t)
    qt = _prep_q(pp, g_q, min(512, t))
    ks_f, ks_b = _prep_k(pp, "k_slc", g_ks, min(512, t))
    kw_f, kw_b = _prep_k(pp, "k_win", g_kw, min(512, t))
    vst = _prep_vt(pp, "v_slc", tk)
    vwt = _prep_vt(pp, "v_win", tq)
    gt = _prep_gate(pp, min(512, t))
    kc = _compress_prompt(pp, "k_cmp", DKP, pe_k_p, w1_k_p, w2_k_p, g_kc, True, False)
    vct = _compress_prompt(pp, "v_cmp", NSA_DV, pe_v_p, w1_v_p, w2_v_p, ones_v, False, True)
    o_nsa_p = _nsa_prompt(qt, kc, vct, ks_b, vst, kw_b, vwt, gt, tq, tk)
    s0_p = jnp.zeros((bp, GLA_HEADS, GLA_DK, GLA_DV), F32)
    o_gla_p, gla_p = _gla(pp, bp, t, s0_p, wa_pad, b_a, GLA_CHUNK, 4, GLA_CHUNK)
    merged_p = _merge1(o_nsa_p, o_gla_p, pp, gla_norm_g, wb_a, wb_b, min(512, t))
    y_p = _merge2(xp, gate[0:1], merged_p, w_o, min(512, t)).reshape(bp, t, d)

    wp = min(NSA_WINDOW, t)
    k_cmp_p = _unpad_heads(pp[:, _SEG["k_cmp"]:_SEG["k_cmp"] + G * DKP], G)
    v_cmp_p = pp[:, _SEG["v_cmp"]:_SEG["v_cmp"] + G * NSA_DV].reshape(t, G, NSA_DV)
    k_slc_p = _unpad_heads(ks_f, G)
    v_slc_p = pp[:, _SEG["v_slc"]:_SEG["v_slc"] + G * NSA_DV].reshape(t, G, NSA_DV)
    k_win_p = _unpad_heads(kw_f, G)[t - wp:]
    v_win_p = pp[t - wp:, _SEG["v_win"]:_SEG["v_win"] + G * NSA_DV].reshape(wp, G, NSA_DV)

    ms = db * ts
    xs = x_sample.reshape(ms, d)
    rep = lambda a: jnp.repeat(a[bp:bp + db], ts, axis=0)
    ps = _proj(xs, rep(scale), rep(shift), ln_g, w_pad, ms)
    seg = lambda name, w: ps[:, _SEG[name]:_SEG[name] + w]
    q_s = _rms_norm(_unpad_heads(seg("q", NSA_HEADS * DKP), NSA_HEADS), q_norm_g[0])
    q_s = q_s.reshape(db, ts, G, NSA_HPG, NSA_DK)
    k_cmp_s = _unpad_heads(seg("k_cmp", G * DKP), G).reshape(db, ts, G, NSA_DK)
    v_cmp_s = seg("v_cmp", G * NSA_DV).reshape(db, ts, G, NSA_DV)
    k_slc_s = _rms_norm(_unpad_heads(seg("k_slc", G * DKP), G), k_slc_norm_g[0]).reshape(db, ts, G, NSA_DK)
    v_slc_s = seg("v_slc", G * NSA_DV).reshape(db, ts, G, NSA_DV)
    k_win_s_new = _rms_norm(_unpad_heads(seg("k_win", G * DKP), G), k_win_norm_g[0]).reshape(db, ts, G, NSA_DK)
    v_win_s_new = seg("v_win", G * NSA_DV).reshape(db, ts, G, NSA_DV)
    g_s = jax.nn.sigmoid(seg("g_nsa", 64).reshape(db, ts, G, 16)[..., :12].reshape(db, ts, G, NSA_HPG, 3))
    o_nsa_s, kw_s, vw_s = _nsa_sample_jnp(
        q_s, g_s, k_slc_s, v_slc_s, k_win_s_new, v_win_s_new, cache_k_cmp, cache_v_cmp, cache_k_slc, cache_v_slc,
        state_k_win[0], state_v_win[0], page_table, pe_k[0], w1_k[0], w2_k[0], pe_v[0], w1_v[0], w2_v[0],
        k_cmp_norm_g[0])
    c_s = 16
    ps_pad = jnp.pad(ps.reshape(db, ts, D_PAD), ((0, 0), (0, c_s - ts), (0, 0))).reshape(db * c_s, D_PAD)
    o_gla_s, gla_s = _gla(ps_pad, db, c_s, state_gla[0], wa_pad, b_a, c_s, 1, ts)
    o_gla_s = o_gla_s.reshape(db, c_s, GLA_WIDTH)[:, :ts].reshape(ms, GLA_WIDTH)
    merged_s = _merge1(o_nsa_s.reshape(ms, NSA_WIDTH), o_gla_s, ps, gla_norm_g, wb_a, wb_b, ms)
    y_s = _merge2(xs, rep(gate), merged_s, w_o, ms).reshape(db, ts, d)

    L1 = lambda a: a[None, None]
    return (y_p, y_s,
            L1(k_cmp_p), L1(v_cmp_p), L1(k_slc_p), L1(v_slc_p), L1(k_win_p), L1(v_win_p), gla_p[None],
            k_cmp_s[None], v_cmp_s[None], k_slc_s[None], v_slc_s[None], kw_s[None], vw_s[None], gla_s[None])
```

```python
import functools
import math

import jax
import jax.numpy as jnp
from jax import lax
from jax.experimental import pallas as pl
from jax.experimental.pallas import tpu as pltpu

F32 = jnp.float32
BF16 = jnp.bfloat16

NSA_HEADS = 16
NSA_GROUPS = 4
NSA_HPG = NSA_HEADS // NSA_GROUPS
NSA_DK = 192
NSA_DV = 128
NSA_BLOCK = 64
NSA_N_SEL = 16
NSA_WINDOW = 512
NSA_CMP_HIDDEN = 256
PAGE_SIZE = 128
GLA_HEADS = 4
GLA_DK = 256
GLA_DV = 512
GLA_RANK = 16
GLA_TAU = 16.0
GLA_CHUNK = 64
NORM_EPS = 1e-6
D_MODEL = 2048
NSA_WIDTH = NSA_HEADS * NSA_DV
GLA_WIDTH = GLA_HEADS * GLA_DV

DKP = 256
LOG2E = 1.4426950408889634
NEG = -1e30
VMEM_LIMIT = 56 * 1024 * 1024

_SEG = {}
_off = 0
for _name, _w in (("z_nsa", NSA_WIDTH), ("z_g", GLA_WIDTH), ("v_g", GLA_WIDTH), ("m_gate", 2 * D_MODEL),
                  ("q", NSA_HEADS * DKP), ("k_cmp", NSA_GROUPS * DKP), ("v_cmp", NSA_GROUPS * NSA_DV),
                  ("k_slc", NSA_GROUPS * DKP), ("v_slc", NSA_GROUPS * NSA_DV),
                  ("k_win", NSA_GROUPS * DKP), ("v_win", NSA_GROUPS * NSA_DV),
                  ("q_g", GLA_HEADS * GLA_DK), ("k_g", GLA_HEADS * GLA_DK),
                  ("g_nsa", 256), ("a_g", 256)):
    _SEG[_name] = _off
    _off += _w
D_PAD = _off


def _cparams(sem):
    return pltpu.CompilerParams(dimension_semantics=sem, vmem_limit_bytes=VMEM_LIMIT)


def _silu(x):
    return x * jax.nn.sigmoid(x)


def _ada_kernel(c_ref, w_ref, b_ref, o_ref):
    a = _silu(c_ref[...]).astype(BF16)
    o_ref[...] = jnp.dot(a, w_ref[...].astype(BF16), preferred_element_type=F32) + b_ref[...]


def _ada(c_all, w_ada, b_ada):
    m, d = c_all.shape
    n = w_ada.shape[1]
    tn = 768
    return pl.pallas_call(
        _ada_kernel,
        grid=(n // tn,),
        in_specs=[pl.BlockSpec((m, d), lambda j: (0, 0)),
                  pl.BlockSpec((d, tn), lambda j: (0, j)),
                  pl.BlockSpec((1, tn), lambda j: (0, j))],
        out_specs=pl.BlockSpec((m, tn), lambda j: (0, j)),
        out_shape=jax.ShapeDtypeStruct((m, n), F32),
        compiler_params=_cparams(("arbitrary",)),
        name="ada",
    )(c_all, w_ada, b_ada)


def _proj_kernel(x_ref, sc_ref, sh_ref, g_ref, w_ref, o_ref, h_ref):
    @pl.when(pl.program_id(1) == 0)
    def _():
        x = x_ref[...]
        y = x * lax.rsqrt(jnp.mean(x * x, axis=-1, keepdims=True) + NORM_EPS) * g_ref[...]
        h_ref[...] = (y * (1.0 + sc_ref[...]) + sh_ref[...]).astype(BF16)

    o_ref[...] = jnp.dot(h_ref[...], w_ref[...], preferred_element_type=F32)


def _proj(x, scale, shift, ln_g, w_pad, tm):
    m, d = x.shape
    n = w_pad.shape[1]
    tn = 1536
    per_row = scale.shape[0] != 1
    mod_spec = (pl.BlockSpec((tm, d), lambda i, j: (i, 0)) if per_row
                else pl.BlockSpec((1, d), lambda i, j: (0, 0)))
    return pl.pallas_call(
        _proj_kernel,
        grid=(m // tm, n // tn),
        in_specs=[pl.BlockSpec((tm, d), lambda i, j: (i, 0)), mod_spec, mod_spec,
                  pl.BlockSpec((1, d), lambda i, j: (0, 0)),
                  pl.BlockSpec((d, tn), lambda i, j: (0, j))],
        out_specs=pl.BlockSpec((tm, tn), lambda i, j: (i, j)),
        out_shape=jax.ShapeDtypeStruct((m, n), F32),
        scratch_shapes=[pltpu.VMEM((tm, d), BF16)],
        compiler_params=_cparams(("arbitrary", "arbitrary")),
        name="proj",
    )(x, scale, shift, ln_g, w_pad)


def _head_norm(x, g):
    ms = jnp.sum(x * x, axis=-1, keepdims=True) * (1.0 / NSA_DK)
    return x * lax.rsqrt(ms + NORM_EPS) * g


def _prep_q_kernel(x_ref, g_ref, o_ref):
    o_ref[0] = _head_norm(x_ref[...], g_ref[...]).T.astype(BF16)


def _prep_q(p, g_pad, tm):
    t = p.shape[0]
    base = _SEG["q"] // DKP
    return pl.pallas_call(
        _prep_q_kernel,
        grid=(t // tm, NSA_HEADS),
        in_specs=[pl.BlockSpec((tm, DKP), lambda i, h: (i, base + h)),
                  pl.BlockSpec((1, DKP), lambda i, h: (0, 0))],
        out_specs=pl.BlockSpec((1, DKP, tm), lambda i, h: (h, 0, i)),
        out_shape=jax.ShapeDtypeStruct((NSA_HEADS, DKP, t), BF16),
        compiler_params=_cparams(("arbitrary", "arbitrary")),
        name="prep_q",
    )(p, g_pad)


def _prep_k_kernel(x_ref, g_ref, o_ref, ob_ref):
    y = _head_norm(x_ref[...], g_ref[...])
    o_ref[...] = y
    ob_ref[...] = y.astype(BF16)


def _prep_k(p, seg, g_pad, tm):
    t = p.shape[0]
    base = _SEG[seg] // DKP
    n = NSA_GROUPS * DKP
    return pl.pallas_call(
        _prep_k_kernel,
        grid=(t // tm, NSA_GROUPS),
        in_specs=[pl.BlockSpec((tm, DKP), lambda i, g: (i, base + g)),
                  pl.BlockSpec((1, DKP), lambda i, g: (0, 0))],
        out_specs=[pl.BlockSpec((tm, DKP), lambda i, g: (i, g)),
                   pl.BlockSpec((tm, DKP), lambda i, g: (i, g))],
        out_shape=[jax.ShapeDtypeStruct((t, n), F32), jax.ShapeDtypeStruct((t, n), BF16)],
        compiler_params=_cparams(("arbitrary", "arbitrary")),
        name="prep_k_" + seg,
    )(p, g_pad)


def _prep_vt_kernel(x_ref, o_ref):
    o_ref[0, 0] = x_ref[...].T.astype(BF16)


def _prep_vt(p, seg, tile):
    t = p.shape[0]
    base = _SEG[seg] // NSA_DV
    return pl.pallas_call(
        _prep_vt_kernel,
        grid=(t // tile, NSA_GROUPS),
        in_specs=[pl.BlockSpec((tile, NSA_DV), lambda i, g: (i, base + g))],
        out_specs=pl.BlockSpec((1, 1, NSA_DV, tile), lambda i, g: (g, i, 0, 0)),
        out_shape=jax.ShapeDtypeStruct((NSA_GROUPS, t // tile, NSA_DV, tile), BF16),
        compiler_params=_cparams(("arbitrary", "arbitrary")),
        name="prep_vt_" + seg,
    )(p)


def _prep_gate_kernel(x_ref, o_ref):
    o_ref[...] = jax.nn.sigmoid(x_ref[...]).T[:4 * NSA_HEADS, :]


def _prep_gate(p, tm):
    t = p.shape[0]
    base = _SEG["g_nsa"] // 256
    return pl.pallas_call(
        _prep_gate_kernel,
        grid=(t // tm,),
        in_specs=[pl.BlockSpec((tm, 256), lambda i: (i, base))],
        out_specs=pl.BlockSpec((4 * NSA_HEADS, tm), lambda i: (0, i)),
        out_shape=jax.ShapeDtypeStruct((4 * NSA_HEADS, t), F32),
        compiler_params=_cparams(("arbitrary",)),
        name="prep_gate",
    )(p)


_CMP_ROWS = 8


def _cmp_kernel(x_ref, pe_ref, w1_ref, w2_ref, g_ref, o_ref, acc_ref, *, norm, transpose_out):
    s = pl.program_id(1)

    @pl.when(s == 0)
    def _():
        acc_ref[...] = jnp.zeros_like(acc_ref)

    acc = acc_ref[...]
    for r in range(_CMP_ROWS):
        xb = (x_ref[:, r, :] + pe_ref[r]).astype(BF16)
        acc = acc + jnp.dot(xb, w1_ref[r], preferred_element_type=F32)
    acc_ref[...] = acc

    @pl.when(s == pl.num_programs(1) - 1)
    def _():
        hid = _silu(acc_ref[...]).astype(BF16)
        y = jnp.dot(hid, w2_ref[...], preferred_element_type=F32)
        if norm:
            y = _head_norm(y, g_ref[...])
        if transpose_out:
            y = y.T
        o_ref[0] = y.astype(o_ref.dtype)


def _compress_prompt(p, seg, dp, pe_pad, w1_pad, w2_pad, g_pad, norm, transpose_out):
    t = p.shape[0]
    nb = t // NSA_BLOCK
    pv = p.reshape(nb, NSA_BLOCK, D_PAD)
    base = _SEG[seg] // dp
    out_shape = (NSA_GROUPS, dp, nb) if transpose_out else (NSA_GROUPS, nb, dp)
    return pl.pallas_call(
        functools.partial(_cmp_kernel, norm=norm, transpose_out=transpose_out),
        grid=(NSA_GROUPS, NSA_BLOCK // _CMP_ROWS),
        in_specs=[pl.BlockSpec((nb, _CMP_ROWS, dp), lambda g, s: (0, s, base + g)),
                  pl.BlockSpec((_CMP_ROWS, 1, dp), lambda g, s: (s, 0, 0)),
                  pl.BlockSpec((_CMP_ROWS, dp, NSA_CMP_HIDDEN), lambda g, s: (s, 0, 0)),
                  pl.BlockSpec((NSA_CMP_HIDDEN, dp), lambda g, s: (0, 0)),
                  pl.BlockSpec((1, dp), lambda g, s: (0, 0))],
        out_specs=pl.BlockSpec((1,) + out_shape[1:], lambda g, s: (g, 0, 0)),
        out_shape=jax.ShapeDtypeStruct(out_shape, BF16),
        scratch_shapes=[pltpu.VMEM((nb, NSA_CMP_HIDDEN), F32)],
        compiler_params=_cparams(("arbitrary", "arbitrary")),
        name="cmp_" + seg,
    )(pv, pe_pad, w1_pad, w2_pad, g_pad)


_PAGES_PER_STEP = 8
_STAGE_PITCH = 72


def _cmp_paged_kernel(pt_ref, *refs, dh, m_blocks, norm):
    del pt_ref
    npg = _PAGES_PER_STEP
    pages = refs[:npg]
    pe_ref, w1_ref, w2_ref, g_ref, o_ref, stage_ref, xs_ref, acc_ref = refs[npg:]
    c = pl.program_id(2)
    nbc = 2 * npg
    n_slab = stage_ref.shape[1]

    @pl.when((pl.program_id(0) == 0) & (pl.program_id(1) == 0) & (c == 0))
    def _():
        stage_ref[...] = jnp.zeros_like(stage_ref)

    for k in range(npg):
        for blk in range(2):
            r0 = (2 * k + blk) * _STAGE_PITCH
            for g in range(NSA_GROUPS):
                x = pages[k][0, 0, blk * NSA_BLOCK:(blk + 1) * NSA_BLOCK, g, :]
                stage_ref[g, 0, r0:r0 + NSA_BLOCK, :] = x[:, 0:128]
                if n_slab == 2:
                    stage_ref[g, 1, r0:r0 + NSA_BLOCK, 0:dh - 128] = x[:, 128:dh]
    row0 = pl.multiple_of(c * nbc, nbc)
    for s in range(NSA_BLOCK):
        for g in range(NSA_GROUPS):
            parts = [stage_ref[g, j, pl.ds(s, nbc, stride=_STAGE_PITCH), :] for j in range(n_slab)]
            x = parts[0] if n_slab == 1 else jnp.concatenate(parts, axis=1)
            xs_ref[s, g, pl.ds(row0, nbc), :] = (x + pe_ref[s:s + 1, :]).astype(BF16)

    @pl.when(c == pl.num_programs(2) - 1)
    def _():
        acc_ref[...] = jnp.zeros_like(acc_ref)

        def body(s, carry):
            w = w1_ref[s]
            for g in range(NSA_GROUPS):
                acc_ref[g] += jnp.dot(xs_ref[s, g], w, preferred_element_type=F32)
            return carry

        lax.fori_loop(0, NSA_BLOCK, body, 0)
        for g in range(NSA_GROUPS):
            hid = _silu(acc_ref[g]).astype(BF16)
            y = jnp.dot(hid, w2_ref[...], preferred_element_type=F32)
            if norm:
                y = _head_norm(y, g_ref[...])
            o_ref[0, g] = y.astype(BF16)


def _compress_paged(cache, page_table, pe_pad, w1_pad, w2_pad, g_pad, norm):
    dh = cache.shape[-1]
    dp = pe_pad.shape[-1]
    db, n_pages = page_table.shape
    npg = _PAGES_PER_STEP
    steps = n_pages // npg
    spp = min(8, steps)
    m_blocks = spp * 2 * npg
    n_ph = steps // spp
    n_slab = dp // 128

    def page_spec(k):
        return pl.BlockSpec((1, 1, PAGE_SIZE, NSA_GROUPS, dh),
                            lambda b, ph, c, pt: (0, pt[b, (ph * spp + c) * npg + k], 0, 0, 0))

    const = lambda *shape: pl.BlockSpec(shape, lambda b, ph, c, pt: (0,) * len(shape))
    grid_spec = pltpu.PrefetchScalarGridSpec(
        num_scalar_prefetch=1,
        grid=(db, n_ph, spp),
        in_specs=[page_spec(k) for k in range(npg)] + [
            const(NSA_BLOCK, dp),
            pl.BlockSpec((NSA_BLOCK, dp, NSA_CMP_HIDDEN), lambda b, ph, c, pt: (0, 0, 0),
                         pipeline_mode=pl.Buffered(1)),
            const(NSA_CMP_HIDDEN, dp), const(1, dp)],
        out_specs=pl.BlockSpec((1, NSA_GROUPS, m_blocks, dp), lambda b, ph, c, pt: (b, 0, ph, 0)),
        scratch_shapes=[pltpu.VMEM((NSA_GROUPS, n_slab, 2 * npg * _STAGE_PITCH, 128), F32),
                        pltpu.VMEM((NSA_BLOCK, NSA_GROUPS, m_blocks, dp), BF16),
                        pltpu.VMEM((NSA_GROUPS, m_blocks, NSA_CMP_HIDDEN), F32)])
    return pl.pallas_call(
        functools.partial(_cmp_paged_kernel, dh=dh, m_blocks=m_blocks, norm=norm),
        grid_spec=grid_spec,
        out_shape=jax.ShapeDtypeStruct((db, NSA_GROUPS, n_pages * 2, dp), BF16),
        compiler_params=_cparams(("arbitrary", "arbitrary", "arbitrary")),
        name="cmp_paged_%d" % dh,
    )(page_table, *([cache] * npg), pe_pad, w1_pad, w2_pad, g_pad)


def _masked_softmax_cols(s, valid):
    s = jnp.where(valid, s, NEG)
    m = jnp.max(s, axis=0, keepdims=True)
    e = jnp.where(valid, jnp.exp2(s - m), 0.0)
    return e / jnp.maximum(jnp.sum(e, axis=0, keepdims=True), 1e-30)


def _select_blocks(score, cb, n_keep):
    nb, w = score.shape
    n_q = lax.broadcasted_iota(jnp.int32, (nb, w), 0)
    n_f = n_q.astype(F32)
    allowed = n_q <= cb
    forced = allowed & ((n_q == 0) | (n_q == cb) | (n_q == cb - 1))
    val0 = jnp.where(forced, jnp.inf, jnp.where(allowed, score, -jnp.inf))

    def pick(_, carry):
        val, sel = carry
        mx = jnp.max(val, axis=0, keepdims=True)
        idx = jnp.min(jnp.where(val == mx, n_f, float(nb)), axis=0, keepdims=True)
        hit = n_f == idx
        sel = jnp.where(hit & (mx > -jnp.inf), 1.0, sel)
        return jnp.where(hit, -jnp.inf, val), sel

    _, sel = lax.fori_loop(0, n_keep, pick, (val0, jnp.zeros((nb, w), F32)))
    return sel


def _nsa_prompt_kernel(qt_ref, kc_ref, vct_ref, ks_ref, vst_ref, kw_ref, vwt_ref, gt_ref,
                       o_ref, sel_ref, acc_ref, *, tq, tk, n_keep):
    i = pl.program_id(1)
    r = NSA_HPG * tq
    t0 = i * tq
    c2 = (NSA_DK ** -0.5) * LOG2E
    qt = jnp.concatenate([qt_ref[h] for h in range(NSA_HPG)], axis=1)
    t_row = t0 + (lax.broadcasted_iota(jnp.int32, (1, r), 1) & (tq - 1))

    nb = kc_ref.shape[1]
    s = jnp.dot(kc_ref[0], qt, preferred_element_type=F32) * c2
    n_io = lax.broadcasted_iota(jnp.int32, (nb, r), 0)
    p = _masked_softmax_cols(s, n_io < ((t_row + 1) >> 6))
    o_cmp = jnp.dot(vct_ref[0], p.astype(BF16), preferred_element_type=F32)
    score = p[:, 0:tq]
    for h in range(1, NSA_HPG):
        score = score + p[:, h * tq:(h + 1) * tq]

    cb = (t0 + lax.broadcasted_iota(jnp.int32, (1, tq), 1)) >> 6
    sel = _select_blocks(score, cb, n_keep)
    sel_ref[...] = jnp.concatenate([sel] * NSA_HPG, axis=1)

    nbk = tk // NSA_BLOCK
    acc_ref[...] = jnp.zeros_like(acc_ref)

    def slc_tile(j, carry, causal):
        m, l = carry
        k = ks_ref[pl.ds(pl.multiple_of(j * tk, tk), tk), :]
        bias = (sel_ref[pl.ds(pl.multiple_of(j * nbk, nbk), nbk), :] - 1.0) * (-NEG)
        bias = jnp.broadcast_to(bias[:, None, :], (nbk, NSA_BLOCK, r)).reshape(tk, r)
        sc = jnp.dot(k, qt, preferred_element_type=F32) * c2 + bias
        if causal:
            pos = j * tk + lax.broadcasted_iota(jnp.int32, (tk, r), 0)
            sc = jnp.where(pos <= t_row, sc, NEG)
        m_new = jnp.maximum(m, jnp.max(sc, axis=0, keepdims=True))
        alpha = jnp.exp2(m - m_new)
        pp = jnp.exp2(sc - m_new)
        l_new = l * alpha + jnp.sum(pp, axis=0, keepdims=True)
        pv = jnp.dot(vst_ref[0, j], pp.astype(BF16), preferred_element_type=F32)
        acc_ref[...] = acc_ref[...] * alpha + pv
        return m_new, l_new

    n_full = t0 // tk
    init = (jnp.full((1, r), NEG, F32), jnp.zeros((1, r), F32))
    carry = lax.fori_loop(0, n_full, functools.partial(slc_tile, causal=False), init)
    m, l = slc_tile(n_full, carry, True)
    o_slc = jnp.where(m > 0.5 * NEG, acc_ref[...] / l, 0.0)

    n_w = NSA_WINDOW // tq + 1
    k_parts, v_parts = [], []
    for d in range(n_w):
        start = jnp.maximum(t0 - NSA_WINDOW + d * tq, 0)
        k_parts.append(kw_ref[pl.ds(pl.multiple_of(start, tq), tq), :])
        v_parts.append(vwt_ref[0, jnp.maximum(i - (n_w - 1) + d, 0)])
    kcat = jnp.concatenate(k_parts, axis=0)
    sw = jnp.dot(kcat, qt, preferred_element_type=F32) * c2
    spos = t0 - NSA_WINDOW + lax.broadcasted_iota(jnp.int32, (n_w * tq, r), 0)
    dt = t_row - spos
    pw = _masked_softmax_cols(sw, (dt >= 0) & (dt <= NSA_WINDOW) & (spos >= 0))
    o_win = jnp.dot(jnp.concatenate(v_parts, axis=1), pw.astype(BF16), preferred_element_type=F32)

    for h in range(NSA_HPG):
        sl = slice(h * tq, (h + 1) * tq)
        o = (gt_ref[3 * h:3 * h + 1, :] * o_cmp[:, sl] + gt_ref[3 * h + 1:3 * h + 2, :] * o_slc[:, sl]
             + gt_ref[3 * h + 2:3 * h + 3, :] * o_win[:, sl])
        o_ref[:, h * NSA_DV:(h + 1) * NSA_DV] = o.T


def _nsa_prompt(qt, kc, vct, ks, vst, kw, vwt, gt, tq, tk):
    t = ks.shape[0]
    nb = t // NSA_BLOCK
    r = NSA_HPG * tq
    resident = dict(pipeline_mode=pl.Buffered(1))
    return pl.pallas_call(
        functools.partial(_nsa_prompt_kernel, tq=tq, tk=tk, n_keep=min(NSA_N_SEL, nb)),
        grid=(NSA_GROUPS, t // tq),
        in_specs=[pl.BlockSpec((NSA_HPG, DKP, tq), lambda g, i: (g, 0, i)),
                  pl.BlockSpec((1, nb, DKP), lambda g, i: (g, 0, 0)),
                  pl.BlockSpec((1, NSA_DV, nb), lambda g, i: (g, 0, 0)),
                  pl.BlockSpec((t, DKP), lambda g, i: (0, g), **resident),
                  pl.BlockSpec((1, t // tk, NSA_DV, tk), lambda g, i: (g, 0, 0, 0), **resident),
                  pl.BlockSpec((t, DKP), lambda g, i: (0, g), **resident),
                  pl.BlockSpec((1, t // tq, NSA_DV, tq), lambda g, i: (g, 0, 0, 0), **resident),
                  pl.BlockSpec((16, tq), lambda g, i: (g, i))],
        out_specs=pl.BlockSpec((tq, NSA_HPG * NSA_DV), lambda g, i: (i, g)),
        out_shape=jax.ShapeDtypeStruct((t, NSA_WIDTH), F32),
        scratch_shapes=[pltpu.VMEM((nb, r), F32), pltpu.VMEM((NSA_DV, r), F32)],
        compiler_params=_cparams(("arbitrary", "arbitrary")),
        name="nsa_prompt",
    )(qt, kc, vct, ks, vst, kw, vwt, gt)


def _nsa_sample_kernel(pt_ref, *refs, past, ts, n_keep):
    del pt_ref
    npg = _PAGES_PER_STEP
    kp, vp = refs[:npg], refs[npg:2 * npg]
    (qz_ref, qzp_ref, kc_ref, vc_ref, ksn_ref, vsn_ref, kwb_ref, vwb_ref, kwn_ref, vwn_ref, gate_ref,
     o_ref, sel_ref, acc_ref, ml_ref, ocw_ref) = refs[2 * npg:]
    c = pl.program_id(1)
    ng, nl = NSA_GROUPS, 128
    lg = nl // ng
    c2 = (NSA_DK ** -0.5) * LOG2E
    lane = lax.broadcasted_iota(jnp.int32, (1, nl), 1)
    tt = lane % ts
    t_row = past + tt
    lane_g = lane // lg

    def scores(k_of_g, q_ref):
        s = jnp.dot(k_of_g(0), q_ref[0, 0], preferred_element_type=F32)
        for g in range(1, ng):
            s = s + jnp.dot(k_of_g(g), q_ref[0, g], preferred_element_type=F32)
        return s * c2

    def pv(v_of_g, p):
        o = None
        for g in range(ng):
            pg = jnp.where(lane_g == g, p, 0.0).astype(BF16)
            d = lax.dot_general(v_of_g(g), pg, (((0,), (0,)), ((), ())), preferred_element_type=F32)
            o = d if o is None else o + d
        return o

    def pad_rows(x):
        return jnp.concatenate([x, jnp.zeros((16 - ts, x.shape[1]), x.dtype)], axis=0).astype(BF16)

    def update(sc, mask, v_of_g):
        sc = jnp.where(mask, sc, NEG)
        m, l = ml_ref[0:1, :], ml_ref[1:2, :]
        m_new = jnp.maximum(m, jnp.max(sc, axis=0, keepdims=True))
        alpha = jnp.exp2(m - m_new)
        pp = jnp.exp2(sc - m_new)
        ml_ref[0:1, :] = m_new
        ml_ref[1:2, :] = l * alpha + jnp.sum(pp, axis=0, keepdims=True)
        acc_ref[...] = acc_ref[...] * alpha + pv(v_of_g, pp)

    @pl.when(c == 0)
    def _():
        nb = kc_ref.shape[2]
        s = scores(lambda g: kc_ref[0, g], qzp_ref)
        n_io = lax.broadcasted_iota(jnp.int32, (nb, nl), 0)
        p = _masked_softmax_cols(s, n_io < ((t_row + 1) >> 6))
        ocw_ref[0] = pv(lambda g: vc_ref[0, g], p)
        li = lax.broadcasted_iota(jnp.int32, (nl, nl), 0)
        lj = lax.broadcasted_iota(jnp.int32, (nl, nl), 1)
        same = ((li // lg) == (lj // lg)) & ((li % ts) == (lj % ts))
        mm = same.astype(F32).astype(BF16)
        hi = p.astype(BF16)
        r1 = p - hi.astype(F32)
        mid = r1.astype(BF16)
        lo = (r1 - mid.astype(F32)).astype(BF16)
        score = (jnp.dot(hi, mm, preferred_element_type=F32) + jnp.dot(mid, mm, preferred_element_type=F32)
                 + jnp.dot(lo, mm, preferred_element_type=F32))
        sel_ref[...] = _select_blocks(score, t_row >> 6, n_keep)
        w = kwb_ref.shape[2]
        s_buf = scores(lambda g: kwb_ref[0, 0, :, g, :].astype(BF16), qz_ref)
        s_new = scores(lambda g: pad_rows(kwn_ref[:, g * DKP:(g + 1) * DKP]), qzp_ref)
        sw = jnp.concatenate([s_buf, s_new], axis=0)
        r_io = lax.broadcasted_iota(jnp.int32, (w + 16, nl), 0)
        spos = past - w + r_io
        dt = t_row - spos
        pw = _masked_softmax_cols(sw, (dt >= 0) & (dt <= NSA_WINDOW) & (spos >= 0) & (r_io < w + ts))
        ocw_ref[1] = (pv(lambda g: vwb_ref[0, 0, :, g, :].astype(BF16), pw[:w])
                      + pv(lambda g: pad_rows(vwn_ref[:, g * NSA_DV:(g + 1) * NSA_DV]), pw[w:]))
        acc_ref[...] = jnp.zeros_like(acc_ref)
        ml_ref[...] = jnp.zeros_like(ml_ref)
        ml_ref[0:1, :] = jnp.full((1, nl), NEG, F32)

    nbk = 2 * npg
    sc = scores(lambda g: jnp.concatenate([kp[k][0, 0, :, g, :] for k in range(npg)], axis=0).astype(BF16),
                qz_ref)
    selt = sel_ref[pl.ds(pl.multiple_of(c * nbk, nbk), nbk), :]
    mask = jnp.broadcast_to(selt[:, None, :], (nbk, NSA_BLOCK, nl)).reshape(nbk * NSA_BLOCK, nl) > 0.0
    update(sc, mask,
           lambda g: jnp.concatenate([vp[k][0, 0, :, g, :] for k in range(npg)], axis=0).astype(BF16))

    @pl.when(c == pl.num_programs(1) - 1)
    def _():
        sn = scores(lambda g: pad_rows(ksn_ref[:, g * DKP:(g + 1) * DKP]), qzp_ref)
        u = lax.broadcasted_iota(jnp.int32, (16, nl), 0)
        update(sn, (u <= tt) & (u < ts), lambda g: pad_rows(vsn_ref[:, g * NSA_DV:(g + 1) * NSA_DV]))
        m, l = ml_ref[0:1, :], ml_ref[1:2, :]
        o_slc = jnp.where(m > 0.5 * NEG, acc_ref[...] / l, 0.0)
        o_ref[0] = gate_ref[0, 0:1, :] * ocw_ref[0] + gate_ref[0, 1:2, :] * o_slc + gate_ref[0, 2:3, :] * ocw_ref[1]


def _nsa_sample(page_table, cache_k, cache_v, qz, qzp, kc, vc, ksn, vsn, kwb, vwb, kwn, vwn, gates, ts):
    db, n_pages = page_table.shape
    npg = _PAGES_PER_STEP
    steps = n_pages // npg
    past = n_pages * PAGE_SIZE
    nb = past // NSA_BLOCK
    w = kwb.shape[2]
    assert NSA_HPG * ts * NSA_GROUPS == 128 and ts <= 16

    def page_spec(dh, k):
        return pl.BlockSpec((1, 1, PAGE_SIZE, NSA_GROUPS, dh), lambda b, c, pt: (0, pt[b, c * npg + k], 0, 0, 0))

    per_seq = lambda *shape: pl.BlockSpec((1,) + shape, lambda b, c, pt: (b,) + (0,) * len(shape))
    rows = lambda width: pl.BlockSpec((ts, width), lambda b, c, pt: (b, 0))
    win = lambda dh: pl.BlockSpec((1, 1, w, NSA_GROUPS, dh), lambda b, c, pt: (0, b, 0, 0, 0))
    grid_spec = pltpu.PrefetchScalarGridSpec(
        num_scalar_prefetch=1,
        grid=(db, steps),
        in_specs=([page_spec(NSA_DK, k) for k in range(npg)] + [page_spec(NSA_DV, k) for k in range(npg)] + [
            per_seq(NSA_GROUPS, NSA_DK, 128), per_seq(NSA_GROUPS, DKP, 128),
            per_seq(NSA_GROUPS, nb, DKP), per_seq(NSA_GROUPS, nb, NSA_DV),
            rows(NSA_GROUPS * DKP), rows(NSA_GROUPS * NSA_DV),
            win(NSA_DK), win(NSA_DV),
            rows(NSA_GROUPS * DKP), rows(NSA_GROUPS * NSA_DV),
            per_seq(8, 128)]),
        out_specs=per_seq(NSA_DV, 128),
        scratch_shapes=[pltpu.VMEM((nb, 128), F32), pltpu.VMEM((NSA_DV, 128), F32),
                        pltpu.VMEM((8, 128), F32), pltpu.VMEM((2, NSA_DV, 128), F32)])
    return pl.pallas_call(
        functools.partial(_nsa_sample_kernel, past=past, ts=ts, n_keep=min(NSA_N_SEL - 1, nb)),
        grid_spec=grid_spec,
        out_shape=jax.ShapeDtypeStruct((db, NSA_DV, 128), F32),
        compiler_params=_cparams(("arbitrary", "arbitrary")),
        name="nsa_sample",
    )(page_table, *([cache_k] * npg), *([cache_v] * npg), qz, qzp, kc, vc, ksn, vsn, kwb, vwb, kwn, vwn, gates)


def _split3(x):
    hi = x.astype(BF16)
    r1 = x - hi.astype(F32)
    mid = r1.astype(BF16)
    lo = (r1 - mid.astype(F32)).astype(BF16)
    return jnp.concatenate([hi, mid, lo], axis=1)


def _gla_kernel(q_ref, k_ref, v_ref, a_ref, wa_ref, ba_ref, s0_ref, o_ref, sout_ref, st_ref,
                *, c, n_sub, n_valid):
    ci = pl.program_id(2)
    levels = int(math.log2(c))

    @pl.when(ci == 0)
    def _():
        st_ref[...] = s0_ref[0, 0].T

    t_io = lax.broadcasted_iota(jnp.int32, (c, c), 0)
    s_io = lax.broadcasted_iota(jnp.int32, (c, c), 1)
    row = lax.broadcasted_iota(jnp.int32, (c, 1), 0)
    sels = [s_io <= t_io]
    for lv in range(1, levels + 1):
        sels.append(s_io <= ((t_io >> lv) << lv) + (1 << (lv - 1)) - 1)
    sel_all = jnp.concatenate(sels, axis=0).astype(F32).astype(BF16)
    dk = GLA_DK
    for u in range(n_sub):
        rows = slice(u * c, (u + 1) * c)
        q = q_ref[rows, :] * (GLA_DK ** -0.5)
        k = k_ref[rows, :]
        v = v_ref[rows, :].astype(BF16)
        z = jnp.dot(a_ref[rows, :].astype(BF16), wa_ref[...], preferred_element_type=F32) + ba_ref[...]
        la = (jnp.minimum(z, 0.0) - jnp.log1p(jnp.exp(-jnp.abs(z)))) * (1.0 / GLA_TAU)
        if n_valid < c:
            la = jnp.where(row < n_valid, la, 0.0)
        big = jnp.dot(sel_all, _split3(la), preferred_element_type=F32)
        big = big[:, 0:dk] + big[:, dk:2 * dk] + big[:, 2 * dk:3 * dk]
        b = big[0:c]
        st = st_ref[...]
        o = lax.dot_general((q * jnp.exp(b)).astype(BF16), st.astype(BF16),
                            (((1,), (1,)), ((), ())), preferred_element_type=F32)
        a_mat = jnp.where(t_io == s_io, jnp.sum(q * k, axis=1, keepdims=True), 0.0)
        for lv in range(1, levels + 1):
            bref = big[lv * c:(lv + 1) * c]
            upper = ((row >> (lv - 1)) & 1) == 1
            ql = jnp.where(upper, q * jnp.exp(b - bref), 0.0).astype(BF16)
            kl = jnp.where(upper, 0.0, k * jnp.exp(bref - b)).astype(BF16)
            al = lax.dot_general(ql, kl, (((1,), (1,)), ((), ())), preferred_element_type=F32)
            a_mat = a_mat + jnp.where((t_io >> lv) == (s_io >> lv), al, 0.0)
        o = o + jnp.dot(a_mat.astype(BF16), v, preferred_element_type=F32)
        o_ref[rows, :] = o
        b_last = b[c - 1:c, :]
        kd = (k * jnp.exp(b_last - b)).astype(BF16)
        st_ref[...] = st * jnp.exp(b_last) + lax.dot_general(
            v, kd, (((0,), (0,)), ((), ())), preferred_element_type=F32)

    @pl.when(ci == pl.num_programs(2) - 1)
    def _():
        sout_ref[0, 0] = st_ref[...].T


def _gla(p, n_batch, t_len, s0, wa_pad, ba, c, n_sub, n_valid):
    step = c * n_sub
    ncs = t_len // step
    qb, kb = _SEG["q_g"] // GLA_DK, _SEG["k_g"] // GLA_DK
    vb, ab = _SEG["v_g"] // GLA_DV, _SEG["a_g"] // 256
    return pl.pallas_call(
        functools.partial(_gla_kernel, c=c, n_sub=n_sub, n_valid=n_valid),
        grid=(n_batch, GLA_HEADS, ncs),
        in_specs=[pl.BlockSpec((step, GLA_DK), lambda b, h, ci: (b * ncs + ci, qb + h)),
                  pl.BlockSpec((step, GLA_DK), lambda b, h, ci: (b * ncs + ci, kb + h)),
                  pl.BlockSpec((step, GLA_DV), lambda b, h, ci: (b * ncs + ci, vb + h)),
                  pl.BlockSpec((step, 256), lambda b, h, ci: (b * ncs + ci, ab)),
                  pl.BlockSpec((256, GLA_DK), lambda b, h, ci: (0, h)),
                  pl.BlockSpec((1, GLA_DK), lambda b, h, ci: (0, h)),
                  pl.BlockSpec((1, 1, GLA_DK, GLA_DV), lambda b, h, ci: (b, h, 0, 0))],
        out_specs=[pl.BlockSpec((step, GLA_DV), lambda b, h, ci: (b * ncs + ci, h)),
                   pl.BlockSpec((1, 1, GLA_DK, GLA_DV), lambda b, h, ci: (b, h, 0, 0))],
        out_shape=[jax.ShapeDtypeStruct((n_batch * t_len, GLA_WIDTH), F32),
                   jax.ShapeDtypeStruct((n_batch, GLA_HEADS, GLA_DK, GLA_DV), F32)],
        scratch_shapes=[pltpu.VMEM((GLA_DV, GLA_DK), F32)],
        compiler_params=_cparams(("arbitrary", "arbitrary", "arbitrary")),
        name="gla",
    )(p, p, p, p, wa_pad, ba, s0)


def _merge1_kernel(on_ref, zn_ref, og_ref, zg_ref, gg_ref, wa_ref, wb_ref, ma_ref, mb_ref, o_ref,
                   a_ref, b_ref):
    @pl.when(pl.program_id(1) == 0)
    def _():
        a_ref[...] = (on_ref[...] * _silu(zn_ref[...])).astype(BF16)
        og = og_ref[...]
        zg = zg_ref[...]
        for h in range(GLA_HEADS):
            sl = slice(h * GLA_DV, (h + 1) * GLA_DV)
            x = og[:, sl]
            y = x * lax.rsqrt(jnp.mean(x * x, axis=-1, keepdims=True) + NORM_EPS) * gg_ref[...]
            b_ref[:, sl] = (y * _silu(zg[:, sl])).astype(BF16)

    ua = jnp.dot(a_ref[...], wa_ref[...], preferred_element_type=F32)
    ub = jnp.dot(b_ref[...], wb_ref[...], preferred_element_type=F32)
    o_ref[...] = (jax.nn.sigmoid(ma_ref[...]) * ua + jax.nn.sigmoid(mb_ref[...]) * ub).astype(BF16)


def _merge1(o_nsa, o_gla, p, gla_g, wb_a, wb_b, tm):
    m = p.shape[0]
    tn = 512
    zn, zg, mg = _SEG["z_nsa"] // NSA_WIDTH, _SEG["z_g"] // GLA_WIDTH, _SEG["m_gate"] // tn
    row = lambda i, j: (i, 0)
    return pl.pallas_call(
        _merge1_kernel,
        grid=(m // tm, D_MODEL // tn),
        in_specs=[pl.BlockSpec((tm, NSA_WIDTH), row),
                  pl.BlockSpec((tm, NSA_WIDTH), lambda i, j: (i, zn)),
                  pl.BlockSpec((tm, GLA_WIDTH), row),
                  pl.BlockSpec((tm, GLA_WIDTH), lambda i, j: (i, zg)),
                  pl.BlockSpec((1, GLA_DV), lambda i, j: (0, 0)),
                  pl.BlockSpec((NSA_WIDTH, tn), lambda i, j: (0, j)),
                  pl.BlockSpec((GLA_WIDTH, tn), lambda i, j: (0, j)),
                  pl.BlockSpec((tm, tn), lambda i, j: (i, mg + j)),
                  pl.BlockSpec((tm, tn), lambda i, j: (i, mg + D_MODEL // tn + j))],
        out_specs=pl.BlockSpec((tm, tn), lambda i, j: (i, j)),
        out_shape=jax.ShapeDtypeStruct((m, D_MODEL), BF16),
        scratch_shapes=[pltpu.VMEM((tm, NSA_WIDTH), BF16), pltpu.VMEM((tm, GLA_WIDTH), BF16)],
        compiler_params=_cparams(("arbitrary", "arbitrary")),
        name="merge1",
    )(o_nsa, p, o_gla, p, gla_g, wb_a, wb_b, p, p)


def _merge2_kernel(x_ref, gate_ref, mg_ref, w_ref, o_ref):
    o_ref[...] = x_ref[...] + gate_ref[...] * jnp.dot(mg_ref[...], w_ref[...], preferred_element_type=F32)


def _merge2(x, gate, merged, w_out, tm):
    m, d = x.shape
    tn = 512
    per_row = gate.shape[0] != 1
    gate_spec = (pl.BlockSpec((tm, tn), lambda i, j: (i, j)) if per_row
                 else pl.BlockSpec((1, tn), lambda i, j: (0, j)))
    return pl.pallas_call(
        _merge2_kernel,
        grid=(m // tm, d // tn),
        in_specs=[pl.BlockSpec((tm, tn), lambda i, j: (i, j)), gate_spec,
                  pl.BlockSpec((tm, d), lambda i, j: (i, 0)),
                  pl.BlockSpec((d, tn), lambda i, j: (0, j))],
        out_specs=pl.BlockSpec((tm, tn), lambda i, j: (i, j)),
        out_shape=jax.ShapeDtypeStruct((m, d), F32),
        compiler_params=_cparams(("arbitrary", "arbitrary")),
        name="merge2",
    )(x, gate, merged, w_out)


def _pad_last(a, width):
    return jnp.pad(a, [(0, 0)] * (a.ndim - 1) + [(0, width - a.shape[-1])])


def _pad_heads(w, n, src, dst):
    return _pad_last(w.reshape(w.shape[0], n, src), dst).reshape(w.shape[0], n * dst)


def _pad_w_in(w_in):
    sizes = (NSA_HEADS * NSA_DK, NSA_GROUPS * NSA_DK, NSA_GROUPS * NSA_DV, NSA_GROUPS * NSA_DK,
             NSA_GROUPS * NSA_DV, NSA_GROUPS * NSA_DK, NSA_GROUPS * NSA_DV, 3 * NSA_HEADS, NSA_WIDTH,
             GLA_HEADS * GLA_DK, GLA_HEADS * GLA_DK, GLA_WIDTH, GLA_WIDTH, GLA_RANK, 2 * D_MODEL)
    cuts = [0]
    for s in sizes:
        cuts.append(cuts[-1] + s)
    names = ("q", "k_cmp", "v_cmp", "k_slc", "v_slc", "k_win", "v_win", "g_nsa", "z_nsa",
             "q_g", "k_g", "v_g", "z_g", "a_g", "m_gate")
    parts = {nm: w_in[:, cuts[n]:cuts[n + 1]] for n, nm in enumerate(names)}
    parts["q"] = _pad_heads(parts["q"], NSA_HEADS, NSA_DK, DKP)
    for nm in ("k_cmp", "k_slc", "k_win"):
        parts[nm] = _pad_heads(parts[nm], NSA_GROUPS, NSA_DK, DKP)
    parts["g_nsa"] = _pad_last(_pad_heads(parts["g_nsa"], NSA_GROUPS, 3 * NSA_HPG, 16), 256)
    parts["a_g"] = _pad_last(parts["a_g"], 256)
    return jnp.concatenate([parts[nm] for nm in _SEG], axis=1).astype(BF16)


def _unpad_heads(a, n):
    return a.reshape(a.shape[0], n, DKP)[:, :, :NSA_DK]


def kernel(x_prompt, x_sample, cache_k_cmp, cache_v_cmp, cache_k_slc, cache_v_slc, state_k_win, state_v_win, state_gla, page_table, c_prompt, c_sample, ln_g, w_ada, b_ada, w_in, q_norm_g, k_cmp_norm_g, k_slc_norm_g, k_win_norm_g, pe_k, w1_k, w2_k, pe_v, w1_v, w2_v, w_a2, b_a, gla_norm_g, w_branch, w_out):
    assert ln_g.shape[0] == 1, "single layer"
    bp, t, d = x_prompt.shape
    db, ts, _ = x_sample.shape
    assert bp == 1 and d == D_MODEL
    G = NSA_GROUPS

    w_pad = _pad_w_in(w_in[0])
    g_q = _pad_last(q_norm_g, DKP)
    g_kc = _pad_last(k_cmp_norm_g, DKP)
    g_ks = _pad_last(k_slc_norm_g, DKP)
    g_kw = _pad_last(k_win_norm_g, DKP)
    pe_k_p = _pad_last(pe_k[0], DKP)[:, None, :]
    w1_k_p = jnp.pad(w1_k[0], ((0, 0), (0, DKP - NSA_DK), (0, 0))).astype(BF16)
    w2_k_p = _pad_last(w2_k[0], DKP).astype(BF16)
    pe_v_p = pe_v[0][:, None, :]
    w1_v_p = w1_v[0].astype(BF16)
    w2_v_p = w2_v[0].astype(BF16)
    wa_pad = jnp.pad(w_a2[0], ((0, 256 - GLA_RANK), (0, 0))).astype(BF16)
    wb_a = w_branch[0, :NSA_WIDTH].astype(BF16)
    wb_b = w_branch[0, NSA_WIDTH:].astype(BF16)
    w_o = w_out[0].astype(BF16)
    ones_v = jnp.ones((1, NSA_DV), F32)

    n_c = bp + db
    c_all = jnp.pad(jnp.concatenate([c_prompt, c_sample], axis=0), ((0, (-n_c) % 8), (0, 0)))
    mod = _ada(c_all, w_ada[0], b_ada)
    shift, scale, gate = mod[:, :d], mod[:, d:2 * d], mod[:, 2 * d:]

    tm = min(1024, t)
    xp = x_prompt.reshape(t, d)
    pp = _proj(xp, scale[0:1], shift[0:1], ln_g, w_pad, tm)
    tq = min(256, t)
    tk = min(512, t)
    qt = _prep_q(pp, g_q, min(512, t))
    ks_f, ks_b = _prep_k(pp, "k_slc", g_ks, min(512, t))
    kw_f, kw_b = _prep_k(pp, "k_win", g_kw, min(512, t))
    vst = _prep_vt(pp, "v_slc", tk)
    vwt = _prep_vt(pp, "v_win", tq)
    gt = _prep_gate(pp, min(512, t))
    kc = _compress_prompt(pp, "k_cmp", DKP, pe_k_p, w1_k_p, w2_k_p, g_kc, True, False)
    vct = _compress_prompt(pp, "v_cmp", NSA_DV, pe_v_p, w1_v_p, w2_v_p, ones_v, False, True)
    o_nsa_p = _nsa_prompt(qt, kc, vct, ks_b, vst, kw_b, vwt, gt, tq, tk)
    s0_p = jnp.zeros((bp, GLA_HEADS, GLA_DK, GLA_DV), F32)
    o_gla_p, gla_p = _gla(pp, bp, t, s0_p, wa_pad, b_a, GLA_CHUNK, 4, GLA_CHUNK)
    merged_p = _merge1(o_nsa_p, o_gla_p, pp, gla_norm_g, wb_a, wb_b, min(512, t))
    y_p = _merge2(xp, gate[0:1], merged_p, w_o, min(512, t)).reshape(bp, t, d)

    wp = min(NSA_WINDOW, t)
    k_cmp_p = _unpad_heads(pp[:, _SEG["k_cmp"]:_SEG["k_cmp"] + G * DKP], G)
    v_cmp_p = pp[:, _SEG["v_cmp"]:_SEG["v_cmp"] + G * NSA_DV].reshape(t, G, NSA_DV)
    k_slc_p = _unpad_heads(ks_f, G)
    v_slc_p = pp[:, _SEG["v_slc"]:_SEG["v_slc"] + G * NSA_DV].reshape(t, G, NSA_DV)
    k_win_p = _unpad_heads(kw_f, G)[t - wp:]
    v_win_p = pp[t - wp:, _SEG["v_win"]:_SEG["v_win"] + G * NSA_DV].reshape(wp, G, NSA_DV)

    ms = db * ts
    xs = x_sample.reshape(ms, d)
    rep = lambda a: jnp.repeat(a[bp:bp + db], ts, axis=0)
    ps = _proj(xs, rep(scale), rep(shift), ln_g, w_pad, ms)
    seg = lambda name, w: ps[:, _SEG[name]:_SEG[name] + w]
    qt_s = _prep_q(ps, g_q, ms)
    kss_f, _ = _prep_k(ps, "k_slc", g_ks, ms)
    kws_f, _ = _prep_k(ps, "k_win", g_kw, ms)
    gt_s = _prep_gate(ps, ms)
    v_slc_new = seg("v_slc", G * NSA_DV)
    v_win_new = seg("v_win", G * NSA_DV)
    k_cmp_s = _unpad_heads(seg("k_cmp", G * DKP), G).reshape(db, ts, G, NSA_DK)
    v_cmp_s = seg("v_cmp", G * NSA_DV).reshape(db, ts, G, NSA_DV)
    k_slc_s = _unpad_heads(kss_f, G).reshape(db, ts, G, NSA_DK)
    v_slc_s = v_slc_new.reshape(db, ts, G, NSA_DV)
    k_win_s_new = _unpad_heads(kws_f, G).reshape(db, ts, G, NSA_DK)
    v_win_s_new = v_win_new.reshape(db, ts, G, NSA_DV)
    q4 = qt_s.reshape(G, NSA_HPG, DKP, db, ts).transpose(3, 0, 2, 1, 4).reshape(db, G, DKP, NSA_HPG * ts)
    own = (jnp.arange(G)[:, None] == jnp.arange(G)[None, :])[None, :, None, :, None]
    qzp = jnp.where(own, q4[:, :, :, None, :], jnp.zeros((), BF16)).reshape(db, G, DKP, G * NSA_HPG * ts)
    qz = qzp[:, :, :NSA_DK, :]
    gates = gt_s.reshape(G, 16, db, ts)[:, :3 * NSA_HPG].reshape(G, NSA_HPG, 3, db, ts)
    gates = gates.transpose(3, 2, 0, 1, 4).reshape(db, 3, G * NSA_HPG * ts)
    gates = jnp.pad(gates, ((0, 0), (0, 5), (0, 0)))
    kc_s = _compress_paged(cache_k_cmp, page_table, pe_k_p[:, 0, :], w1_k_p, w2_k_p, g_kc, True)
    vc_s = _compress_paged(cache_v_cmp, page_table, pe_v[0], w1_v_p, w2_v_p, ones_v, False)
    o_t = _nsa_sample(page_table, cache_k_slc, cache_v_slc, qz, qzp, kc_s, vc_s, kss_f, v_slc_new,
                      state_k_win, state_v_win, kws_f, v_win_new, gates, ts)
    o_nsa_s = o_t.reshape(db, NSA_DV, G, NSA_HPG, ts).transpose(0, 4, 2, 3, 1)
    kw_s = jnp.concatenate([state_k_win[0][:, ts:], k_win_s_new], axis=1)
    vw_s = jnp.concatenate([state_v_win[0][:, ts:], v_win_s_new], axis=1)
    c_s = 16
    ps_pad = jnp.pad(ps.reshape(db, ts, D_PAD), ((0, 0), (0, c_s - ts), (0, 0))).reshape(db * c_s, D_PAD)
    o_gla_s, gla_s = _gla(ps_pad, db, c_s, state_gla[0], wa_pad, b_a, c_s, 1, ts)
    o_gla_s = o_gla_s.reshape(db, c_s, GLA_WIDTH)[:, :ts].reshape(ms, GLA_WIDTH)
    merged_s = _merge1(o_nsa_s.reshape(ms, NSA_WIDTH), o_gla_s, ps, gla_norm_g, wb_a, wb_b, ms)
    y_s = _merge2(xs, rep(gate), merged_s, w_o, ms).reshape(db, ts, d)

    L1 = lambda a: a[None, None]
    return (y_p, y_s,
            L1(k_cmp_p), L1(v_cmp_p), L1(k_slc_p), L1(v_slc_p), L1(k_win_p), L1(v_win_p), gla_p[None],
            k_cmp_s[None], v_cmp_s[None], k_slc_s[None], v_slc_s[None], kw_s[None], vw_s[None], gla_s[None])
```

```python
import functools
import math

import jax
import jax.numpy as jnp
from jax import lax
from jax.experimental import pallas as pl
from jax.experimental.pallas import tpu as pltpu

F32 = jnp.float32
BF16 = jnp.bfloat16

NSA_HEADS = 16
NSA_GROUPS = 4
NSA_HPG = NSA_HEADS // NSA_GROUPS
NSA_DK = 192
NSA_DV = 128
NSA_BLOCK = 64
NSA_N_SEL = 16
NSA_WINDOW = 512
NSA_CMP_HIDDEN = 256
PAGE_SIZE = 128
GLA_HEADS = 4
GLA_DK = 256
GLA_DV = 512
GLA_RANK = 16
GLA_TAU = 16.0
GLA_CHUNK = 64
NORM_EPS = 1e-6
D_MODEL = 2048
NSA_WIDTH = NSA_HEADS * NSA_DV
GLA_WIDTH = GLA_HEADS * GLA_DV

DKP = 256
LOG2E = 1.4426950408889634
NEG = -1e30
VMEM_LIMIT = 56 * 1024 * 1024

_SEG = {}
_off = 0
for _name, _w in (("z_nsa", NSA_WIDTH), ("z_g", GLA_WIDTH), ("v_g", GLA_WIDTH), ("m_gate", 2 * D_MODEL),
                  ("q", NSA_HEADS * DKP), ("k_cmp", NSA_GROUPS * DKP), ("v_cmp", NSA_GROUPS * NSA_DV),
                  ("k_slc", NSA_GROUPS * DKP), ("v_slc", NSA_GROUPS * NSA_DV),
                  ("k_win", NSA_GROUPS * DKP), ("v_win", NSA_GROUPS * NSA_DV),
                  ("q_g", GLA_HEADS * GLA_DK), ("k_g", GLA_HEADS * GLA_DK),
                  ("g_nsa", 256), ("a_g", 256)):
    _SEG[_name] = _off
    _off += _w
D_PAD = _off


def _cparams(sem):
    return pltpu.CompilerParams(dimension_semantics=sem, vmem_limit_bytes=VMEM_LIMIT)


def _silu(x):
    return x * jax.nn.sigmoid(x)


def _ada_kernel(c_ref, w_ref, b_ref, o_ref):
    a = _silu(c_ref[...]).astype(BF16)
    o_ref[...] = jnp.dot(a, w_ref[...].astype(BF16), preferred_element_type=F32) + b_ref[...]


def _ada(c_all, w_ada, b_ada):
    m, d = c_all.shape
    n = w_ada.shape[1]
    tn = 768
    return pl.pallas_call(
        _ada_kernel,
        grid=(n // tn,),
        in_specs=[pl.BlockSpec((m, d), lambda j: (0, 0)),
                  pl.BlockSpec((d, tn), lambda j: (0, j)),
                  pl.BlockSpec((1, tn), lambda j: (0, j))],
        out_specs=pl.BlockSpec((m, tn), lambda j: (0, j)),
        out_shape=jax.ShapeDtypeStruct((m, n), F32),
        compiler_params=_cparams(("arbitrary",)),
        name="ada",
    )(c_all, w_ada, b_ada)


def _proj_kernel(x_ref, sc_ref, sh_ref, g_ref, w_ref, o_ref, h_ref):
    @pl.when(pl.program_id(1) == 0)
    def _():
        x = x_ref[...]
        y = x * lax.rsqrt(jnp.mean(x * x, axis=-1, keepdims=True) + NORM_EPS) * g_ref[...]
        h_ref[...] = (y * (1.0 + sc_ref[...]) + sh_ref[...]).astype(BF16)

    o_ref[...] = jnp.dot(h_ref[...], w_ref[...], preferred_element_type=F32)


def _proj(x, scale, shift, ln_g, w_pad, tm):
    m, d = x.shape
    n = w_pad.shape[1]
    tn = 1536
    per_row = scale.shape[0] != 1
    mod_spec = (pl.BlockSpec((tm, d), lambda i, j: (i, 0)) if per_row
                else pl.BlockSpec((1, d), lambda i, j: (0, 0)))
    return pl.pallas_call(
        _proj_kernel,
        grid=(m // tm, n // tn),
        in_specs=[pl.BlockSpec((tm, d), lambda i, j: (i, 0)), mod_spec, mod_spec,
                  pl.BlockSpec((1, d), lambda i, j: (0, 0)),
                  pl.BlockSpec((d, tn), lambda i, j: (0, j))],
        out_specs=pl.BlockSpec((tm, tn), lambda i, j: (i, j)),
        out_shape=jax.ShapeDtypeStruct((m, n), F32),
        scratch_shapes=[pltpu.VMEM((tm, d), BF16)],
        compiler_params=_cparams(("arbitrary", "arbitrary")),
        name="proj",
    )(x, scale, shift, ln_g, w_pad)


def _head_norm(x, g):
    ms = jnp.sum(x * x, axis=-1, keepdims=True) * (1.0 / NSA_DK)
    return x * lax.rsqrt(ms + NORM_EPS) * g


def _prep_q_kernel(x_ref, g_ref, o_ref):
    o_ref[0] = _head_norm(x_ref[...], g_ref[...]).T.astype(BF16)


def _prep_q(p, g_pad, tm):
    t = p.shape[0]
    base = _SEG["q"] // DKP
    return pl.pallas_call(
        _prep_q_kernel,
        grid=(t // tm, NSA_HEADS),
        in_specs=[pl.BlockSpec((tm, DKP), lambda i, h: (i, base + h)),
                  pl.BlockSpec((1, DKP), lambda i, h: (0, 0))],
        out_specs=pl.BlockSpec((1, DKP, tm), lambda i, h: (h, 0, i)),
        out_shape=jax.ShapeDtypeStruct((NSA_HEADS, DKP, t), BF16),
        compiler_params=_cparams(("arbitrary", "arbitrary")),
        name="prep_q",
    )(p, g_pad)


def _prep_k_kernel(x_ref, g_ref, o_ref, ob_ref):
    y = _head_norm(x_ref[...], g_ref[...])
    o_ref[...] = y
    ob_ref[...] = y.astype(BF16)


def _prep_k(p, seg, g_pad, tm):
    t = p.shape[0]
    base = _SEG[seg] // DKP
    n = NSA_GROUPS * DKP
    return pl.pallas_call(
        _prep_k_kernel,
        grid=(t // tm, NSA_GROUPS),
        in_specs=[pl.BlockSpec((tm, DKP), lambda i, g: (i, base + g)),
                  pl.BlockSpec((1, DKP), lambda i, g: (0, 0))],
        out_specs=[pl.BlockSpec((tm, DKP), lambda i, g: (i, g)),
                   pl.BlockSpec((tm, DKP), lambda i, g: (i, g))],
        out_shape=[jax.ShapeDtypeStruct((t, n), F32), jax.ShapeDtypeStruct((t, n), BF16)],
        compiler_params=_cparams(("arbitrary", "arbitrary")),
        name="prep_k_" + seg,
    )(p, g_pad)


def _prep_vt_kernel(x_ref, o_ref):
    o_ref[0, 0] = x_ref[...].T.astype(BF16)


def _prep_vt(p, seg, tile):
    t = p.shape[0]
    base = _SEG[seg] // NSA_DV
    return pl.pallas_call(
        _prep_vt_kernel,
        grid=(t // tile, NSA_GROUPS),
        in_specs=[pl.BlockSpec((tile, NSA_DV), lambda i, g: (i, base + g))],
        out_specs=pl.BlockSpec((1, 1, NSA_DV, tile), lambda i, g: (g, i, 0, 0)),
        out_shape=jax.ShapeDtypeStruct((NSA_GROUPS, t // tile, NSA_DV, tile), BF16),
        compiler_params=_cparams(("arbitrary", "arbitrary")),
        name="prep_vt_" + seg,
    )(p)


def _prep_gate_kernel(x_ref, o_ref):
    o_ref[...] = jax.nn.sigmoid(x_ref[...]).T[:4 * NSA_HEADS, :]


def _prep_gate(p, tm):
    t = p.shape[0]
    base = _SEG["g_nsa"] // 256
    return pl.pallas_call(
        _prep_gate_kernel,
        grid=(t // tm,),
        in_specs=[pl.BlockSpec((tm, 256), lambda i: (i, base))],
        out_specs=pl.BlockSpec((4 * NSA_HEADS, tm), lambda i: (0, i)),
        out_shape=jax.ShapeDtypeStruct((4 * NSA_HEADS, t), F32),
        compiler_params=_cparams(("arbitrary",)),
        name="prep_gate",
    )(p)


_CMP_ROWS = 8


def _cmp_kernel(x_ref, pe_ref, w1_ref, w2_ref, g_ref, o_ref, acc_ref, *, norm, transpose_out):
    s = pl.program_id(1)

    @pl.when(s == 0)
    def _():
        acc_ref[...] = jnp.zeros_like(acc_ref)

    acc = acc_ref[...]
    for r in range(_CMP_ROWS):
        xb = (x_ref[:, r, :] + pe_ref[r]).astype(BF16)
        acc = acc + jnp.dot(xb, w1_ref[r], preferred_element_type=F32)
    acc_ref[...] = acc

    @pl.when(s == pl.num_programs(1) - 1)
    def _():
        hid = _silu(acc_ref[...]).astype(BF16)
        y = jnp.dot(hid, w2_ref[...], preferred_element_type=F32)
        if norm:
            y = _head_norm(y, g_ref[...])
        if transpose_out:
            y = y.T
        o_ref[0] = y.astype(o_ref.dtype)


def _compress_prompt(p, seg, dp, pe_pad, w1_pad, w2_pad, g_pad, norm, transpose_out):
    t = p.shape[0]
    nb = t // NSA_BLOCK
    pv = p.reshape(nb, NSA_BLOCK, D_PAD)
    base = _SEG[seg] // dp
    out_shape = (NSA_GROUPS, dp, nb) if transpose_out else (NSA_GROUPS, nb, dp)
    return pl.pallas_call(
        functools.partial(_cmp_kernel, norm=norm, transpose_out=transpose_out),
        grid=(NSA_GROUPS, NSA_BLOCK // _CMP_ROWS),
        in_specs=[pl.BlockSpec((nb, _CMP_ROWS, dp), lambda g, s: (0, s, base + g)),
                  pl.BlockSpec((_CMP_ROWS, 1, dp), lambda g, s: (s, 0, 0)),
                  pl.BlockSpec((_CMP_ROWS, dp, NSA_CMP_HIDDEN), lambda g, s: (s, 0, 0)),
                  pl.BlockSpec((NSA_CMP_HIDDEN, dp), lambda g, s: (0, 0)),
                  pl.BlockSpec((1, dp), lambda g, s: (0, 0))],
        out_specs=pl.BlockSpec((1,) + out_shape[1:], lambda g, s: (g, 0, 0)),
        out_shape=jax.ShapeDtypeStruct(out_shape, BF16),
        scratch_shapes=[pltpu.VMEM((nb, NSA_CMP_HIDDEN), F32)],
        compiler_params=_cparams(("arbitrary", "arbitrary")),
        name="cmp_" + seg,
    )(pv, pe_pad, w1_pad, w2_pad, g_pad)


_PAGES_PER_STEP = 8
_STAGE_PITCH = 72


def _cmp_paged_kernel(pt_ref, *refs, dh, keys_on_lanes, norm):
    del pt_ref
    npg = _PAGES_PER_STEP
    pages = refs[:npg]
    pe_ref, w1_ref, w2_ref, g_ref, o_ref, stage_ref, xs_ref = refs[npg:]
    c = pl.program_id(2)
    nbc = 2 * npg
    n_slab = stage_ref.shape[1]

    @pl.when((pl.program_id(0) == 0) & (pl.program_id(1) == 0) & (c == 0))
    def _():
        stage_ref[...] = jnp.zeros_like(stage_ref)

    for k in range(npg):
        for g in range(NSA_GROUPS):
            if keys_on_lanes:
                x = pages[k][0, 0, g].T
            else:
                x = pages[k][0, pl.ds(g, PAGE_SIZE, stride=NSA_GROUPS), :]
            for blk in range(2):
                r0 = (2 * k + blk) * _STAGE_PITCH
                xb = x[blk * NSA_BLOCK:(blk + 1) * NSA_BLOCK]
                stage_ref[g, 0, r0:r0 + NSA_BLOCK, :] = xb[:, 0:128]
                if n_slab == 2:
                    stage_ref[g, 1, r0:r0 + NSA_BLOCK, 0:dh - 128] = xb[:, 128:dh]
    dp = pe_ref.shape[1]
    row0 = pl.multiple_of(c * nbc, nbc)
    for s in range(NSA_BLOCK):
        for g in range(NSA_GROUPS):
            parts = [stage_ref[g, j, pl.ds(s, nbc, stride=_STAGE_PITCH), :] for j in range(n_slab)]
            x = parts[0] if n_slab == 1 else jnp.concatenate(parts, axis=1)
            xs_ref[g, pl.ds(row0, nbc), s * dp:(s + 1) * dp] = (x + pe_ref[s:s + 1, :]).astype(BF16)

    @pl.when(c == pl.num_programs(2) - 1)
    def _():
        for g in range(NSA_GROUPS):
            acc = jnp.dot(xs_ref[g], w1_ref[...], preferred_element_type=F32)
            hid = _silu(acc).astype(BF16)
            y = jnp.dot(hid, w2_ref[...], preferred_element_type=F32)
            if norm:
                y = _head_norm(y, g_ref[...])
            o_ref[0, g] = y.astype(BF16)


def _compress_paged(cache, page_table, pe_pad, w1_pad, w2_pad, g_pad, norm):
    keys_on_lanes = cache.ndim == 5
    dh = cache.shape[3] if keys_on_lanes else cache.shape[2]
    dp = pe_pad.shape[-1]
    db, n_pages = page_table.shape
    npg = _PAGES_PER_STEP
    steps = n_pages // npg
    spp = min(8, steps)
    m_blocks = spp * 2 * npg
    n_ph = steps // spp
    n_slab = dp // 128

    def page_spec(k):
        page = lambda b, ph, c, pt: pt[b, (ph * spp + c) * npg + k]
        if keys_on_lanes:
            return pl.BlockSpec((1, 1, NSA_GROUPS, dh, PAGE_SIZE), lambda b, ph, c, pt: (0, page(b, ph, c, pt), 0, 0, 0))
        return pl.BlockSpec((1, PAGE_SIZE * NSA_GROUPS, dh), lambda b, ph, c, pt: (page(b, ph, c, pt), 0, 0))

    const = lambda *shape: pl.BlockSpec(shape, lambda b, ph, c, pt: (0,) * len(shape))
    grid_spec = pltpu.PrefetchScalarGridSpec(
        num_scalar_prefetch=1,
        grid=(db, n_ph, spp),
        in_specs=[page_spec(k) for k in range(npg)] + [
            const(NSA_BLOCK, dp),
            pl.BlockSpec((NSA_BLOCK * dp, NSA_CMP_HIDDEN), lambda b, ph, c, pt: (0, 0),
                         pipeline_mode=pl.Buffered(1)),
            const(NSA_CMP_HIDDEN, dp), const(1, dp)],
        out_specs=pl.BlockSpec((1, NSA_GROUPS, m_blocks, dp), lambda b, ph, c, pt: (b, 0, ph, 0)),
        scratch_shapes=[pltpu.VMEM((NSA_GROUPS, n_slab, 2 * npg * _STAGE_PITCH, 128), F32),
                        pltpu.VMEM((NSA_GROUPS, m_blocks, NSA_BLOCK * dp), BF16)])
    return pl.pallas_call(
        functools.partial(_cmp_paged_kernel, dh=dh, keys_on_lanes=keys_on_lanes, norm=norm),
        grid_spec=grid_spec,
        out_shape=jax.ShapeDtypeStruct((db, NSA_GROUPS, n_pages * 2, dp), BF16),
        compiler_params=_cparams(("arbitrary", "arbitrary", "arbitrary")),
        name="cmp_paged_%d" % dh,
    )(page_table, *([cache] * npg), pe_pad, w1_pad.reshape(NSA_BLOCK * dp, NSA_CMP_HIDDEN), w2_pad, g_pad)


def _masked_softmax_cols(s, valid):
    s = jnp.where(valid, s, NEG)
    m = jnp.max(s, axis=0, keepdims=True)
    e = jnp.where(valid, jnp.exp2(s - m), 0.0)
    return e / jnp.maximum(jnp.sum(e, axis=0, keepdims=True), 1e-30)


def _select_blocks(score, cb, n_keep):
    nb, w = score.shape
    n_q = lax.broadcasted_iota(jnp.int32, (nb, w), 0)
    n_f = n_q.astype(F32)
    allowed = n_q <= cb
    forced = allowed & ((n_q == 0) | (n_q == cb) | (n_q == cb - 1))
    val0 = jnp.where(forced, jnp.inf, jnp.where(allowed, score, -jnp.inf))

    def pick(_, val):
        mx = jnp.max(val, axis=0, keepdims=True)
        idx = jnp.min(jnp.where(val == mx, n_f, float(nb)), axis=0, keepdims=True)
        return jnp.where(n_f == idx, -jnp.inf, val)

    val = lax.fori_loop(0, n_keep, pick, val0)
    return jnp.where(allowed & (val == -jnp.inf), 1.0, 0.0)


def _nsa_prompt_kernel(qt_ref, kc_ref, vct_ref, ks_ref, vst_ref, kw_ref, vwt_ref, gt_ref,
                       o_ref, sel_ref, acc_ref, *, tq, tk, n_keep):
    i = pl.program_id(1)
    r = NSA_HPG * tq
    t0 = i * tq
    c2 = (NSA_DK ** -0.5) * LOG2E
    qt = jnp.concatenate([qt_ref[h] for h in range(NSA_HPG)], axis=1)
    t_row = t0 + (lax.broadcasted_iota(jnp.int32, (1, r), 1) & (tq - 1))

    nb = kc_ref.shape[1]
    s = jnp.dot(kc_ref[0], qt, preferred_element_type=F32) * c2
    n_io = lax.broadcasted_iota(jnp.int32, (nb, r), 0)
    p = _masked_softmax_cols(s, n_io < ((t_row + 1) >> 6))
    o_cmp = jnp.dot(vct_ref[0], p.astype(BF16), preferred_element_type=F32)
    score = p[:, 0:tq]
    for h in range(1, NSA_HPG):
        score = score + p[:, h * tq:(h + 1) * tq]

    cb = (t0 + lax.broadcasted_iota(jnp.int32, (1, tq), 1)) >> 6
    sel = _select_blocks(score, cb, n_keep)
    sel_ref[...] = jnp.concatenate([sel] * NSA_HPG, axis=1)

    nbk = tk // NSA_BLOCK
    acc_ref[...] = jnp.zeros_like(acc_ref)

    def slc_tile(j, carry, causal):
        m, l = carry
        k = ks_ref[pl.ds(pl.multiple_of(j * tk, tk), tk), :]
        bias = (sel_ref[pl.ds(pl.multiple_of(j * nbk, nbk), nbk), :] - 1.0) * (-NEG)
        bias = jnp.broadcast_to(bias[:, None, :], (nbk, NSA_BLOCK, r)).reshape(tk, r)
        sc = jnp.dot(k, qt, preferred_element_type=F32) * c2 + bias
        if causal:
            pos = j * tk + lax.broadcasted_iota(jnp.int32, (tk, r), 0)
            sc = jnp.where(pos <= t_row, sc, NEG)
        m_new = jnp.maximum(m, jnp.max(sc, axis=0, keepdims=True))
        alpha = jnp.exp2(m - m_new)
        pp = jnp.exp2(sc - m_new)
        l_new = l * alpha + jnp.sum(pp, axis=0, keepdims=True)
        pv = jnp.dot(vst_ref[0, j], pp.astype(BF16), preferred_element_type=F32)
        acc_ref[...] = acc_ref[...] * alpha + pv
        return m_new, l_new

    n_full = t0 // tk
    init = (jnp.full((1, r), NEG, F32), jnp.zeros((1, r), F32))
    carry = lax.fori_loop(0, n_full, functools.partial(slc_tile, causal=False), init)
    m, l = slc_tile(n_full, carry, True)
    o_slc = jnp.where(m > 0.5 * NEG, acc_ref[...] / l, 0.0)

    n_w = NSA_WINDOW // tq + 1
    k_parts, v_parts = [], []
    for d in range(n_w):
        start = jnp.maximum(t0 - NSA_WINDOW + d * tq, 0)
        k_parts.append(kw_ref[pl.ds(pl.multiple_of(start, tq), tq), :])
        v_parts.append(vwt_ref[0, jnp.maximum(i - (n_w - 1) + d, 0)])
    kcat = jnp.concatenate(k_parts, axis=0)
    sw = jnp.dot(kcat, qt, preferred_element_type=F32) * c2
    spos = t0 - NSA_WINDOW + lax.broadcasted_iota(jnp.int32, (n_w * tq, r), 0)
    dt = t_row - spos
    pw = _masked_softmax_cols(sw, (dt >= 0) & (dt <= NSA_WINDOW) & (spos >= 0))
    o_win = jnp.dot(jnp.concatenate(v_parts, axis=1), pw.astype(BF16), preferred_element_type=F32)

    for h in range(NSA_HPG):
        sl = slice(h * tq, (h + 1) * tq)
        o = (gt_ref[3 * h:3 * h + 1, :] * o_cmp[:, sl] + gt_ref[3 * h + 1:3 * h + 2, :] * o_slc[:, sl]
             + gt_ref[3 * h + 2:3 * h + 3, :] * o_win[:, sl])
        o_ref[:, h * NSA_DV:(h + 1) * NSA_DV] = o.T


def _nsa_prompt(qt, kc, vct, ks, vst, kw, vwt, gt, tq, tk):
    t = ks.shape[0]
    nb = t // NSA_BLOCK
    r = NSA_HPG * tq
    resident = dict(pipeline_mode=pl.Buffered(1))
    return pl.pallas_call(
        functools.partial(_nsa_prompt_kernel, tq=tq, tk=tk, n_keep=min(NSA_N_SEL, nb)),
        grid=(NSA_GROUPS, t // tq),
        in_specs=[pl.BlockSpec((NSA_HPG, DKP, tq), lambda g, i: (g, 0, i)),
                  pl.BlockSpec((1, nb, DKP), lambda g, i: (g, 0, 0)),
                  pl.BlockSpec((1, NSA_DV, nb), lambda g, i: (g, 0, 0)),
                  pl.BlockSpec((t, DKP), lambda g, i: (0, g), **resident),
                  pl.BlockSpec((1, t // tk, NSA_DV, tk), lambda g, i: (g, 0, 0, 0), **resident),
                  pl.BlockSpec((t, DKP), lambda g, i: (0, g), **resident),
                  pl.BlockSpec((1, t // tq, NSA_DV, tq), lambda g, i: (g, 0, 0, 0), **resident),
                  pl.BlockSpec((16, tq), lambda g, i: (g, i))],
        out_specs=pl.BlockSpec((tq, NSA_HPG * NSA_DV), lambda g, i: (i, g)),
        out_shape=jax.ShapeDtypeStruct((t, NSA_WIDTH), F32),
        scratch_shapes=[pltpu.VMEM((nb, r), F32), pltpu.VMEM((NSA_DV, r), F32)],
        compiler_params=_cparams(("arbitrary", "arbitrary")),
        name="nsa_prompt",
    )(qt, kc, vct, ks, vst, kw, vwt, gt)


def _nsa_sample_kernel(pt_ref, *refs, past, ts, n_keep):
    del pt_ref
    npg = _PAGES_PER_STEP
    kp, vp = refs[:npg], refs[npg:2 * npg]
    (qz_ref, qzp_ref, kc_ref, vc_ref, ksn_ref, vsn_ref, kwb_ref, vwb_ref, kwn_ref, vwn_ref, gate_ref,
     o_ref, sel_ref, acc_ref, ml_ref, ocw_ref) = refs[2 * npg:]
    c = pl.program_id(1)
    ng, nl = NSA_GROUPS, 128
    lg = nl // ng
    c2 = (NSA_DK ** -0.5) * LOG2E
    lane = lax.broadcasted_iota(jnp.int32, (1, nl), 1)
    tt = lane % ts
    t_row = past + tt
    lane_g = lane // lg

    def scores(k_of_g, q_ref, keys_on_lanes=False):
        dims = (((0,), (0,)), ((), ())) if keys_on_lanes else (((1,), (0,)), ((), ()))
        s = lax.dot_general(k_of_g(0), q_ref[0, 0], dims, preferred_element_type=F32)
        for g in range(1, ng):
            s = s + lax.dot_general(k_of_g(g), q_ref[0, g], dims, preferred_element_type=F32)
        return s * c2

    def group_rows(ref, g, n):
        return ref[0, pl.ds(g, n, stride=ng), :]

    def pv(v_of_g, p):
        o = None
        for g in range(ng):
            pg = jnp.where(lane_g == g, p, 0.0).astype(BF16)
            d = lax.dot_general(v_of_g(g), pg, (((0,), (0,)), ((), ())), preferred_element_type=F32)
            o = d if o is None else o + d
        return o

    def pad_rows(x):
        return jnp.concatenate([x, jnp.zeros((16 - ts, x.shape[1]), x.dtype)], axis=0).astype(BF16)

    def update(sc, mask, v_of_g):
        sc = jnp.where(mask, sc, NEG)
        m, l = ml_ref[0:1, :], ml_ref[1:2, :]
        m_new = jnp.maximum(m, jnp.max(sc, axis=0, keepdims=True))
        alpha = jnp.exp2(m - m_new)
        pp = jnp.exp2(sc - m_new)
        ml_ref[0:1, :] = m_new
        ml_ref[1:2, :] = l * alpha + jnp.sum(pp, axis=0, keepdims=True)
        acc_ref[...] = acc_ref[...] * alpha + pv(v_of_g, pp)

    @pl.when(c == 0)
    def _():
        nb = kc_ref.shape[2]
        s = scores(lambda g: kc_ref[0, g], qzp_ref)
        n_io = lax.broadcasted_iota(jnp.int32, (nb, nl), 0)
        p = _masked_softmax_cols(s, n_io < ((t_row + 1) >> 6))
        ocw_ref[0] = pv(lambda g: vc_ref[0, g], p)
        li = lax.broadcasted_iota(jnp.int32, (nl, nl), 0)
        lj = lax.broadcasted_iota(jnp.int32, (nl, nl), 1)
        same = ((li // lg) == (lj // lg)) & ((li % ts) == (lj % ts))
        mm = same.astype(F32).astype(BF16)
        hi = p.astype(BF16)
        r1 = p - hi.astype(F32)
        mid = r1.astype(BF16)
        lo = (r1 - mid.astype(F32)).astype(BF16)
        score = (jnp.dot(hi, mm, preferred_element_type=F32) + jnp.dot(mid, mm, preferred_element_type=F32)
                 + jnp.dot(lo, mm, preferred_element_type=F32))
        sel_ref[...] = _select_blocks(score, t_row >> 6, n_keep)
        w = kwb_ref.shape[4]
        s_buf = scores(lambda g: kwb_ref[0, 0, g].astype(BF16), qz_ref, True)
        s_new = scores(lambda g: pad_rows(kwn_ref[:, g * DKP:(g + 1) * DKP]), qzp_ref)
        sw = jnp.concatenate([s_buf, s_new], axis=0)
        r_io = lax.broadcasted_iota(jnp.int32, (w + 16, nl), 0)
        spos = past - w + r_io
        dt = t_row - spos
        pw = _masked_softmax_cols(sw, (dt >= 0) & (dt <= NSA_WINDOW) & (spos >= 0) & (r_io < w + ts))
        ocw_ref[1] = (pv(lambda g: group_rows(vwb_ref, g, w).astype(BF16), pw[:w])
                      + pv(lambda g: pad_rows(vwn_ref[:, g * NSA_DV:(g + 1) * NSA_DV]), pw[w:]))
        acc_ref[...] = jnp.zeros_like(acc_ref)
        ml_ref[...] = jnp.zeros_like(ml_ref)
        ml_ref[0:1, :] = jnp.full((1, nl), NEG, F32)

    nbk = 2 * npg
    sc = scores(lambda g: jnp.concatenate([kp[k][0, 0, g] for k in range(npg)], axis=1).astype(BF16),
                qz_ref, True)
    selt = sel_ref[pl.ds(pl.multiple_of(c * nbk, nbk), nbk), :]
    mask = jnp.broadcast_to(selt[:, None, :], (nbk, NSA_BLOCK, nl)).reshape(nbk * NSA_BLOCK, nl) > 0.0
    update(sc, mask,
           lambda g: jnp.concatenate([group_rows(vp[k], g, PAGE_SIZE) for k in range(npg)], axis=0).astype(BF16))

    @pl.when(c == pl.num_programs(1) - 1)
    def _():
        sn = scores(lambda g: pad_rows(ksn_ref[:, g * DKP:(g + 1) * DKP]), qzp_ref)
        u = lax.broadcasted_iota(jnp.int32, (16, nl), 0)
        update(sn, (u <= tt) & (u < ts), lambda g: pad_rows(vsn_ref[:, g * NSA_DV:(g + 1) * NSA_DV]))
        m, l = ml_ref[0:1, :], ml_ref[1:2, :]
        o_slc = jnp.where(m > 0.5 * NEG, acc_ref[...] / l, 0.0)
        o_ref[0] = gate_ref[0, 0:1, :] * ocw_ref[0] + gate_ref[0, 1:2, :] * o_slc + gate_ref[0, 2:3, :] * ocw_ref[1]


def _nsa_sample(page_table, cache_k, cache_v, qz, qzp, kc, vc, ksn, vsn, kwb, vwb, kwn, vwn, gates, ts):
    db, n_pages = page_table.shape
    npg = _PAGES_PER_STEP
    steps = n_pages // npg
    past = n_pages * PAGE_SIZE
    nb = past // NSA_BLOCK
    w = kwb.shape[4]
    assert NSA_HPG * ts * NSA_GROUPS == 128 and ts <= 16

    def kpage_spec(k):
        return pl.BlockSpec((1, 1, NSA_GROUPS, NSA_DK, PAGE_SIZE), lambda b, c, pt: (0, pt[b, c * npg + k], 0, 0, 0))

    def vpage_spec(k):
        return pl.BlockSpec((1, PAGE_SIZE * NSA_GROUPS, NSA_DV), lambda b, c, pt: (pt[b, c * npg + k], 0, 0))

    per_seq = lambda *shape: pl.BlockSpec((1,) + shape, lambda b, c, pt: (b,) + (0,) * len(shape))
    rows = lambda width: pl.BlockSpec((ts, width), lambda b, c, pt: (b, 0))
    kwin = pl.BlockSpec((1, 1, NSA_GROUPS, NSA_DK, w), lambda b, c, pt: (0, b, 0, 0, 0))
    vwin = pl.BlockSpec((1, w * NSA_GROUPS, NSA_DV), lambda b, c, pt: (b, 0, 0))
    grid_spec = pltpu.PrefetchScalarGridSpec(
        num_scalar_prefetch=1,
        grid=(db, steps),
        in_specs=([kpage_spec(k) for k in range(npg)] + [vpage_spec(k) for k in range(npg)] + [
            per_seq(NSA_GROUPS, NSA_DK, 128), per_seq(NSA_GROUPS, DKP, 128),
            per_seq(NSA_GROUPS, nb, DKP), per_seq(NSA_GROUPS, nb, NSA_DV),
            rows(NSA_GROUPS * DKP), rows(NSA_GROUPS * NSA_DV),
            kwin, vwin,
            rows(NSA_GROUPS * DKP), rows(NSA_GROUPS * NSA_DV),
            per_seq(8, 128)]),
        out_specs=per_seq(NSA_DV, 128),
        scratch_shapes=[pltpu.VMEM((nb, 128), F32), pltpu.VMEM((NSA_DV, 128), F32),
                        pltpu.VMEM((8, 128), F32), pltpu.VMEM((2, NSA_DV, 128), F32)])
    return pl.pallas_call(
        functools.partial(_nsa_sample_kernel, past=past, ts=ts, n_keep=min(NSA_N_SEL - 1, nb)),
        grid_spec=grid_spec,
        out_shape=jax.ShapeDtypeStruct((db, NSA_DV, 128), F32),
        compiler_params=_cparams(("arbitrary", "arbitrary")),
        name="nsa_sample",
    )(page_table, *([cache_k] * npg), *([cache_v] * npg), qz, qzp, kc, vc, ksn, vsn, kwb, vwb, kwn, vwn, gates)


def _split3(x):
    hi = x.astype(BF16)
    r1 = x - hi.astype(F32)
    mid = r1.astype(BF16)
    lo = (r1 - mid.astype(F32)).astype(BF16)
    return jnp.concatenate([hi, mid, lo], axis=1)


def _gla_kernel(q_ref, k_ref, v_ref, a_ref, wa_ref, ba_ref, s0_ref, o_ref, sout_ref, st_ref,
                *, c, n_sub, n_valid):
    ci = pl.program_id(2)
    levels = int(math.log2(c))

    @pl.when(ci == 0)
    def _():
        st_ref[...] = s0_ref[0, 0].T

    t_io = lax.broadcasted_iota(jnp.int32, (c, c), 0)
    s_io = lax.broadcasted_iota(jnp.int32, (c, c), 1)
    row = lax.broadcasted_iota(jnp.int32, (c, 1), 0)
    sels = [s_io <= t_io]
    for lv in range(1, levels + 1):
        sels.append(s_io <= ((t_io >> lv) << lv) + (1 << (lv - 1)) - 1)
    sel_all = jnp.concatenate(sels, axis=0).astype(F32).astype(BF16)
    dk = GLA_DK
    for u in range(n_sub):
        rows = slice(u * c, (u + 1) * c)
        q = q_ref[rows, :] * (GLA_DK ** -0.5)
        k = k_ref[rows, :]
        v = v_ref[rows, :].astype(BF16)
        z = jnp.dot(a_ref[rows, :].astype(BF16), wa_ref[...], preferred_element_type=F32) + ba_ref[...]
        la = (jnp.minimum(z, 0.0) - jnp.log1p(jnp.exp(-jnp.abs(z)))) * (1.0 / GLA_TAU)
        if n_valid < c:
            la = jnp.where(row < n_valid, la, 0.0)
        big = jnp.dot(sel_all, _split3(la), preferred_element_type=F32)
        big = big[:, 0:dk] + big[:, dk:2 * dk] + big[:, 2 * dk:3 * dk]
        b = big[0:c]
        st = st_ref[...]
        o = lax.dot_general((q * jnp.exp(b)).astype(BF16), st.astype(BF16),
                            (((1,), (1,)), ((), ())), preferred_element_type=F32)
        a_mat = jnp.where(t_io == s_io, jnp.sum(q * k, axis=1, keepdims=True), 0.0)
        for lv in range(1, levels + 1):
            bref = big[lv * c:(lv + 1) * c]
            upper = ((row >> (lv - 1)) & 1) == 1
            ql = jnp.where(upper, q * jnp.exp(b - bref), 0.0).astype(BF16)
            kl = jnp.where(upper, 0.0, k * jnp.exp(bref - b)).astype(BF16)
            al = lax.dot_general(ql, kl, (((1,), (1,)), ((), ())), preferred_element_type=F32)
            a_mat = a_mat + jnp.where((t_io >> lv) == (s_io >> lv), al, 0.0)
        o = o + jnp.dot(a_mat.astype(BF16), v, preferred_element_type=F32)
        o_ref[rows, :] = o
        b_last = b[c - 1:c, :]
        kd = (k * jnp.exp(b_last - b)).astype(BF16)
        st_ref[...] = st * jnp.exp(b_last) + lax.dot_general(
            v, kd, (((0,), (0,)), ((), ())), preferred_element_type=F32)

    @pl.when(ci == pl.num_programs(2) - 1)
    def _():
        sout_ref[0, 0] = st_ref[...].T


def _gla(p, n_batch, t_len, s0, wa_pad, ba, c, n_sub, n_valid):
    step = c * n_sub
    ncs = t_len // step
    qb, kb = _SEG["q_g"] // GLA_DK, _SEG["k_g"] // GLA_DK
    vb, ab = _SEG["v_g"] // GLA_DV, _SEG["a_g"] // 256
    return pl.pallas_call(
        functools.partial(_gla_kernel, c=c, n_sub=n_sub, n_valid=n_valid),
        grid=(n_batch, GLA_HEADS, ncs),
        in_specs=[pl.BlockSpec((step, GLA_DK), lambda b, h, ci: (b * ncs + ci, qb + h)),
                  pl.BlockSpec((step, GLA_DK), lambda b, h, ci: (b * ncs + ci, kb + h)),
                  pl.BlockSpec((step, GLA_DV), lambda b, h, ci: (b * ncs + ci, vb + h)),
                  pl.BlockSpec((step, 256), lambda b, h, ci: (b * ncs + ci, ab)),
                  pl.BlockSpec((256, GLA_DK), lambda b, h, ci: (0, h)),
                  pl.BlockSpec((1, GLA_DK), lambda b, h, ci: (0, h)),
                  pl.BlockSpec((1, 1, GLA_DK, GLA_DV), lambda b, h, ci: (b, h, 0, 0))],
        out_specs=[pl.BlockSpec((step, GLA_DV), lambda b, h, ci: (b * ncs + ci, h)),
                   pl.BlockSpec((1, 1, GLA_DK, GLA_DV), lambda b, h, ci: (b, h, 0, 0))],
        out_shape=[jax.ShapeDtypeStruct((n_batch * t_len, GLA_WIDTH), F32),
                   jax.ShapeDtypeStruct((n_batch, GLA_HEADS, GLA_DK, GLA_DV), F32)],
        scratch_shapes=[pltpu.VMEM((GLA_DV, GLA_DK), F32)],
        compiler_params=_cparams(("arbitrary", "arbitrary", "arbitrary")),
        name="gla",
    )(p, p, p, p, wa_pad, ba, s0)


def _merge1_kernel(on_ref, zn_ref, og_ref, zg_ref, gg_ref, wa_ref, wb_ref, ma_ref, mb_ref, o_ref,
                   a_ref, b_ref):
    @pl.when(pl.program_id(1) == 0)
    def _():
        a_ref[...] = (on_ref[...] * _silu(zn_ref[...])).astype(BF16)
        og = og_ref[...]
        zg = zg_ref[...]
        for h in range(GLA_HEADS):
            sl = slice(h * GLA_DV, (h + 1) * GLA_DV)
            x = og[:, sl]
            y = x * lax.rsqrt(jnp.mean(x * x, axis=-1, keepdims=True) + NORM_EPS) * gg_ref[...]
            b_ref[:, sl] = (y * _silu(zg[:, sl])).astype(BF16)

    ua = jnp.dot(a_ref[...], wa_ref[...], preferred_element_type=F32)
    ub = jnp.dot(b_ref[...], wb_ref[...], preferred_element_type=F32)
    o_ref[...] = (jax.nn.sigmoid(ma_ref[...]) * ua + jax.nn.sigmoid(mb_ref[...]) * ub).astype(BF16)


def _merge1(o_nsa, o_gla, p, gla_g, wb_a, wb_b, tm):
    m = p.shape[0]
    tn = 512
    zn, zg, mg = _SEG["z_nsa"] // NSA_WIDTH, _SEG["z_g"] // GLA_WIDTH, _SEG["m_gate"] // tn
    row = lambda i, j: (i, 0)
    return pl.pallas_call(
        _merge1_kernel,
        grid=(m // tm, D_MODEL // tn),
        in_specs=[pl.BlockSpec((tm, NSA_WIDTH), row),
                  pl.BlockSpec((tm, NSA_WIDTH), lambda i, j: (i, zn)),
                  pl.BlockSpec((tm, GLA_WIDTH), row),
                  pl.BlockSpec((tm, GLA_WIDTH), lambda i, j: (i, zg)),
                  pl.BlockSpec((1, GLA_DV), lambda i, j: (0, 0)),
                  pl.BlockSpec((NSA_WIDTH, tn), lambda i, j: (0, j)),
                  pl.BlockSpec((GLA_WIDTH, tn), lambda i, j: (0, j)),
                  pl.BlockSpec((tm, tn), lambda i, j: (i, mg + j)),
                  pl.BlockSpec((tm, tn), lambda i, j: (i, mg + D_MODEL // tn + j))],
        out_specs=pl.BlockSpec((tm, tn), lambda i, j: (i, j)),
        out_shape=jax.ShapeDtypeStruct((m, D_MODEL), BF16),
        scratch_shapes=[pltpu.VMEM((tm, NSA_WIDTH), BF16), pltpu.VMEM((tm, GLA_WIDTH), BF16)],
        compiler_params=_cparams(("arbitrary", "arbitrary")),
        name="merge1",
    )(o_nsa, p, o_gla, p, gla_g, wb_a, wb_b, p, p)


def _merge2_kernel(x_ref, gate_ref, mg_ref, w_ref, o_ref):
    o_ref[...] = x_ref[...] + gate_ref[...] * jnp.dot(mg_ref[...], w_ref[...], preferred_element_type=F32)


def _merge2(x, gate, merged, w_out, tm):
    m, d = x.shape
    tn = 512
    per_row = gate.shape[0] != 1
    gate_spec = (pl.BlockSpec((tm, tn), lambda i, j: (i, j)) if per_row
                 else pl.BlockSpec((1, tn), lambda i, j: (0, j)))
    return pl.pallas_call(
        _merge2_kernel,
        grid=(m // tm, d // tn),
        in_specs=[pl.BlockSpec((tm, tn), lambda i, j: (i, j)), gate_spec,
                  pl.BlockSpec((tm, d), lambda i, j: (i, 0)),
                  pl.BlockSpec((d, tn), lambda i, j: (0, j))],
        out_specs=pl.BlockSpec((tm, tn), lambda i, j: (i, j)),
        out_shape=jax.ShapeDtypeStruct((m, d), F32),
        compiler_params=_cparams(("arbitrary", "arbitrary")),
        name="merge2",
    )(x, gate, merged, w_out)


def _pad_last(a, width):
    return jnp.pad(a, [(0, 0)] * (a.ndim - 1) + [(0, width - a.shape[-1])])


def _pad_heads(w, n, src, dst):
    return _pad_last(w.reshape(w.shape[0], n, src), dst).reshape(w.shape[0], n * dst)


def _pad_w_in(w_in):
    sizes = (NSA_HEADS * NSA_DK, NSA_GROUPS * NSA_DK, NSA_GROUPS * NSA_DV, NSA_GROUPS * NSA_DK,
             NSA_GROUPS * NSA_DV, NSA_GROUPS * NSA_DK, NSA_GROUPS * NSA_DV, 3 * NSA_HEADS, NSA_WIDTH,
             GLA_HEADS * GLA_DK, GLA_HEADS * GLA_DK, GLA_WIDTH, GLA_WIDTH, GLA_RANK, 2 * D_MODEL)
    cuts = [0]
    for s in sizes:
        cuts.append(cuts[-1] + s)
    names = ("q", "k_cmp", "v_cmp", "k_slc", "v_slc", "k_win", "v_win", "g_nsa", "z_nsa",
             "q_g", "k_g", "v_g", "z_g", "a_g", "m_gate")
    parts = {nm: w_in[:, cuts[n]:cuts[n + 1]] for n, nm in enumerate(names)}
    parts["q"] = _pad_heads(parts["q"], NSA_HEADS, NSA_DK, DKP)
    for nm in ("k_cmp", "k_slc", "k_win"):
        parts[nm] = _pad_heads(parts[nm], NSA_GROUPS, NSA_DK, DKP)
    parts["g_nsa"] = _pad_last(_pad_heads(parts["g_nsa"], NSA_GROUPS, 3 * NSA_HPG, 16), 256)
    parts["a_g"] = _pad_last(parts["a_g"], 256)
    return jnp.concatenate([parts[nm] for nm in _SEG], axis=1).astype(BF16)


def _unpad_heads(a, n):
    return a.reshape(a.shape[0], n, DKP)[:, :, :NSA_DK]


def kernel(x_prompt, x_sample, cache_k_cmp, cache_v_cmp, cache_k_slc, cache_v_slc, state_k_win, state_v_win, state_gla, page_table, c_prompt, c_sample, ln_g, w_ada, b_ada, w_in, q_norm_g, k_cmp_norm_g, k_slc_norm_g, k_win_norm_g, pe_k, w1_k, w2_k, pe_v, w1_v, w2_v, w_a2, b_a, gla_norm_g, w_branch, w_out):
    assert ln_g.shape[0] == 1, "single layer"
    bp, t, d = x_prompt.shape
    db, ts, _ = x_sample.shape
    assert bp == 1 and d == D_MODEL
    G = NSA_GROUPS

    w_pad = _pad_w_in(w_in[0])
    g_q = _pad_last(q_norm_g, DKP)
    g_kc = _pad_last(k_cmp_norm_g, DKP)
    g_ks = _pad_last(k_slc_norm_g, DKP)
    g_kw = _pad_last(k_win_norm_g, DKP)
    pe_k_p = _pad_last(pe_k[0], DKP)[:, None, :]
    w1_k_p = jnp.pad(w1_k[0], ((0, 0), (0, DKP - NSA_DK), (0, 0))).astype(BF16)
    w2_k_p = _pad_last(w2_k[0], DKP).astype(BF16)
    pe_v_p = pe_v[0][:, None, :]
    w1_v_p = w1_v[0].astype(BF16)
    w2_v_p = w2_v[0].astype(BF16)
    wa_pad = jnp.pad(w_a2[0], ((0, 256 - GLA_RANK), (0, 0))).astype(BF16)
    wb_a = w_branch[0, :NSA_WIDTH].astype(BF16)
    wb_b = w_branch[0, NSA_WIDTH:].astype(BF16)
    w_o = w_out[0].astype(BF16)
    ones_v = jnp.ones((1, NSA_DV), F32)

    n_c = bp + db
    c_all = jnp.pad(jnp.concatenate([c_prompt, c_sample], axis=0), ((0, (-n_c) % 8), (0, 0)))
    mod = _ada(c_all, w_ada[0], b_ada)
    shift, scale, gate = mod[:, :d], mod[:, d:2 * d], mod[:, 2 * d:]

    tm = min(1024, t)
    xp = x_prompt.reshape(t, d)
    pp = _proj(xp, scale[0:1], shift[0:1], ln_g, w_pad, tm)
    tq = min(256, t)
    tk = min(512, t)
    qt = _prep_q(pp, g_q, min(512, t))
    ks_f, ks_b = _prep_k(pp, "k_slc", g_ks, min(512, t))
    kw_f, kw_b = _prep_k(pp, "k_win", g_kw, min(512, t))
    vst = _prep_vt(pp, "v_slc", tk)
    vwt = _prep_vt(pp, "v_win", tq)
    gt = _prep_gate(pp, min(512, t))
    kc = _compress_prompt(pp, "k_cmp", DKP, pe_k_p, w1_k_p, w2_k_p, g_kc, True, False)
    vct = _compress_prompt(pp, "v_cmp", NSA_DV, pe_v_p, w1_v_p, w2_v_p, ones_v, False, True)
    o_nsa_p = _nsa_prompt(qt, kc, vct, ks_b, vst, kw_b, vwt, gt, tq, tk)
    s0_p = jnp.zeros((bp, GLA_HEADS, GLA_DK, GLA_DV), F32)
    o_gla_p, gla_p = _gla(pp, bp, t, s0_p, wa_pad, b_a, GLA_CHUNK, 4, GLA_CHUNK)
    merged_p = _merge1(o_nsa_p, o_gla_p, pp, gla_norm_g, wb_a, wb_b, min(512, t))
    y_p = _merge2(xp, gate[0:1], merged_p, w_o, min(512, t)).reshape(bp, t, d)

    wp = min(NSA_WINDOW, t)
    k_cmp_p = _unpad_heads(pp[:, _SEG["k_cmp"]:_SEG["k_cmp"] + G * DKP], G)
    v_cmp_p = pp[:, _SEG["v_cmp"]:_SEG["v_cmp"] + G * NSA_DV].reshape(t, G, NSA_DV)
    k_slc_p = _unpad_heads(ks_f, G)
    v_slc_p = pp[:, _SEG["v_slc"]:_SEG["v_slc"] + G * NSA_DV].reshape(t, G, NSA_DV)
    k_win_p = _unpad_heads(kw_f, G)[t - wp:]
    v_win_p = pp[t - wp:, _SEG["v_win"]:_SEG["v_win"] + G * NSA_DV].reshape(wp, G, NSA_DV)

    ms = db * ts
    xs = x_sample.reshape(ms, d)
    rep = lambda a: jnp.repeat(a[bp:bp + db], ts, axis=0)
    ps = _proj(xs, rep(scale), rep(shift), ln_g, w_pad, ms)
    seg = lambda name, w: ps[:, _SEG[name]:_SEG[name] + w]
    qt_s = _prep_q(ps, g_q, ms)
    kss_f, _ = _prep_k(ps, "k_slc", g_ks, ms)
    kws_f, _ = _prep_k(ps, "k_win", g_kw, ms)
    gt_s = _prep_gate(ps, ms)
    v_slc_new = seg("v_slc", G * NSA_DV)
    v_win_new = seg("v_win", G * NSA_DV)
    k_cmp_s = _unpad_heads(seg("k_cmp", G * DKP), G).reshape(db, ts, G, NSA_DK)
    v_cmp_s = seg("v_cmp", G * NSA_DV).reshape(db, ts, G, NSA_DV)
    k_slc_s = _unpad_heads(kss_f, G).reshape(db, ts, G, NSA_DK)
    v_slc_s = v_slc_new.reshape(db, ts, G, NSA_DV)
    k_win_s_new = _unpad_heads(kws_f, G).reshape(db, ts, G, NSA_DK)
    v_win_s_new = v_win_new.reshape(db, ts, G, NSA_DV)
    q4 = qt_s.reshape(G, NSA_HPG, DKP, db, ts).transpose(3, 0, 2, 1, 4).reshape(db, G, DKP, NSA_HPG * ts)
    own = (jnp.arange(G)[:, None] == jnp.arange(G)[None, :])[None, :, None, :, None]
    qzp = jnp.where(own, q4[:, :, :, None, :], jnp.zeros((), BF16)).reshape(db, G, DKP, G * NSA_HPG * ts)
    qz = qzp[:, :, :NSA_DK, :]
    gates = gt_s.reshape(G, 16, db, ts)[:, :3 * NSA_HPG].reshape(G, NSA_HPG, 3, db, ts)
    gates = gates.transpose(3, 2, 0, 1, 4).reshape(db, 3, G * NSA_HPG * ts)
    gates = jnp.pad(gates, ((0, 0), (0, 5), (0, 0)))
    keys_t = lambda a: jnp.transpose(a, (0, 1, 3, 4, 2))
    vals_2d = lambda a: a.reshape(a.shape[1], a.shape[2] * G, NSA_DV)
    kc_s = _compress_paged(keys_t(cache_k_cmp), page_table, pe_k_p[:, 0, :], w1_k_p, w2_k_p, g_kc, True)
    vc_s = _compress_paged(vals_2d(cache_v_cmp), page_table, pe_v[0], w1_v_p, w2_v_p, ones_v, False)
    o_t = _nsa_sample(page_table, keys_t(cache_k_slc), vals_2d(cache_v_slc), qz, qzp, kc_s, vc_s, kss_f,
                      v_slc_new, keys_t(state_k_win), vals_2d(state_v_win), kws_f, v_win_new, gates, ts)
    o_nsa_s = o_t.reshape(db, NSA_DV, G, NSA_HPG, ts).transpose(0, 4, 2, 3, 1)
    kw_s = jnp.concatenate([state_k_win[0][:, ts:], k_win_s_new], axis=1)
    vw_s = jnp.concatenate([state_v_win[0][:, ts:], v_win_s_new], axis=1)
    c_s = 16
    ps_pad = jnp.pad(ps.reshape(db, ts, D_PAD), ((0, 0), (0, c_s - ts), (0, 0))).reshape(db * c_s, D_PAD)
    o_gla_s, gla_s = _gla(ps_pad, db, c_s, state_gla[0], wa_pad, b_a, c_s, 1, ts)
    o_gla_s = o_gla_s.reshape(db, c_s, GLA_WIDTH)[:, :ts].reshape(ms, GLA_WIDTH)
    merged_s = _merge1(o_nsa_s.reshape(ms, NSA_WIDTH), o_gla_s, ps, gla_norm_g, wb_a, wb_b, ms)
    y_s = _merge2(xs, rep(gate), merged_s, w_o, ms).reshape(db, ts, d)

    L1 = lambda a: a[None, None]
    return (y_p, y_s,
            L1(k_cmp_p), L1(v_cmp_p), L1(k_slc_p), L1(v_slc_p), L1(k_win_p), L1(v_win_p), gla_p[None],
            k_cmp_s[None], v_cmp_s[None], k_slc_s[None], v_slc_s[None], kw_s[None], vw_s[None], gla_s[None])
```

```python
import functools
import math

import jax
import jax.numpy as jnp
from jax import lax
from jax.experimental import pallas as pl
from jax.experimental.pallas import tpu as pltpu

F32 = jnp.float32
BF16 = jnp.bfloat16

NSA_HEADS = 16
NSA_GROUPS = 4
NSA_HPG = NSA_HEADS // NSA_GROUPS
NSA_DK = 192
NSA_DV = 128
NSA_BLOCK = 64
NSA_N_SEL = 16
NSA_WINDOW = 512
NSA_CMP_HIDDEN = 256
PAGE_SIZE = 128
GLA_HEADS = 4
GLA_DK = 256
GLA_DV = 512
GLA_RANK = 16
GLA_TAU = 16.0
GLA_CHUNK = 64
NORM_EPS = 1e-6
D_MODEL = 2048
NSA_WIDTH = NSA_HEADS * NSA_DV
GLA_WIDTH = GLA_HEADS * GLA_DV

DKP = 256
LOG2E = 1.4426950408889634
NEG = -1e30
Q_SCALE = (NSA_DK ** -0.5) * LOG2E
VMEM_LIMIT = 56 * 1024 * 1024

_SEG = {}
_off = 0
for _name, _w in (("z_nsa", NSA_WIDTH), ("z_g", GLA_WIDTH), ("v_g", GLA_WIDTH), ("m_gate", 2 * D_MODEL),
                  ("q", NSA_HEADS * DKP), ("k_cmp", NSA_GROUPS * DKP), ("v_cmp", NSA_GROUPS * NSA_DV),
                  ("k_slc", NSA_GROUPS * DKP), ("v_slc", NSA_GROUPS * NSA_DV),
                  ("k_win", NSA_GROUPS * DKP), ("v_win", NSA_GROUPS * NSA_DV),
                  ("q_g", GLA_HEADS * GLA_DK), ("k_g", GLA_HEADS * GLA_DK),
                  ("g_nsa", 256), ("a_g", 256)):
    _SEG[_name] = _off
    _off += _w
D_PAD = _off


def _cparams(sem):
    return pltpu.CompilerParams(dimension_semantics=sem, vmem_limit_bytes=VMEM_LIMIT)


def _silu(x):
    return x * jax.nn.sigmoid(x)


def _ada_kernel(c_ref, w_ref, b_ref, o_ref):
    a = _silu(c_ref[...]).astype(BF16)
    o_ref[...] = jnp.dot(a, w_ref[...].astype(BF16), preferred_element_type=F32) + b_ref[...]


def _ada(c_all, w_ada, b_ada):
    m, d = c_all.shape
    n = w_ada.shape[1]
    tn = 768
    return pl.pallas_call(
        _ada_kernel,
        grid=(n // tn,),
        in_specs=[pl.BlockSpec((m, d), lambda j: (0, 0)),
                  pl.BlockSpec((d, tn), lambda j: (0, j)),
                  pl.BlockSpec((1, tn), lambda j: (0, j))],
        out_specs=pl.BlockSpec((m, tn), lambda j: (0, j)),
        out_shape=jax.ShapeDtypeStruct((m, n), F32),
        compiler_params=_cparams(("arbitrary",)),
        name="ada",
    )(c_all, w_ada, b_ada)


def _proj_kernel(x_ref, sc_ref, sh_ref, g_ref, w_ref, o_ref, h_ref):
    @pl.when(pl.program_id(1) == 0)
    def _():
        x = x_ref[...]
        y = x * lax.rsqrt(jnp.mean(x * x, axis=-1, keepdims=True) + NORM_EPS) * g_ref[...]
        h_ref[...] = (y * (1.0 + sc_ref[...]) + sh_ref[...]).astype(BF16)

    o_ref[...] = jnp.dot(h_ref[...], w_ref[...], preferred_element_type=F32)


def _proj(x, scale, shift, ln_g, w_pad, tm):
    m, d = x.shape
    n = w_pad.shape[1]
    tn = 1536
    per_row = scale.shape[0] != 1
    mod_spec = (pl.BlockSpec((tm, d), lambda i, j: (i, 0)) if per_row
                else pl.BlockSpec((1, d), lambda i, j: (0, 0)))
    return pl.pallas_call(
        _proj_kernel,
        grid=(m // tm, n // tn),
        in_specs=[pl.BlockSpec((tm, d), lambda i, j: (i, 0)), mod_spec, mod_spec,
                  pl.BlockSpec((1, d), lambda i, j: (0, 0)),
                  pl.BlockSpec((d, tn), lambda i, j: (0, j))],
        out_specs=pl.BlockSpec((tm, tn), lambda i, j: (i, j)),
        out_shape=jax.ShapeDtypeStruct((m, n), F32),
        scratch_shapes=[pltpu.VMEM((tm, d), BF16)],
        compiler_params=_cparams(("arbitrary", "arbitrary")),
        name="proj",
    )(x, scale, shift, ln_g, w_pad)


def _head_norm(x, g):
    ms = jnp.sum(x * x, axis=-1, keepdims=True) * (1.0 / NSA_DK)
    return x * lax.rsqrt(ms + NORM_EPS) * g


def _prep_q_kernel(x_ref, g_ref, o_ref):
    o_ref[0] = (_head_norm(x_ref[...], g_ref[...]) * Q_SCALE).T.astype(BF16)


def _prep_q(p, g_pad, tm):
    t = p.shape[0]
    base = _SEG["q"] // DKP
    return pl.pallas_call(
        _prep_q_kernel,
        grid=(t // tm, NSA_HEADS),
        in_specs=[pl.BlockSpec((tm, DKP), lambda i, h: (i, base + h)),
                  pl.BlockSpec((1, DKP), lambda i, h: (0, 0))],
        out_specs=pl.BlockSpec((1, DKP, tm), lambda i, h: (h, 0, i)),
        out_shape=jax.ShapeDtypeStruct((NSA_HEADS, DKP, t), BF16),
        compiler_params=_cparams(("arbitrary", "arbitrary")),
        name="prep_q",
    )(p, g_pad)


def _prep_k_kernel(x_ref, g_ref, o_ref, ob_ref):
    y = _head_norm(x_ref[...], g_ref[...])
    o_ref[...] = y
    ob_ref[...] = y.astype(BF16)


def _prep_k(p, seg, g_pad, tm):
    t = p.shape[0]
    base = _SEG[seg] // DKP
    n = NSA_GROUPS * DKP
    return pl.pallas_call(
        _prep_k_kernel,
        grid=(t // tm, NSA_GROUPS),
        in_specs=[pl.BlockSpec((tm, DKP), lambda i, g: (i, base + g)),
                  pl.BlockSpec((1, DKP), lambda i, g: (0, 0))],
        out_specs=[pl.BlockSpec((tm, DKP), lambda i, g: (i, g)),
                   pl.BlockSpec((tm, DKP), lambda i, g: (i, g))],
        out_shape=[jax.ShapeDtypeStruct((t, n), F32), jax.ShapeDtypeStruct((t, n), BF16)],
        compiler_params=_cparams(("arbitrary", "arbitrary")),
        name="prep_k_" + seg,
    )(p, g_pad)


ONES_ROWS = 16


def _prep_vt_kernel(x_ref, o_ref, *, ones_rows):
    xt = x_ref[...].T
    if ones_rows:
        xt = jnp.concatenate([xt, jnp.ones((ones_rows, xt.shape[1]), F32)], axis=0)
    o_ref[0, 0] = xt.astype(BF16)


def _prep_vt(p, seg, tile, ones_rows=0):
    t = p.shape[0]
    base = _SEG[seg] // NSA_DV
    rows = NSA_DV + ones_rows
    return pl.pallas_call(
        functools.partial(_prep_vt_kernel, ones_rows=ones_rows),
        grid=(t // tile, NSA_GROUPS),
        in_specs=[pl.BlockSpec((tile, NSA_DV), lambda i, g: (i, base + g))],
        out_specs=pl.BlockSpec((1, 1, rows, tile), lambda i, g: (g, i, 0, 0)),
        out_shape=jax.ShapeDtypeStruct((NSA_GROUPS, t // tile, rows, tile), BF16),
        compiler_params=_cparams(("arbitrary", "arbitrary")),
        name="prep_vt_" + seg,
    )(p)


def _prep_gate_kernel(x_ref, o_ref):
    o_ref[...] = jax.nn.sigmoid(x_ref[...]).T[:4 * NSA_HEADS, :]


def _prep_gate(p, tm):
    t = p.shape[0]
    base = _SEG["g_nsa"] // 256
    return pl.pallas_call(
        _prep_gate_kernel,
        grid=(t // tm,),
        in_specs=[pl.BlockSpec((tm, 256), lambda i: (i, base))],
        out_specs=pl.BlockSpec((4 * NSA_HEADS, tm), lambda i: (0, i)),
        out_shape=jax.ShapeDtypeStruct((4 * NSA_HEADS, t), F32),
        compiler_params=_cparams(("arbitrary",)),
        name="prep_gate",
    )(p)


_CMP_ROWS = 8


def _cmp_kernel(x_ref, pe_ref, w1_ref, w2_ref, g_ref, o_ref, acc_ref, *, norm, transpose_out):
    s = pl.program_id(1)

    @pl.when(s == 0)
    def _():
        acc_ref[...] = jnp.zeros_like(acc_ref)

    acc = acc_ref[...]
    for r in range(_CMP_ROWS):
        xb = (x_ref[:, r, :] + pe_ref[r]).astype(BF16)
        acc = acc + jnp.dot(xb, w1_ref[r], preferred_element_type=F32)
    acc_ref[...] = acc

    @pl.when(s == pl.num_programs(1) - 1)
    def _():
        hid = _silu(acc_ref[...]).astype(BF16)
        y = jnp.dot(hid, w2_ref[...], preferred_element_type=F32)
        if norm:
            y = _head_norm(y, g_ref[...])
        if transpose_out:
            y = y.T
        o_ref[0] = y.astype(o_ref.dtype)


def _compress_prompt(p, seg, dp, pe_pad, w1_pad, w2_pad, g_pad, norm, transpose_out):
    t = p.shape[0]
    nb = t // NSA_BLOCK
    pv = p.reshape(nb, NSA_BLOCK, D_PAD)
    base = _SEG[seg] // dp
    out_shape = (NSA_GROUPS, dp, nb) if transpose_out else (NSA_GROUPS, nb, dp)
    return pl.pallas_call(
        functools.partial(_cmp_kernel, norm=norm, transpose_out=transpose_out),
        grid=(NSA_GROUPS, NSA_BLOCK // _CMP_ROWS),
        in_specs=[pl.BlockSpec((nb, _CMP_ROWS, dp), lambda g, s: (0, s, base + g)),
                  pl.BlockSpec((_CMP_ROWS, 1, dp), lambda g, s: (s, 0, 0)),
                  pl.BlockSpec((_CMP_ROWS, dp, NSA_CMP_HIDDEN), lambda g, s: (s, 0, 0)),
                  pl.BlockSpec((NSA_CMP_HIDDEN, dp), lambda g, s: (0, 0)),
                  pl.BlockSpec((1, dp), lambda g, s: (0, 0))],
        out_specs=pl.BlockSpec((1,) + out_shape[1:], lambda g, s: (g, 0, 0)),
        out_shape=jax.ShapeDtypeStruct(out_shape, BF16),
        scratch_shapes=[pltpu.VMEM((nb, NSA_CMP_HIDDEN), F32)],
        compiler_params=_cparams(("arbitrary", "arbitrary")),
        name="cmp_" + seg,
    )(pv, pe_pad, w1_pad, w2_pad, g_pad)


_PAGES_PER_STEP = 8
_STAGE_PITCH = 72


def _cmp_paged_kernel(pt_ref, *refs, dh, keys_on_lanes, norm):
    del pt_ref
    npg = _PAGES_PER_STEP
    pages = refs[:npg]
    pe_ref, w1_ref, w2_ref, g_ref, o_ref, stage_ref, xs_ref = refs[npg:]
    c = pl.program_id(2)
    nbc = 2 * npg
    n_slab = stage_ref.shape[1]

    @pl.when((pl.program_id(0) == 0) & (pl.program_id(1) == 0) & (c == 0))
    def _():
        stage_ref[...] = jnp.zeros_like(stage_ref)

    for k in range(npg):
        for g in range(NSA_GROUPS):
            if keys_on_lanes:
                x = pages[k][0, 0, g].T
            else:
                x = pages[k][0, pl.ds(g, PAGE_SIZE, stride=NSA_GROUPS), :]
            for blk in range(2):
                r0 = (2 * k + blk) * _STAGE_PITCH
                xb = x[blk * NSA_BLOCK:(blk + 1) * NSA_BLOCK]
                stage_ref[g, 0, r0:r0 + NSA_BLOCK, :] = xb[:, 0:128]
                if n_slab == 2:
                    stage_ref[g, 1, r0:r0 + NSA_BLOCK, 0:dh - 128] = xb[:, 128:dh]
    dp = pe_ref.shape[1]
    row0 = pl.multiple_of(c * nbc, nbc)
    for s in range(NSA_BLOCK):
        for g in range(NSA_GROUPS):
            parts = [stage_ref[g, j, pl.ds(s, nbc, stride=_STAGE_PITCH), :] for j in range(n_slab)]
            x = parts[0] if n_slab == 1 else jnp.concatenate(parts, axis=1)
            xs_ref[g, pl.ds(row0, nbc), s * dp:(s + 1) * dp] = (x + pe_ref[s:s + 1, :]).astype(BF16)

    @pl.when(c == pl.num_programs(2) - 1)
    def _():
        for g in range(NSA_GROUPS):
            acc = jnp.dot(xs_ref[g], w1_ref[...], preferred_element_type=F32)
            hid = _silu(acc).astype(BF16)
            y = jnp.dot(hid, w2_ref[...], preferred_element_type=F32)
            if norm:
                y = _head_norm(y, g_ref[...])
            o_ref[0, g] = y.astype(BF16)


def _compress_paged(cache, page_table, pe_pad, w1_pad, w2_pad, g_pad, norm):
    keys_on_lanes = cache.ndim == 5
    dh = cache.shape[3] if keys_on_lanes else cache.shape[2]
    dp = pe_pad.shape[-1]
    db, n_pages = page_table.shape
    npg = _PAGES_PER_STEP
    steps = n_pages // npg
    spp = min(8, steps)
    m_blocks = spp * 2 * npg
    n_ph = steps // spp
    n_slab = dp // 128

    def page_spec(k):
        page = lambda b, ph, c, pt: pt[b, (ph * spp + c) * npg + k]
        if keys_on_lanes:
            return pl.BlockSpec((1, 1, NSA_GROUPS, dh, PAGE_SIZE), lambda b, ph, c, pt: (0, page(b, ph, c, pt), 0, 0, 0))
        return pl.BlockSpec((1, PAGE_SIZE * NSA_GROUPS, dh), lambda b, ph, c, pt: (page(b, ph, c, pt), 0, 0))

    const = lambda *shape: pl.BlockSpec(shape, lambda b, ph, c, pt: (0,) * len(shape))
    grid_spec = pltpu.PrefetchScalarGridSpec(
        num_scalar_prefetch=1,
        grid=(db, n_ph, spp),
        in_specs=[page_spec(k) for k in range(npg)] + [
            const(NSA_BLOCK, dp),
            pl.BlockSpec((NSA_BLOCK * dp, NSA_CMP_HIDDEN), lambda b, ph, c, pt: (0, 0),
                         pipeline_mode=pl.Buffered(1)),
            const(NSA_CMP_HIDDEN, dp), const(1, dp)],
        out_specs=pl.BlockSpec((1, NSA_GROUPS, m_blocks, dp), lambda b, ph, c, pt: (b, 0, ph, 0)),
        scratch_shapes=[pltpu.VMEM((NSA_GROUPS, n_slab, 2 * npg * _STAGE_PITCH, 128), F32),
                        pltpu.VMEM((NSA_GROUPS, m_blocks, NSA_BLOCK * dp), BF16)])
    return pl.pallas_call(
        functools.partial(_cmp_paged_kernel, dh=dh, keys_on_lanes=keys_on_lanes, norm=norm),
        grid_spec=grid_spec,
        out_shape=jax.ShapeDtypeStruct((db, NSA_GROUPS, n_pages * 2, dp), BF16),
        compiler_params=_cparams(("arbitrary", "arbitrary", "arbitrary")),
        name="cmp_paged_%d" % dh,
    )(page_table, *([cache] * npg), pe_pad, w1_pad.reshape(NSA_BLOCK * dp, NSA_CMP_HIDDEN), w2_pad, g_pad)


def _masked_softmax_cols(s, valid):
    s = jnp.where(valid, s, NEG)
    m = jnp.max(s, axis=0, keepdims=True)
    e = jnp.where(valid, jnp.exp2(s - m), 0.0)
    return e / jnp.maximum(jnp.sum(e, axis=0, keepdims=True), 1e-30)


def _select_blocks(score, cb, n_keep):
    nb, w = score.shape
    n_q = lax.broadcasted_iota(jnp.int32, (nb, w), 0)
    n_f = n_q.astype(F32)
    allowed = n_q <= cb
    forced = allowed & ((n_q == 0) | (n_q == cb) | (n_q == cb - 1))
    val0 = jnp.where(forced, jnp.inf, jnp.where(allowed, score, -jnp.inf))

    def pick(_, val):
        mx = jnp.max(val, axis=0, keepdims=True)
        idx = jnp.min(jnp.where(val == mx, n_f, float(nb)), axis=0, keepdims=True)
        return jnp.where(n_f == idx, -jnp.inf, val)

    val = lax.fori_loop(0, n_keep, pick, val0)
    return jnp.where(allowed & (val == -jnp.inf), 1.0, 0.0)


def _nsa_prompt_kernel(qt_ref, kc_ref, vct_ref, ks_ref, vst_ref, kw_ref, vwt_ref, gt_ref,
                       o_ref, sel_ref, acc_ref, s_ref, st_ref, *, tq, tk, n_keep):
    i = pl.program_id(1)
    r = NSA_HPG * tq
    t0 = i * tq
    qt = jnp.concatenate([qt_ref[h] for h in range(NSA_HPG)], axis=1)
    t_row = t0 + (lax.broadcasted_iota(jnp.int32, (1, r), 1) & (tq - 1))

    nb = kc_ref.shape[1]
    s = jnp.dot(kc_ref[0], qt, preferred_element_type=F32)
    n_io = lax.broadcasted_iota(jnp.int32, (nb, r), 0)
    p = _masked_softmax_cols(s, n_io < ((t_row + 1) >> 6))
    o_cmp = jnp.dot(vct_ref[0], p.astype(BF16), preferred_element_type=F32)
    score = p[:, 0:tq]
    for h in range(1, NSA_HPG):
        score = score + p[:, h * tq:(h + 1) * tq]

    cb = (t0 + lax.broadcasted_iota(jnp.int32, (1, tq), 1)) >> 6
    sel = _select_blocks(score, cb, n_keep)
    sel_ref[...] = jnp.concatenate([sel] * NSA_HPG, axis=1)

    nbk = tk // NSA_BLOCK
    acc_ref[...] = jnp.zeros_like(acc_ref)
    st_ref[...] = jnp.full(st_ref.shape, NEG, F32)

    def tile_scores(j, buf, causal, m_in):
        k = ks_ref[pl.ds(pl.multiple_of(j * tk, tk), tk), :]
        bias = (sel_ref[pl.ds(pl.multiple_of(j * nbk, nbk), nbk), :] - 1.0) * (-NEG)
        bias = jnp.broadcast_to(bias[:, None, :], (nbk, NSA_BLOCK, r)).reshape(tk, r)
        sc = jnp.dot(k, qt, preferred_element_type=F32) + bias
        if causal:
            pos = j * tk + lax.broadcasted_iota(jnp.int32, (tk, r), 0)
            sc = jnp.where(pos <= t_row, sc, NEG)
        s_ref[buf] = sc
        st_ref[buf, 0:1, :] = m_in
        st_ref[buf, 1:2, :] = jnp.maximum(m_in, jnp.max(sc, axis=0, keepdims=True))

    def tile_accumulate(j, buf):
        m_in, m_out = st_ref[buf, 0:1, :], st_ref[buf, 1:2, :]
        alpha = jnp.exp2(m_in - m_out)
        pp = jnp.exp2(s_ref[buf] - m_out)
        pv = jnp.dot(vst_ref[0, j], pp.astype(BF16), preferred_element_type=F32)
        acc_ref[...] = acc_ref[...] * alpha + pv

    m_after = lambda buf: st_ref[buf, 1:2, :]
    n_full = t0 // tk
    neg_row = jnp.full((1, r), NEG, F32)

    @pl.when(n_full == 0)
    def _():
        tile_scores(0, 0, True, neg_row)
        tile_accumulate(0, 0)

    @pl.when(n_full > 0)
    def _():
        tile_scores(0, 0, False, neg_row)

    n_pairs = jnp.maximum(n_full - 1, 0) // 2

    def pair(u, carry):
        j = 2 * u
        tile_accumulate(j, 0)
        tile_scores(j + 1, 1, False, m_after(0))
        tile_accumulate(j + 1, 1)
        tile_scores(j + 2, 0, False, m_after(1))
        return carry

    lax.fori_loop(0, n_pairs, pair, 0)
    done = 2 * n_pairs

    @pl.when((n_full > 0) & (n_full - done == 1))
    def _():
        tile_accumulate(done, 0)
        tile_scores(n_full, 1, True, m_after(0))
        tile_accumulate(n_full, 1)

    @pl.when((n_full > 0) & (n_full - done == 2))
    def _():
        tile_accumulate(done, 0)
        tile_scores(done + 1, 1, False, m_after(0))
        tile_accumulate(done + 1, 1)
        tile_scores(n_full, 0, True, m_after(1))
        tile_accumulate(n_full, 0)

    m = jnp.where(n_full % 2 == 1, m_after(1), m_after(0))
    o_slc = jnp.where(m > 0.5 * NEG, acc_ref[0:NSA_DV, :] / acc_ref[NSA_DV:NSA_DV + 1, :], 0.0)

    n_w = NSA_WINDOW // tq + 1
    k_parts, v_parts = [], []
    for d in range(n_w):
        start = jnp.maximum(t0 - NSA_WINDOW + d * tq, 0)
        k_parts.append(kw_ref[pl.ds(pl.multiple_of(start, tq), tq), :])
        v_parts.append(vwt_ref[0, jnp.maximum(i - (n_w - 1) + d, 0)])
    kcat = jnp.concatenate(k_parts, axis=0)
    sw = jnp.dot(kcat, qt, preferred_element_type=F32)
    spos = t0 - NSA_WINDOW + lax.broadcasted_iota(jnp.int32, (n_w * tq, r), 0)
    dt = t_row - spos
    pw = _masked_softmax_cols(sw, (dt >= 0) & (dt <= NSA_WINDOW) & (spos >= 0))
    o_win = jnp.dot(jnp.concatenate(v_parts, axis=1), pw.astype(BF16), preferred_element_type=F32)

    for h in range(NSA_HPG):
        sl = slice(h * tq, (h + 1) * tq)
        o = (gt_ref[3 * h:3 * h + 1, :] * o_cmp[:, sl] + gt_ref[3 * h + 1:3 * h + 2, :] * o_slc[:, sl]
             + gt_ref[3 * h + 2:3 * h + 3, :] * o_win[:, sl])
        o_ref[:, h * NSA_DV:(h + 1) * NSA_DV] = o.T


def _nsa_prompt(qt, kc, vct, ks, vst, kw, vwt, gt, tq, tk):
    t = ks.shape[0]
    nb = t // NSA_BLOCK
    r = NSA_HPG * tq
    resident = dict(pipeline_mode=pl.Buffered(1))
    return pl.pallas_call(
        functools.partial(_nsa_prompt_kernel, tq=tq, tk=tk, n_keep=min(NSA_N_SEL, nb)),
        grid=(NSA_GROUPS, t // tq),
        in_specs=[pl.BlockSpec((NSA_HPG, DKP, tq), lambda g, i: (g, 0, i)),
                  pl.BlockSpec((1, nb, DKP), lambda g, i: (g, 0, 0)),
                  pl.BlockSpec((1, NSA_DV, nb), lambda g, i: (g, 0, 0)),
                  pl.BlockSpec((t, DKP), lambda g, i: (0, g), **resident),
                  pl.BlockSpec((1, t // tk, NSA_DV + ONES_ROWS, tk), lambda g, i: (g, 0, 0, 0), **resident),
                  pl.BlockSpec((t, DKP), lambda g, i: (0, g), **resident),
                  pl.BlockSpec((1, t // tq, NSA_DV, tq), lambda g, i: (g, 0, 0, 0), **resident),
                  pl.BlockSpec((16, tq), lambda g, i: (g, i))],
        out_specs=pl.BlockSpec((tq, NSA_HPG * NSA_DV), lambda g, i: (i, g)),
        out_shape=jax.ShapeDtypeStruct((t, NSA_WIDTH), F32),
        scratch_shapes=[pltpu.VMEM((nb, r), F32), pltpu.VMEM((NSA_DV + ONES_ROWS, r), F32),
                        pltpu.VMEM((2, tk, r), F32), pltpu.VMEM((2, 8, r), F32)],
        compiler_params=_cparams(("arbitrary", "arbitrary")),
        name="nsa_prompt",
    )(qt, kc, vct, ks, vst, kw, vwt, gt)


def _nsa_sample_kernel(pt_ref, *refs, past, ts, n_keep):
    del pt_ref
    npg = _PAGES_PER_STEP
    kp, vp = refs[:npg], refs[npg:2 * npg]
    (qz_ref, qzp_ref, kc_ref, vc_ref, ksn_ref, vsn_ref, kwb_ref, vwb_ref, kwn_ref, vwn_ref, gate_ref,
     o_ref, sel_ref, acc_ref, ml_ref, ocw_ref) = refs[2 * npg:]
    c = pl.program_id(1)
    ng, nl = NSA_GROUPS, 128
    lg = nl // ng
    lane = lax.broadcasted_iota(jnp.int32, (1, nl), 1)
    tt = lane % ts
    t_row = past + tt
    lane_g = lane // lg

    def scores(k_of_g, q_ref, keys_on_lanes=False):
        dims = (((0,), (0,)), ((), ())) if keys_on_lanes else (((1,), (0,)), ((), ()))
        s = lax.dot_general(k_of_g(0), q_ref[0, 0], dims, preferred_element_type=F32)
        for g in range(1, ng):
            s = s + lax.dot_general(k_of_g(g), q_ref[0, g], dims, preferred_element_type=F32)
        return s

    def group_rows(ref, g, n):
        return ref[0, pl.ds(g, n, stride=ng), :]

    def pv(v_of_g, p):
        o = None
        for g in range(ng):
            pg = jnp.where(lane_g == g, p, 0.0).astype(BF16)
            d = lax.dot_general(v_of_g(g), pg, (((0,), (0,)), ((), ())), preferred_element_type=F32)
            o = d if o is None else o + d
        return o

    def pad_rows(x):
        return jnp.concatenate([x, jnp.zeros((16 - ts, x.shape[1]), x.dtype)], axis=0).astype(BF16)

    def update(sc, mask, v_of_g):
        sc = jnp.where(mask, sc, NEG)
        m, l = ml_ref[0:1, :], ml_ref[1:2, :]
        m_new = jnp.maximum(m, jnp.max(sc, axis=0, keepdims=True))
        alpha = jnp.exp2(m - m_new)
        pp = jnp.exp2(sc - m_new)
        ml_ref[0:1, :] = m_new
        ml_ref[1:2, :] = l * alpha + jnp.sum(pp, axis=0, keepdims=True)
        acc_ref[...] = acc_ref[...] * alpha + pv(v_of_g, pp)

    @pl.when(c == 0)
    def _():
        nb = kc_ref.shape[2]
        s = scores(lambda g: kc_ref[0, g], qzp_ref)
        n_io = lax.broadcasted_iota(jnp.int32, (nb, nl), 0)
        p = _masked_softmax_cols(s, n_io < ((t_row + 1) >> 6))
        ocw_ref[0] = pv(lambda g: vc_ref[0, g], p)
        li = lax.broadcasted_iota(jnp.int32, (nl, nl), 0)
        lj = lax.broadcasted_iota(jnp.int32, (nl, nl), 1)
        same = ((li // lg) == (lj // lg)) & ((li % ts) == (lj % ts))
        mm = same.astype(F32).astype(BF16)
        hi = p.astype(BF16)
        r1 = p - hi.astype(F32)
        mid = r1.astype(BF16)
        lo = (r1 - mid.astype(F32)).astype(BF16)
        score = (jnp.dot(hi, mm, preferred_element_type=F32) + jnp.dot(mid, mm, preferred_element_type=F32)
                 + jnp.dot(lo, mm, preferred_element_type=F32))
        sel_ref[...] = _select_blocks(score, t_row >> 6, n_keep)
        w = kwb_ref.shape[4]
        s_buf = scores(lambda g: kwb_ref[0, 0, g].astype(BF16), qz_ref, True)
        s_new = scores(lambda g: pad_rows(kwn_ref[:, g * DKP:(g + 1) * DKP]), qzp_ref)
        sw = jnp.concatenate([s_buf, s_new], axis=0)
        r_io = lax.broadcasted_iota(jnp.int32, (w + 16, nl), 0)
        spos = past - w + r_io
        dt = t_row - spos
        pw = _masked_softmax_cols(sw, (dt >= 0) & (dt <= NSA_WINDOW) & (spos >= 0) & (r_io < w + ts))
        ocw_ref[1] = (pv(lambda g: group_rows(vwb_ref, g, w).astype(BF16), pw[:w])
                      + pv(lambda g: pad_rows(vwn_ref[:, g * NSA_DV:(g + 1) * NSA_DV]), pw[w:]))
        acc_ref[...] = jnp.zeros_like(acc_ref)
        ml_ref[...] = jnp.zeros_like(ml_ref)
        ml_ref[0:1, :] = jnp.full((1, nl), NEG, F32)

    nbk = 2 * npg
    sc = scores(lambda g: jnp.concatenate([kp[k][0, 0, g] for k in range(npg)], axis=1).astype(BF16),
                qz_ref, True)
    selt = sel_ref[pl.ds(pl.multiple_of(c * nbk, nbk), nbk), :]
    mask = jnp.broadcast_to(selt[:, None, :], (nbk, NSA_BLOCK, nl)).reshape(nbk * NSA_BLOCK, nl) > 0.0
    update(sc, mask,
           lambda g: jnp.concatenate([group_rows(vp[k], g, PAGE_SIZE) for k in range(npg)], axis=0).astype(BF16))

    @pl.when(c == pl.num_programs(1) - 1)
    def _():
        sn = scores(lambda g: pad_rows(ksn_ref[:, g * DKP:(g + 1) * DKP]), qzp_ref)
        u = lax.broadcasted_iota(jnp.int32, (16, nl), 0)
        update(sn, (u <= tt) & (u < ts), lambda g: pad_rows(vsn_ref[:, g * NSA_DV:(g + 1) * NSA_DV]))
        m, l = ml_ref[0:1, :], ml_ref[1:2, :]
        o_slc = jnp.where(m > 0.5 * NEG, acc_ref[...] / l, 0.0)
        o_ref[0] = gate_ref[0, 0:1, :] * ocw_ref[0] + gate_ref[0, 1:2, :] * o_slc + gate_ref[0, 2:3, :] * ocw_ref[1]


def _nsa_sample(page_table, cache_k, cache_v, qz, qzp, kc, vc, ksn, vsn, kwb, vwb, kwn, vwn, gates, ts):
    db, n_pages = page_table.shape
    npg = _PAGES_PER_STEP
    steps = n_pages // npg
    past = n_pages * PAGE_SIZE
    nb = past // NSA_BLOCK
    w = kwb.shape[4]
    assert NSA_HPG * ts * NSA_GROUPS == 128 and ts <= 16

    def kpage_spec(k):
        return pl.BlockSpec((1, 1, NSA_GROUPS, NSA_DK, PAGE_SIZE), lambda b, c, pt: (0, pt[b, c * npg + k], 0, 0, 0))

    def vpage_spec(k):
        return pl.BlockSpec((1, PAGE_SIZE * NSA_GROUPS, NSA_DV), lambda b, c, pt: (pt[b, c * npg + k], 0, 0))

    per_seq = lambda *shape: pl.BlockSpec((1,) + shape, lambda b, c, pt: (b,) + (0,) * len(shape))
    rows = lambda width: pl.BlockSpec((ts, width), lambda b, c, pt: (b, 0))
    kwin = pl.BlockSpec((1, 1, NSA_GROUPS, NSA_DK, w), lambda b, c, pt: (0, b, 0, 0, 0))
    vwin = pl.BlockSpec((1, w * NSA_GROUPS, NSA_DV), lambda b, c, pt: (b, 0, 0))
    grid_spec = pltpu.PrefetchScalarGridSpec(
        num_scalar_prefetch=1,
        grid=(db, steps),
        in_specs=([kpage_spec(k) for k in range(npg)] + [vpage_spec(k) for k in range(npg)] + [
            per_seq(NSA_GROUPS, NSA_DK, 128), per_seq(NSA_GROUPS, DKP, 128),
            per_seq(NSA_GROUPS, nb, DKP), per_seq(NSA_GROUPS, nb, NSA_DV),
            rows(NSA_GROUPS * DKP), rows(NSA_GROUPS * NSA_DV),
            kwin, vwin,
            rows(NSA_GROUPS * DKP), rows(NSA_GROUPS * NSA_DV),
            per_seq(8, 128)]),
        out_specs=per_seq(NSA_DV, 128),
        scratch_shapes=[pltpu.VMEM((nb, 128), F32), pltpu.VMEM((NSA_DV, 128), F32),
                        pltpu.VMEM((8, 128), F32), pltpu.VMEM((2, NSA_DV, 128), F32)])
    return pl.pallas_call(
        functools.partial(_nsa_sample_kernel, past=past, ts=ts, n_keep=min(NSA_N_SEL - 1, nb)),
        grid_spec=grid_spec,
        out_shape=jax.ShapeDtypeStruct((db, NSA_DV, 128), F32),
        compiler_params=_cparams(("arbitrary", "arbitrary")),
        name="nsa_sample",
    )(page_table, *([cache_k] * npg), *([cache_v] * npg), qz, qzp, kc, vc, ksn, vsn, kwb, vwb, kwn, vwn, gates)


def _split3(x):
    hi = x.astype(BF16)
    r1 = x - hi.astype(F32)
    mid = r1.astype(BF16)
    lo = (r1 - mid.astype(F32)).astype(BF16)
    return jnp.concatenate([hi, mid, lo], axis=1)


def _gla_kernel(q_ref, k_ref, v_ref, a_ref, wa_ref, ba_ref, s0_ref, o_ref, sout_ref, st_ref,
                *, c, n_sub, n_valid):
    ci = pl.program_id(2)
    levels = int(math.log2(c))

    @pl.when(ci == 0)
    def _():
        st_ref[...] = s0_ref[0, 0].T

    t_io = lax.broadcasted_iota(jnp.int32, (c, c), 0)
    s_io = lax.broadcasted_iota(jnp.int32, (c, c), 1)
    row = lax.broadcasted_iota(jnp.int32, (c, 1), 0)
    sels = [s_io <= t_io]
    for lv in range(1, levels + 1):
        sels.append(s_io <= ((t_io >> lv) << lv) + (1 << (lv - 1)) - 1)
    sel_all = jnp.concatenate(sels, axis=0).astype(F32).astype(BF16)
    dk = GLA_DK
    for u in range(n_sub):
        rows = slice(u * c, (u + 1) * c)
        q = q_ref[rows, :] * (GLA_DK ** -0.5)
        k = k_ref[rows, :]
        v = v_ref[rows, :].astype(BF16)
        z = jnp.dot(a_ref[rows, :].astype(BF16), wa_ref[...], preferred_element_type=F32) + ba_ref[...]
        la = (jnp.minimum(z, 0.0) - jnp.log1p(jnp.exp(-jnp.abs(z)))) * (1.0 / GLA_TAU)
        if n_valid < c:
            la = jnp.where(row < n_valid, la, 0.0)
        big = jnp.dot(sel_all, _split3(la), preferred_element_type=F32)
        big = big[:, 0:dk] + big[:, dk:2 * dk] + big[:, 2 * dk:3 * dk]
        b = big[0:c]
        st = st_ref[...]
        o = lax.dot_general((q * jnp.exp(b)).astype(BF16), st.astype(BF16),
                            (((1,), (1,)), ((), ())), preferred_element_type=F32)
        a_mat = jnp.where(t_io == s_io, jnp.sum(q * k, axis=1, keepdims=True), 0.0)
        for lv in range(1, levels + 1):
            bref = big[lv * c:(lv + 1) * c]
            upper = ((row >> (lv - 1)) & 1) == 1
            ql = jnp.where(upper, q * jnp.exp(b - bref), 0.0).astype(BF16)
            kl = jnp.where(upper, 0.0, k * jnp.exp(bref - b)).astype(BF16)
            al = lax.dot_general(ql, kl, (((1,), (1,)), ((), ())), preferred_element_type=F32)
            a_mat = a_mat + jnp.where((t_io >> lv) == (s_io >> lv), al, 0.0)
        o = o + jnp.dot(a_mat.astype(BF16), v, preferred_element_type=F32)
        o_ref[rows, :] = o
        b_last = b[c - 1:c, :]
        kd = (k * jnp.exp(b_last - b)).astype(BF16)
        st_ref[...] = st * jnp.exp(b_last) + lax.dot_general(
            v, kd, (((0,), (0,)), ((), ())), preferred_element_type=F32)

    @pl.when(ci == pl.num_programs(2) - 1)
    def _():
        sout_ref[0, 0] = st_ref[...].T


def _gla(p, n_batch, t_len, s0, wa_pad, ba, c, n_sub, n_valid):
    step = c * n_sub
    ncs = t_len // step
    qb, kb = _SEG["q_g"] // GLA_DK, _SEG["k_g"] // GLA_DK
    vb, ab = _SEG["v_g"] // GLA_DV, _SEG["a_g"] // 256
    return pl.pallas_call(
        functools.partial(_gla_kernel, c=c, n_sub=n_sub, n_valid=n_valid),
        grid=(n_batch, GLA_HEADS, ncs),
        in_specs=[pl.BlockSpec((step, GLA_DK), lambda b, h, ci: (b * ncs + ci, qb + h)),
                  pl.BlockSpec((step, GLA_DK), lambda b, h, ci: (b * ncs + ci, kb + h)),
                  pl.BlockSpec((step, GLA_DV), lambda b, h, ci: (b * ncs + ci, vb + h)),
                  pl.BlockSpec((step, 256), lambda b, h, ci: (b * ncs + ci, ab)),
                  pl.BlockSpec((256, GLA_DK), lambda b, h, ci: (0, h)),
                  pl.BlockSpec((1, GLA_DK), lambda b, h, ci: (0, h)),
                  pl.BlockSpec((1, 1, GLA_DK, GLA_DV), lambda b, h, ci: (b, h, 0, 0))],
        out_specs=[pl.BlockSpec((step, GLA_DV), lambda b, h, ci: (b * ncs + ci, h)),
                   pl.BlockSpec((1, 1, GLA_DK, GLA_DV), lambda b, h, ci: (b, h, 0, 0))],
        out_shape=[jax.ShapeDtypeStruct((n_batch * t_len, GLA_WIDTH), F32),
                   jax.ShapeDtypeStruct((n_batch, GLA_HEADS, GLA_DK, GLA_DV), F32)],
        scratch_shapes=[pltpu.VMEM((GLA_DV, GLA_DK), F32)],
        compiler_params=_cparams(("arbitrary", "arbitrary", "arbitrary")),
        name="gla",
    )(p, p, p, p, wa_pad, ba, s0)


def _merge1_kernel(on_ref, zn_ref, og_ref, zg_ref, gg_ref, wa_ref, wb_ref, ma_ref, mb_ref, o_ref,
                   a_ref, b_ref):
    @pl.when(pl.program_id(1) == 0)
    def _():
        a_ref[...] = (on_ref[...] * _silu(zn_ref[...])).astype(BF16)
        og = og_ref[...]
        zg = zg_ref[...]
        for h in range(GLA_HEADS):
            sl = slice(h * GLA_DV, (h + 1) * GLA_DV)
            x = og[:, sl]
            y = x * lax.rsqrt(jnp.mean(x * x, axis=-1, keepdims=True) + NORM_EPS) * gg_ref[...]
            b_ref[:, sl] = (y * _silu(zg[:, sl])).astype(BF16)

    ua = jnp.dot(a_ref[...], wa_ref[...], preferred_element_type=F32)
    ub = jnp.dot(b_ref[...], wb_ref[...], preferred_element_type=F32)
    o_ref[...] = (jax.nn.sigmoid(ma_ref[...]) * ua + jax.nn.sigmoid(mb_ref[...]) * ub).astype(BF16)


def _merge1(o_nsa, o_gla, p, gla_g, wb_a, wb_b, tm):
    m = p.shape[0]
    tn = 512
    zn, zg, mg = _SEG["z_nsa"] // NSA_WIDTH, _SEG["z_g"] // GLA_WIDTH, _SEG["m_gate"] // tn
    row = lambda i, j: (i, 0)
    return pl.pallas_call(
        _merge1_kernel,
        grid=(m // tm, D_MODEL // tn),
        in_specs=[pl.BlockSpec((tm, NSA_WIDTH), row),
                  pl.BlockSpec((tm, NSA_WIDTH), lambda i, j: (i, zn)),
                  pl.BlockSpec((tm, GLA_WIDTH), row),
                  pl.BlockSpec((tm, GLA_WIDTH), lambda i, j: (i, zg)),
                  pl.BlockSpec((1, GLA_DV), lambda i, j: (0, 0)),
                  pl.BlockSpec((NSA_WIDTH, tn), lambda i, j: (0, j)),
                  pl.BlockSpec((GLA_WIDTH, tn), lambda i, j: (0, j)),
                  pl.BlockSpec((tm, tn), lambda i, j: (i, mg + j)),
                  pl.BlockSpec((tm, tn), lambda i, j: (i, mg + D_MODEL // tn + j))],
        out_specs=pl.BlockSpec((tm, tn), lambda i, j: (i, j)),
        out_shape=jax.ShapeDtypeStruct((m, D_MODEL), BF16),
        scratch_shapes=[pltpu.VMEM((tm, NSA_WIDTH), BF16), pltpu.VMEM((tm, GLA_WIDTH), BF16)],
        compiler_params=_cparams(("arbitrary", "arbitrary")),
        name="merge1",
    )(o_nsa, p, o_gla, p, gla_g, wb_a, wb_b, p, p)


def _merge2_kernel(x_ref, gate_ref, mg_ref, w_ref, o_ref):
    o_ref[...] = x_ref[...] + gate_ref[...] * jnp.dot(mg_ref[...], w_ref[...], preferred_element_type=F32)


def _merge2(x, gate, merged, w_out, tm):
    m, d = x.shape
    tn = 512
    per_row = gate.shape[0] != 1
    gate_spec = (pl.BlockSpec((tm, tn), lambda i, j: (i, j)) if per_row
                 else pl.BlockSpec((1, tn), lambda i, j: (0, j)))
    return pl.pallas_call(
        _merge2_kernel,
        grid=(m // tm, d // tn),
        in_specs=[pl.BlockSpec((tm, tn), lambda i, j: (i, j)), gate_spec,
                  pl.BlockSpec((tm, d), lambda i, j: (i, 0)),
                  pl.BlockSpec((d, tn), lambda i, j: (0, j))],
        out_specs=pl.BlockSpec((tm, tn), lambda i, j: (i, j)),
        out_shape=jax.ShapeDtypeStruct((m, d), F32),
        compiler_params=_cparams(("arbitrary", "arbitrary")),
        name="merge2",
    )(x, gate, merged, w_out)


def _pad_last(a, width):
    return jnp.pad(a, [(0, 0)] * (a.ndim - 1) + [(0, width - a.shape[-1])])


def _pad_heads(w, n, src, dst):
    return _pad_last(w.reshape(w.shape[0], n, src), dst).reshape(w.shape[0], n * dst)


def _pad_w_in(w_in):
    sizes = (NSA_HEADS * NSA_DK, NSA_GROUPS * NSA_DK, NSA_GROUPS * NSA_DV, NSA_GROUPS * NSA_DK,
             NSA_GROUPS * NSA_DV, NSA_GROUPS * NSA_DK, NSA_GROUPS * NSA_DV, 3 * NSA_HEADS, NSA_WIDTH,
             GLA_HEADS * GLA_DK, GLA_HEADS * GLA_DK, GLA_WIDTH, GLA_WIDTH, GLA_RANK, 2 * D_MODEL)
    cuts = [0]
    for s in sizes:
        cuts.append(cuts[-1] + s)
    names = ("q", "k_cmp", "v_cmp", "k_slc", "v_slc", "k_win", "v_win", "g_nsa", "z_nsa",
             "q_g", "k_g", "v_g", "z_g", "a_g", "m_gate")
    parts = {nm: w_in[:, cuts[n]:cuts[n + 1]] for n, nm in enumerate(names)}
    parts["q"] = _pad_heads(parts["q"], NSA_HEADS, NSA_DK, DKP)
    for nm in ("k_cmp", "k_slc", "k_win"):
        parts[nm] = _pad_heads(parts[nm], NSA_GROUPS, NSA_DK, DKP)
    parts["g_nsa"] = _pad_last(_pad_heads(parts["g_nsa"], NSA_GROUPS, 3 * NSA_HPG, 16), 256)
    parts["a_g"] = _pad_last(parts["a_g"], 256)
    return jnp.concatenate([parts[nm] for nm in _SEG], axis=1).astype(BF16)


def _unpad_heads(a, n):
    return a.reshape(a.shape[0], n, DKP)[:, :, :NSA_DK]


def kernel(x_prompt, x_sample, cache_k_cmp, cache_v_cmp, cache_k_slc, cache_v_slc, state_k_win, state_v_win, state_gla, page_table, c_prompt, c_sample, ln_g, w_ada, b_ada, w_in, q_norm_g, k_cmp_norm_g, k_slc_norm_g, k_win_norm_g, pe_k, w1_k, w2_k, pe_v, w1_v, w2_v, w_a2, b_a, gla_norm_g, w_branch, w_out):
    assert ln_g.shape[0] == 1, "single layer"
    bp, t, d = x_prompt.shape
    db, ts, _ = x_sample.shape
    assert bp == 1 and d == D_MODEL
    G = NSA_GROUPS

    w_pad = _pad_w_in(w_in[0])
    g_q = _pad_last(q_norm_g, DKP)
    g_kc = _pad_last(k_cmp_norm_g, DKP)
    g_ks = _pad_last(k_slc_norm_g, DKP)
    g_kw = _pad_last(k_win_norm_g, DKP)
    pe_k_p = _pad_last(pe_k[0], DKP)[:, None, :]
    w1_k_p = jnp.pad(w1_k[0], ((0, 0), (0, DKP - NSA_DK), (0, 0))).astype(BF16)
    w2_k_p = _pad_last(w2_k[0], DKP).astype(BF16)
    pe_v_p = pe_v[0][:, None, :]
    w1_v_p = w1_v[0].astype(BF16)
    w2_v_p = w2_v[0].astype(BF16)
    wa_pad = jnp.pad(w_a2[0], ((0, 256 - GLA_RANK), (0, 0))).astype(BF16)
    wb_a = w_branch[0, :NSA_WIDTH].astype(BF16)
    wb_b = w_branch[0, NSA_WIDTH:].astype(BF16)
    w_o = w_out[0].astype(BF16)
    ones_v = jnp.ones((1, NSA_DV), F32)

    n_c = bp + db
    c_all = jnp.pad(jnp.concatenate([c_prompt, c_sample], axis=0), ((0, (-n_c) % 8), (0, 0)))
    mod = _ada(c_all, w_ada[0], b_ada)
    shift, scale, gate = mod[:, :d], mod[:, d:2 * d], mod[:, 2 * d:]

    tm = min(1024, t)
    xp = x_prompt.reshape(t, d)
    pp = _proj(xp, scale[0:1], shift[0:1], ln_g, w_pad, tm)
    tq = min(256, t)
    tk = min(512, t)
    qt = _prep_q(pp, g_q, min(512, t))
    ks_f, ks_b = _prep_k(pp, "k_slc", g_ks, min(512, t))
    kw_f, kw_b = _prep_k(pp, "k_win", g_kw, min(512, t))
    vst = _prep_vt(pp, "v_slc", tk, ONES_ROWS)
    vwt = _prep_vt(pp, "v_win", tq)
    gt = _prep_gate(pp, min(512, t))
    kc = _compress_prompt(pp, "k_cmp", DKP, pe_k_p, w1_k_p, w2_k_p, g_kc, True, False)
    vct = _compress_prompt(pp, "v_cmp", NSA_DV, pe_v_p, w1_v_p, w2_v_p, ones_v, False, True)
    o_nsa_p = _nsa_prompt(qt, kc, vct, ks_b, vst, kw_b, vwt, gt, tq, tk)
    s0_p = jnp.zeros((bp, GLA_HEADS, GLA_DK, GLA_DV), F32)
    o_gla_p, gla_p = _gla(pp, bp, t, s0_p, wa_pad, b_a, GLA_CHUNK, 4, GLA_CHUNK)
    merged_p = _merge1(o_nsa_p, o_gla_p, pp, gla_norm_g, wb_a, wb_b, min(512, t))
    y_p = _merge2(xp, gate[0:1], merged_p, w_o, min(512, t)).reshape(bp, t, d)

    wp = min(NSA_WINDOW, t)
    k_cmp_p = _unpad_heads(pp[:, _SEG["k_cmp"]:_SEG["k_cmp"] + G * DKP], G)
    v_cmp_p = pp[:, _SEG["v_cmp"]:_SEG["v_cmp"] + G * NSA_DV].reshape(t, G, NSA_DV)
    k_slc_p = _unpad_heads(ks_f, G)
    v_slc_p = pp[:, _SEG["v_slc"]:_SEG["v_slc"] + G * NSA_DV].reshape(t, G, NSA_DV)
    k_win_p = _unpad_heads(kw_f, G)[t - wp:]
    v_win_p = pp[t - wp:, _SEG["v_win"]:_SEG["v_win"] + G * NSA_DV].reshape(wp, G, NSA_DV)

    ms = db * ts
    xs = x_sample.reshape(ms, d)
    rep = lambda a: jnp.repeat(a[bp:bp + db], ts, axis=0)
    ps = _proj(xs, rep(scale), rep(shift), ln_g, w_pad, ms)
    seg = lambda name, w: ps[:, _SEG[name]:_SEG[name] + w]
    qt_s = _prep_q(ps, g_q, ms)
    kss_f, _ = _prep_k(ps, "k_slc", g_ks, ms)
    kws_f, _ = _prep_k(ps, "k_win", g_kw, ms)
    gt_s = _prep_gate(ps, ms)
    v_slc_new = seg("v_slc", G * NSA_DV)
    v_win_new = seg("v_win", G * NSA_DV)
    k_cmp_s = _unpad_heads(seg("k_cmp", G * DKP), G).reshape(db, ts, G, NSA_DK)
    v_cmp_s = seg("v_cmp", G * NSA_DV).reshape(db, ts, G, NSA_DV)
    k_slc_s = _unpad_heads(kss_f, G).reshape(db, ts, G, NSA_DK)
    v_slc_s = v_slc_new.reshape(db, ts, G, NSA_DV)
    k_win_s_new = _unpad_heads(kws_f, G).reshape(db, ts, G, NSA_DK)
    v_win_s_new = v_win_new.reshape(db, ts, G, NSA_DV)
    q4 = qt_s.reshape(G, NSA_HPG, DKP, db, ts).transpose(3, 0, 2, 1, 4).reshape(db, G, DKP, NSA_HPG * ts)
    own = (jnp.arange(G)[:, None] == jnp.arange(G)[None, :])[None, :, None, :, None]
    qzp = jnp.where(own, q4[:, :, :, None, :], jnp.zeros((), BF16)).reshape(db, G, DKP, G * NSA_HPG * ts)
    qz = qzp[:, :, :NSA_DK, :]
    gates = gt_s.reshape(G, 16, db, ts)[:, :3 * NSA_HPG].reshape(G, NSA_HPG, 3, db, ts)
    gates = gates.transpose(3, 2, 0, 1, 4).reshape(db, 3, G * NSA_HPG * ts)
    gates = jnp.pad(gates, ((0, 0), (0, 5), (0, 0)))
    keys_t = lambda a: jnp.transpose(a, (0, 1, 3, 4, 2))
    vals_2d = lambda a: a.reshape(a.shape[1], a.shape[2] * G, NSA_DV)
    kc_s = _compress_paged(keys_t(cache_k_cmp), page_table, pe_k_p[:, 0, :], w1_k_p, w2_k_p, g_kc, True)
    vc_s = _compress_paged(vals_2d(cache_v_cmp), page_table, pe_v[0], w1_v_p, w2_v_p, ones_v, False)
    o_t = _nsa_sample(page_table, keys_t(cache_k_slc), vals_2d(cache_v_slc), qz, qzp, kc_s, vc_s, kss_f,
                      v_slc_new, keys_t(state_k_win), vals_2d(state_v_win), kws_f, v_win_new, gates, ts)
    o_nsa_s = o_t.reshape(db, NSA_DV, G, NSA_HPG, ts).transpose(0, 4, 2, 3, 1)
    kw_s = jnp.concatenate([state_k_win[0][:, ts:], k_win_s_new], axis=1)
    vw_s = jnp.concatenate([state_v_win[0][:, ts:], v_win_s_new], axis=1)
    c_s = 16
    ps_pad = jnp.pad(ps.reshape(db, ts, D_PAD), ((0, 0), (0, c_s - ts), (0, 0))).reshape(db * c_s, D_PAD)
    o_gla_s, gla_s = _gla(ps_pad, db, c_s, state_gla[0], wa_pad, b_a, c_s, 1, ts)
    o_gla_s = o_gla_s.reshape(db, c_s, GLA_WIDTH)[:, :ts].reshape(ms, GLA_WIDTH)
    merged_s = _merge1(o_nsa_s.reshape(ms, NSA_WIDTH), o_gla_s, ps, gla_norm_g, wb_a, wb_b, ms)
    y_s = _merge2(xs, rep(gate), merged_s, w_o, ms).reshape(db, ts, d)

    L1 = lambda a: a[None, None]
    return (y_p, y_s,
            L1(k_cmp_p), L1(v_cmp_p), L1(k_slc_p), L1(v_slc_p), L1(k_win_p), L1(v_win_p), gla_p[None],
            k_cmp_s[None], v_cmp_s[None], k_slc_s[None], v_slc_s[None], kw_s[None], vw_s[None], gla_s[None])
```

```python
import functools
import math

import jax
import jax.numpy as jnp
from jax import lax
from jax.experimental import pallas as pl
from jax.experimental.pallas import tpu as pltpu

F32 = jnp.float32
BF16 = jnp.bfloat16

NSA_HEADS = 16
NSA_GROUPS = 4
NSA_HPG = NSA_HEADS // NSA_GROUPS
NSA_DK = 192
NSA_DV = 128
NSA_BLOCK = 64
NSA_N_SEL = 16
NSA_WINDOW = 512
NSA_CMP_HIDDEN = 256
PAGE_SIZE = 128
GLA_HEADS = 4
GLA_DK = 256
GLA_DV = 512
GLA_RANK = 16
GLA_TAU = 16.0
GLA_CHUNK = 64
NORM_EPS = 1e-6
D_MODEL = 2048
NSA_WIDTH = NSA_HEADS * NSA_DV
GLA_WIDTH = GLA_HEADS * GLA_DV

DKP = 256
LOG2E = 1.4426950408889634
NEG = -1e30
Q_SCALE = (NSA_DK ** -0.5) * LOG2E
VMEM_LIMIT = 56 * 1024 * 1024

_SEG = {}
_off = 0
for _name, _w in (("z_nsa", NSA_WIDTH), ("z_g", GLA_WIDTH), ("v_g", GLA_WIDTH), ("m_gate", 2 * D_MODEL),
                  ("q", NSA_HEADS * DKP), ("k_cmp", NSA_GROUPS * DKP), ("v_cmp", NSA_GROUPS * NSA_DV),
                  ("k_slc", NSA_GROUPS * DKP), ("v_slc", NSA_GROUPS * NSA_DV),
                  ("k_win", NSA_GROUPS * DKP), ("v_win", NSA_GROUPS * NSA_DV),
                  ("q_g", GLA_HEADS * GLA_DK), ("k_g", GLA_HEADS * GLA_DK),
                  ("g_nsa", 256), ("a_g", 256)):
    _SEG[_name] = _off
    _off += _w
D_PAD = _off


def _cparams(sem):
    return pltpu.CompilerParams(dimension_semantics=sem, vmem_limit_bytes=VMEM_LIMIT)


def _silu(x):
    return x * jax.nn.sigmoid(x)


def _ada_kernel(c_ref, w_ref, b_ref, o_ref):
    a = _silu(c_ref[...]).astype(BF16)
    o_ref[...] = jnp.dot(a, w_ref[...].astype(BF16), preferred_element_type=F32) + b_ref[...]


def _ada(c_all, w_ada, b_ada):
    m, d = c_all.shape
    n = w_ada.shape[1]
    tn = 768
    return pl.pallas_call(
        _ada_kernel,
        grid=(n // tn,),
        in_specs=[pl.BlockSpec((m, d), lambda j: (0, 0)),
                  pl.BlockSpec((d, tn), lambda j: (0, j)),
                  pl.BlockSpec((1, tn), lambda j: (0, j))],
        out_specs=pl.BlockSpec((m, tn), lambda j: (0, j)),
        out_shape=jax.ShapeDtypeStruct((m, n), F32),
        compiler_params=_cparams(("arbitrary",)),
        name="ada",
    )(c_all, w_ada, b_ada)


def _proj_kernel(x_ref, sc_ref, sh_ref, g_ref, w_ref, o_ref, h_ref):
    @pl.when(pl.program_id(1) == 0)
    def _():
        x = x_ref[...]
        y = x * lax.rsqrt(jnp.mean(x * x, axis=-1, keepdims=True) + NORM_EPS) * g_ref[...]
        h_ref[...] = (y * (1.0 + sc_ref[...]) + sh_ref[...]).astype(BF16)

    o_ref[...] = jnp.dot(h_ref[...], w_ref[...], preferred_element_type=F32)


def _proj(x, scale, shift, ln_g, w_pad, tm):
    m, d = x.shape
    n = w_pad.shape[1]
    tn = 1536
    per_row = scale.shape[0] != 1
    mod_spec = (pl.BlockSpec((tm, d), lambda i, j: (i, 0)) if per_row
                else pl.BlockSpec((1, d), lambda i, j: (0, 0)))
    return pl.pallas_call(
        _proj_kernel,
        grid=(m // tm, n // tn),
        in_specs=[pl.BlockSpec((tm, d), lambda i, j: (i, 0)), mod_spec, mod_spec,
                  pl.BlockSpec((1, d), lambda i, j: (0, 0)),
                  pl.BlockSpec((d, tn), lambda i, j: (0, j))],
        out_specs=pl.BlockSpec((tm, tn), lambda i, j: (i, j)),
        out_shape=jax.ShapeDtypeStruct((m, n), F32),
        scratch_shapes=[pltpu.VMEM((tm, d), BF16)],
        compiler_params=_cparams(("arbitrary", "arbitrary")),
        name="proj",
    )(x, scale, shift, ln_g, w_pad)


def _head_norm(x, g):
    ms = jnp.sum(x * x, axis=-1, keepdims=True) * (1.0 / NSA_DK)
    return x * lax.rsqrt(ms + NORM_EPS) * g


def _prep_q_kernel(x_ref, g_ref, o_ref):
    o_ref[0] = (_head_norm(x_ref[...], g_ref[...]) * Q_SCALE).T.astype(BF16)


def _prep_q(p, g_pad, tm):
    t = p.shape[0]
    base = _SEG["q"] // DKP
    return pl.pallas_call(
        _prep_q_kernel,
        grid=(t // tm, NSA_HEADS),
        in_specs=[pl.BlockSpec((tm, DKP), lambda i, h: (i, base + h)),
                  pl.BlockSpec((1, DKP), lambda i, h: (0, 0))],
        out_specs=pl.BlockSpec((1, DKP, tm), lambda i, h: (h, 0, i)),
        out_shape=jax.ShapeDtypeStruct((NSA_HEADS, DKP, t), BF16),
        compiler_params=_cparams(("arbitrary", "arbitrary")),
        name="prep_q",
    )(p, g_pad)


def _prep_k_kernel(x_ref, g_ref, o_ref, ob_ref):
    y = _head_norm(x_ref[...], g_ref[...])
    o_ref[...] = y
    ob_ref[...] = y.astype(BF16)


def _prep_k(p, seg, g_pad, tm):
    t = p.shape[0]
    base = _SEG[seg] // DKP
    n = NSA_GROUPS * DKP
    return pl.pallas_call(
        _prep_k_kernel,
        grid=(t // tm, NSA_GROUPS),
        in_specs=[pl.BlockSpec((tm, DKP), lambda i, g: (i, base + g)),
                  pl.BlockSpec((1, DKP), lambda i, g: (0, 0))],
        out_specs=[pl.BlockSpec((tm, DKP), lambda i, g: (i, g)),
                   pl.BlockSpec((tm, DKP), lambda i, g: (i, g))],
        out_shape=[jax.ShapeDtypeStruct((t, n), F32), jax.ShapeDtypeStruct((t, n), BF16)],
        compiler_params=_cparams(("arbitrary", "arbitrary")),
        name="prep_k_" + seg,
    )(p, g_pad)


ONES_ROWS = 16


def _prep_vt_kernel(x_ref, o_ref, *, ones_rows):
    xt = x_ref[...].T
    if ones_rows:
        xt = jnp.concatenate([xt, jnp.ones((ones_rows, xt.shape[1]), F32)], axis=0)
    o_ref[0, 0] = xt.astype(BF16)


def _prep_vt(p, seg, tile, ones_rows=0):
    t = p.shape[0]
    base = _SEG[seg] // NSA_DV
    rows = NSA_DV + ones_rows
    return pl.pallas_call(
        functools.partial(_prep_vt_kernel, ones_rows=ones_rows),
        grid=(t // tile, NSA_GROUPS),
        in_specs=[pl.BlockSpec((tile, NSA_DV), lambda i, g: (i, base + g))],
        out_specs=pl.BlockSpec((1, 1, rows, tile), lambda i, g: (g, i, 0, 0)),
        out_shape=jax.ShapeDtypeStruct((NSA_GROUPS, t // tile, rows, tile), BF16),
        compiler_params=_cparams(("arbitrary", "arbitrary")),
        name="prep_vt_" + seg,
    )(p)


def _prep_gate_kernel(x_ref, o_ref):
    o_ref[...] = jax.nn.sigmoid(x_ref[...]).T[:4 * NSA_HEADS, :]


def _prep_gate(p, tm):
    t = p.shape[0]
    base = _SEG["g_nsa"] // 256
    return pl.pallas_call(
        _prep_gate_kernel,
        grid=(t // tm,),
        in_specs=[pl.BlockSpec((tm, 256), lambda i: (i, base))],
        out_specs=pl.BlockSpec((4 * NSA_HEADS, tm), lambda i: (0, i)),
        out_shape=jax.ShapeDtypeStruct((4 * NSA_HEADS, t), F32),
        compiler_params=_cparams(("arbitrary",)),
        name="prep_gate",
    )(p)


_CMP_ROWS = 8


def _cmp_kernel(x_ref, pe_ref, w1_ref, w2_ref, g_ref, o_ref, acc_ref, *, norm, transpose_out):
    s = pl.program_id(1)

    @pl.when(s == 0)
    def _():
        acc_ref[...] = jnp.zeros_like(acc_ref)

    acc = acc_ref[...]
    for r in range(_CMP_ROWS):
        xb = (x_ref[:, r, :] + pe_ref[r]).astype(BF16)
        acc = acc + jnp.dot(xb, w1_ref[r], preferred_element_type=F32)
    acc_ref[...] = acc

    @pl.when(s == pl.num_programs(1) - 1)
    def _():
        hid = _silu(acc_ref[...]).astype(BF16)
        y = jnp.dot(hid, w2_ref[...], preferred_element_type=F32)
        if norm:
            y = _head_norm(y, g_ref[...])
        if transpose_out:
            y = y.T
        o_ref[0] = y.astype(o_ref.dtype)


def _compress_prompt(p, seg, dp, pe_pad, w1_pad, w2_pad, g_pad, norm, transpose_out):
    t = p.shape[0]
    nb = t // NSA_BLOCK
    pv = p.reshape(nb, NSA_BLOCK, D_PAD)
    base = _SEG[seg] // dp
    out_shape = (NSA_GROUPS, dp, nb) if transpose_out else (NSA_GROUPS, nb, dp)
    return pl.pallas_call(
        functools.partial(_cmp_kernel, norm=norm, transpose_out=transpose_out),
        grid=(NSA_GROUPS, NSA_BLOCK // _CMP_ROWS),
        in_specs=[pl.BlockSpec((nb, _CMP_ROWS, dp), lambda g, s: (0, s, base + g)),
                  pl.BlockSpec((_CMP_ROWS, 1, dp), lambda g, s: (s, 0, 0)),
                  pl.BlockSpec((_CMP_ROWS, dp, NSA_CMP_HIDDEN), lambda g, s: (s, 0, 0)),
                  pl.BlockSpec((NSA_CMP_HIDDEN, dp), lambda g, s: (0, 0)),
                  pl.BlockSpec((1, dp), lambda g, s: (0, 0))],
        out_specs=pl.BlockSpec((1,) + out_shape[1:], lambda g, s: (g, 0, 0)),
        out_shape=jax.ShapeDtypeStruct(out_shape, BF16),
        scratch_shapes=[pltpu.VMEM((nb, NSA_CMP_HIDDEN), F32)],
        compiler_params=_cparams(("arbitrary", "arbitrary")),
        name="cmp_" + seg,
    )(pv, pe_pad, w1_pad, w2_pad, g_pad)


_PAGES_PER_STEP = 8
_STAGE_PITCH = 72


def _cmp_paged_kernel(pt_ref, *refs, dh, keys_on_lanes, norm):
    del pt_ref
    npg = _PAGES_PER_STEP
    pages = refs[:npg]
    pe_ref, w1_ref, w2_ref, g_ref, o_ref, stage_ref, xs_ref = refs[npg:]
    c = pl.program_id(2)
    nbc = 2 * npg
    n_slab = stage_ref.shape[1]

    @pl.when((pl.program_id(0) == 0) & (pl.program_id(1) == 0) & (c == 0))
    def _():
        stage_ref[...] = jnp.zeros_like(stage_ref)

    for k in range(npg):
        for g in range(NSA_GROUPS):
            if keys_on_lanes:
                x = pages[k][0, 0, g].T
            else:
                x = pages[k][0, pl.ds(g, PAGE_SIZE, stride=NSA_GROUPS), :]
            for blk in range(2):
                r0 = (2 * k + blk) * _STAGE_PITCH
                xb = x[blk * NSA_BLOCK:(blk + 1) * NSA_BLOCK]
                stage_ref[g, 0, r0:r0 + NSA_BLOCK, :] = xb[:, 0:128]
                if n_slab == 2:
                    stage_ref[g, 1, r0:r0 + NSA_BLOCK, 0:dh - 128] = xb[:, 128:dh]
    dp = pe_ref.shape[1]
    row0 = pl.multiple_of(c * nbc, nbc)
    for s in range(NSA_BLOCK):
        for g in range(NSA_GROUPS):
            parts = [stage_ref[g, j, pl.ds(s, nbc, stride=_STAGE_PITCH), :] for j in range(n_slab)]
            x = parts[0] if n_slab == 1 else jnp.concatenate(parts, axis=1)
            xs_ref[g, pl.ds(row0, nbc), s * dp:(s + 1) * dp] = (x + pe_ref[s:s + 1, :]).astype(BF16)

    @pl.when(c == pl.num_programs(2) - 1)
    def _():
        for g in range(NSA_GROUPS):
            acc = jnp.dot(xs_ref[g], w1_ref[...], preferred_element_type=F32)
            hid = _silu(acc).astype(BF16)
            y = jnp.dot(hid, w2_ref[...], preferred_element_type=F32)
            if norm:
                y = _head_norm(y, g_ref[...])
            o_ref[0, g] = y.astype(BF16)


def _compress_paged(cache, page_table, pe_pad, w1_pad, w2_pad, g_pad, norm):
    keys_on_lanes = cache.ndim == 5
    dh = cache.shape[3] if keys_on_lanes else cache.shape[2]
    dp = pe_pad.shape[-1]
    db, n_pages = page_table.shape
    npg = _PAGES_PER_STEP
    steps = n_pages // npg
    spp = min(8, steps)
    m_blocks = spp * 2 * npg
    n_ph = steps // spp
    n_slab = dp // 128

    def page_spec(k):
        page = lambda b, ph, c, pt: pt[b, (ph * spp + c) * npg + k]
        if keys_on_lanes:
            return pl.BlockSpec((1, 1, NSA_GROUPS, dh, PAGE_SIZE), lambda b, ph, c, pt: (0, page(b, ph, c, pt), 0, 0, 0))
        return pl.BlockSpec((1, PAGE_SIZE * NSA_GROUPS, dh), lambda b, ph, c, pt: (page(b, ph, c, pt), 0, 0))

    const = lambda *shape: pl.BlockSpec(shape, lambda b, ph, c, pt: (0,) * len(shape))
    grid_spec = pltpu.PrefetchScalarGridSpec(
        num_scalar_prefetch=1,
        grid=(db, n_ph, spp),
        in_specs=[page_spec(k) for k in range(npg)] + [
            const(NSA_BLOCK, dp),
            pl.BlockSpec((NSA_BLOCK * dp, NSA_CMP_HIDDEN), lambda b, ph, c, pt: (0, 0),
                         pipeline_mode=pl.Buffered(1)),
            const(NSA_CMP_HIDDEN, dp), const(1, dp)],
        out_specs=pl.BlockSpec((1, NSA_GROUPS, m_blocks, dp), lambda b, ph, c, pt: (b, 0, ph, 0)),
        scratch_shapes=[pltpu.VMEM((NSA_GROUPS, n_slab, 2 * npg * _STAGE_PITCH, 128), F32),
                        pltpu.VMEM((NSA_GROUPS, m_blocks, NSA_BLOCK * dp), BF16)])
    return pl.pallas_call(
        functools.partial(_cmp_paged_kernel, dh=dh, keys_on_lanes=keys_on_lanes, norm=norm),
        grid_spec=grid_spec,
        out_shape=jax.ShapeDtypeStruct((db, NSA_GROUPS, n_pages * 2, dp), BF16),
        compiler_params=_cparams(("arbitrary", "arbitrary", "arbitrary")),
        name="cmp_paged_%d" % dh,
    )(page_table, *([cache] * npg), pe_pad, w1_pad.reshape(NSA_BLOCK * dp, NSA_CMP_HIDDEN), w2_pad, g_pad)


def _masked_softmax_cols(s, valid):
    s = jnp.where(valid, s, NEG)
    m = jnp.max(s, axis=0, keepdims=True)
    e = jnp.where(valid, jnp.exp2(s - m), 0.0)
    return e / jnp.maximum(jnp.sum(e, axis=0, keepdims=True), 1e-30)


_LANE_CHUNK = 256


def _select_blocks(score, cb, n_keep):
    nb, w = score.shape
    n_q = lax.broadcasted_iota(jnp.int32, (nb, w), 0)
    n_f = n_q.astype(F32)
    allowed = n_q <= cb
    forced = allowed & ((n_q == 0) | (n_q == cb) | (n_q == cb - 1))
    val0 = jnp.where(forced, jnp.inf, jnp.where(allowed, score, -jnp.inf))

    def pick(_, val):
        mx = jnp.max(val, axis=0, keepdims=True)
        idx = jnp.min(jnp.where(val == mx, n_f, float(nb)), axis=0, keepdims=True)
        return jnp.where(n_f == idx, -jnp.inf, val)

    val = lax.fori_loop(0, n_keep, pick, val0)
    return jnp.where(allowed & (val == -jnp.inf), 1.0, 0.0)


def _nsa_prompt_kernel(qt_ref, kc_ref, vct_ref, ks_ref, vst_ref, kw_ref, vwt_ref, gt_ref,
                       o_ref, sel_ref, acc_ref, s_ref, st_ref, *, tq, tk, n_keep):
    i = pl.program_id(1)
    r = NSA_HPG * tq
    t0 = i * tq
    qt = jnp.concatenate([qt_ref[h] for h in range(NSA_HPG)], axis=1)
    t_row = t0 + (lax.broadcasted_iota(jnp.int32, (1, r), 1) & (tq - 1))

    nb = kc_ref.shape[1]
    s = jnp.dot(kc_ref[0], qt, preferred_element_type=F32)
    n_io = lax.broadcasted_iota(jnp.int32, (nb, r), 0)
    p = _masked_softmax_cols(s, n_io < ((t_row + 1) >> 6))
    o_cmp = jnp.dot(vct_ref[0], p.astype(BF16), preferred_element_type=F32)
    score = p[:, 0:tq]
    for h in range(1, NSA_HPG):
        score = score + p[:, h * tq:(h + 1) * tq]

    cb = (t0 + lax.broadcasted_iota(jnp.int32, (1, tq), 1)) >> 6
    sel = _select_blocks(score, cb, n_keep)
    sel_ref[...] = jnp.concatenate([sel] * NSA_HPG, axis=1)

    nbk = tk // NSA_BLOCK
    cw = min(r, _LANE_CHUNK)
    acc_ref[...] = jnp.zeros_like(acc_ref)
    st_ref[...] = jnp.full(st_ref.shape, NEG, F32)

    def tile_scores(j, buf, causal, m_in):
        k = ks_ref[pl.ds(pl.multiple_of(j * tk, tk), tk), :]
        bias = (sel_ref[pl.ds(pl.multiple_of(j * nbk, nbk), nbk), :] - 1.0) * (-NEG)
        for c0 in range(0, r, cw):
            cs = slice(c0, c0 + cw)
            sc = jnp.dot(k, qt[:, cs], preferred_element_type=F32)
            sc = sc + jnp.broadcast_to(bias[:, None, cs], (nbk, NSA_BLOCK, cw)).reshape(tk, cw)
            if causal:
                pos = j * tk + lax.broadcasted_iota(jnp.int32, (tk, cw), 0)
                sc = jnp.where(pos <= t_row[:, cs], sc, NEG)
            s_ref[buf, :, cs] = sc
            st_ref[buf, 0:1, cs] = m_in[:, cs]
            st_ref[buf, 1:2, cs] = jnp.maximum(m_in[:, cs], jnp.max(sc, axis=0, keepdims=True))

    def tile_accumulate(j, buf):
        for c0 in range(0, r, cw):
            cs = slice(c0, c0 + cw)
            m_in, m_out = st_ref[buf, 0:1, cs], st_ref[buf, 1:2, cs]
            alpha = jnp.exp2(m_in - m_out)
            pp = jnp.exp2(s_ref[buf, :, cs] - m_out)
            pv = jnp.dot(vst_ref[0, j], pp.astype(BF16), preferred_element_type=F32)
            acc_ref[:, cs] = acc_ref[:, cs] * alpha + pv

    m_after = lambda buf: st_ref[buf, 1:2, :]
    n_full = t0 // tk
    neg_row = jnp.full((1, r), NEG, F32)

    @pl.when(n_full == 0)
    def _():
        tile_scores(0, 0, True, neg_row)
        tile_accumulate(0, 0)

    @pl.when(n_full > 0)
    def _():
        tile_scores(0, 0, False, neg_row)

    n_pairs = jnp.maximum(n_full - 1, 0) // 2

    def pair(u, carry):
        j = 2 * u
        tile_accumulate(j, 0)
        tile_scores(j + 1, 1, False, m_after(0))
        tile_accumulate(j + 1, 1)
        tile_scores(j + 2, 0, False, m_after(1))
        return carry

    lax.fori_loop(0, n_pairs, pair, 0)
    done = 2 * n_pairs

    @pl.when((n_full > 0) & (n_full - done == 1))
    def _():
        tile_accumulate(done, 0)
        tile_scores(n_full, 1, True, m_after(0))
        tile_accumulate(n_full, 1)

    @pl.when((n_full > 0) & (n_full - done == 2))
    def _():
        tile_accumulate(done, 0)
        tile_scores(done + 1, 1, False, m_after(0))
        tile_accumulate(done + 1, 1)
        tile_scores(n_full, 0, True, m_after(1))
        tile_accumulate(n_full, 0)

    m = jnp.where(n_full % 2 == 1, m_after(1), m_after(0))
    o_slc = jnp.where(m > 0.5 * NEG, acc_ref[0:NSA_DV, :] / acc_ref[NSA_DV:NSA_DV + 1, :], 0.0)

    n_w = NSA_WINDOW // tq + 1
    k_parts, v_parts = [], []
    for d in range(n_w):
        start = jnp.maximum(t0 - NSA_WINDOW + d * tq, 0)
        k_parts.append(kw_ref[pl.ds(pl.multiple_of(start, tq), tq), :])
        v_parts.append(vwt_ref[0, jnp.maximum(i - (n_w - 1) + d, 0)])
    kcat = jnp.concatenate(k_parts, axis=0)
    sw = jnp.dot(kcat, qt, preferred_element_type=F32)
    spos = t0 - NSA_WINDOW + lax.broadcasted_iota(jnp.int32, (n_w * tq, r), 0)
    dt = t_row - spos
    pw = _masked_softmax_cols(sw, (dt >= 0) & (dt <= NSA_WINDOW) & (spos >= 0))
    o_win = jnp.dot(jnp.concatenate(v_parts, axis=1), pw.astype(BF16), preferred_element_type=F32)

    for h in range(NSA_HPG):
        sl = slice(h * tq, (h + 1) * tq)
        o = (gt_ref[3 * h:3 * h + 1, :] * o_cmp[:, sl] + gt_ref[3 * h + 1:3 * h + 2, :] * o_slc[:, sl]
             + gt_ref[3 * h + 2:3 * h + 3, :] * o_win[:, sl])
        o_ref[:, h * NSA_DV:(h + 1) * NSA_DV] = o.T


def _nsa_prompt(qt, kc, vct, ks, vst, kw, vwt, gt, tq, tk):
    t = ks.shape[0]
    nb = t // NSA_BLOCK
    r = NSA_HPG * tq
    resident = dict(pipeline_mode=pl.Buffered(1))
    return pl.pallas_call(
        functools.partial(_nsa_prompt_kernel, tq=tq, tk=tk, n_keep=min(NSA_N_SEL, nb)),
        grid=(NSA_GROUPS, t // tq),
        in_specs=[pl.BlockSpec((NSA_HPG, DKP, tq), lambda g, i: (g, 0, i)),
                  pl.BlockSpec((1, nb, DKP), lambda g, i: (g, 0, 0)),
                  pl.BlockSpec((1, NSA_DV, nb), lambda g, i: (g, 0, 0)),
                  pl.BlockSpec((t, DKP), lambda g, i: (0, g), **resident),
                  pl.BlockSpec((1, t // tk, NSA_DV + ONES_ROWS, tk), lambda g, i: (g, 0, 0, 0), **resident),
                  pl.BlockSpec((t, DKP), lambda g, i: (0, g), **resident),
                  pl.BlockSpec((1, t // tq, NSA_DV, tq), lambda g, i: (g, 0, 0, 0), **resident),
                  pl.BlockSpec((16, tq), lambda g, i: (g, i))],
        out_specs=pl.BlockSpec((tq, NSA_HPG * NSA_DV), lambda g, i: (i, g)),
        out_shape=jax.ShapeDtypeStruct((t, NSA_WIDTH), F32),
        scratch_shapes=[pltpu.VMEM((nb, r), F32), pltpu.VMEM((NSA_DV + ONES_ROWS, r), F32),
                        pltpu.VMEM((2, tk, r), F32), pltpu.VMEM((2, 8, r), F32)],
        compiler_params=_cparams(("arbitrary", "arbitrary")),
        name="nsa_prompt",
    )(qt, kc, vct, ks, vst, kw, vwt, gt)


def _nsa_sample_kernel(pt_ref, *refs, past, ts, n_keep):
    del pt_ref
    npg = _PAGES_PER_STEP
    kp, vp = refs[:npg], refs[npg:2 * npg]
    (qz_ref, qzp_ref, kc_ref, vc_ref, ksn_ref, vsn_ref, kwb_ref, vwb_ref, kwn_ref, vwn_ref, gate_ref,
     o_ref, sel_ref, acc_ref, ml_ref, ocw_ref) = refs[2 * npg:]
    c = pl.program_id(1)
    ng, nl = NSA_GROUPS, 128
    lg = nl // ng
    lane = lax.broadcasted_iota(jnp.int32, (1, nl), 1)
    tt = lane % ts
    t_row = past + tt
    lane_g = lane // lg

    def scores(k_of_g, q_ref, keys_on_lanes=False):
        dims = (((0,), (0,)), ((), ())) if keys_on_lanes else (((1,), (0,)), ((), ()))
        s = lax.dot_general(k_of_g(0), q_ref[0, 0], dims, preferred_element_type=F32)
        for g in range(1, ng):
            s = s + lax.dot_general(k_of_g(g), q_ref[0, g], dims, preferred_element_type=F32)
        return s

    def group_rows(ref, g, n):
        return ref[0, pl.ds(g, n, stride=ng), :]

    def pv(v_of_g, p):
        o = None
        for g in range(ng):
            pg = jnp.where(lane_g == g, p, 0.0).astype(BF16)
            d = lax.dot_general(v_of_g(g), pg, (((0,), (0,)), ((), ())), preferred_element_type=F32)
            o = d if o is None else o + d
        return o

    def pad_rows(x):
        return jnp.concatenate([x, jnp.zeros((16 - ts, x.shape[1]), x.dtype)], axis=0).astype(BF16)

    def update(sc, mask, v_of_g):
        sc = jnp.where(mask, sc, NEG)
        m, l = ml_ref[0:1, :], ml_ref[1:2, :]
        m_new = jnp.maximum(m, jnp.max(sc, axis=0, keepdims=True))
        alpha = jnp.exp2(m - m_new)
        pp = jnp.exp2(sc - m_new)
        ml_ref[0:1, :] = m_new
        ml_ref[1:2, :] = l * alpha + jnp.sum(pp, axis=0, keepdims=True)
        acc_ref[...] = acc_ref[...] * alpha + pv(v_of_g, pp)

    @pl.when(c == 0)
    def _():
        nb = kc_ref.shape[2]
        s = scores(lambda g: kc_ref[0, g], qzp_ref)
        n_io = lax.broadcasted_iota(jnp.int32, (nb, nl), 0)
        p = _masked_softmax_cols(s, n_io < ((t_row + 1) >> 6))
        ocw_ref[0] = pv(lambda g: vc_ref[0, g], p)
        li = lax.broadcasted_iota(jnp.int32, (nl, nl), 0)
        lj = lax.broadcasted_iota(jnp.int32, (nl, nl), 1)
        same = ((li // lg) == (lj // lg)) & ((li % ts) == (lj % ts))
        mm = same.astype(F32).astype(BF16)
        hi = p.astype(BF16)
        r1 = p - hi.astype(F32)
        mid = r1.astype(BF16)
        lo = (r1 - mid.astype(F32)).astype(BF16)
        score = (jnp.dot(hi, mm, preferred_element_type=F32) + jnp.dot(mid, mm, preferred_element_type=F32)
                 + jnp.dot(lo, mm, preferred_element_type=F32))
        sel_ref[...] = _select_blocks(score, t_row >> 6, n_keep)
        w = kwb_ref.shape[4]
        s_buf = scores(lambda g: kwb_ref[0, 0, g].astype(BF16), qz_ref, True)
        s_new = scores(lambda g: pad_rows(kwn_ref[:, g * DKP:(g + 1) * DKP]), qzp_ref)
        sw = jnp.concatenate([s_buf, s_new], axis=0)
        r_io = lax.broadcasted_iota(jnp.int32, (w + 16, nl), 0)
        spos = past - w + r_io
        dt = t_row - spos
        pw = _masked_softmax_cols(sw, (dt >= 0) & (dt <= NSA_WINDOW) & (spos >= 0) & (r_io < w + ts))
        ocw_ref[1] = (pv(lambda g: group_rows(vwb_ref, g, w).astype(BF16), pw[:w])
                      + pv(lambda g: pad_rows(vwn_ref[:, g * NSA_DV:(g + 1) * NSA_DV]), pw[w:]))
        acc_ref[...] = jnp.zeros_like(acc_ref)
        ml_ref[...] = jnp.zeros_like(ml_ref)
        ml_ref[0:1, :] = jnp.full((1, nl), NEG, F32)

    nbk = 2 * npg
    sc = scores(lambda g: jnp.concatenate([kp[k][0, 0, g] for k in range(npg)], axis=1).astype(BF16),
                qz_ref, True)
    selt = sel_ref[pl.ds(pl.multiple_of(c * nbk, nbk), nbk), :]
    mask = jnp.broadcast_to(selt[:, None, :], (nbk, NSA_BLOCK, nl)).reshape(nbk * NSA_BLOCK, nl) > 0.0
    update(sc, mask,
           lambda g: jnp.concatenate([group_rows(vp[k], g, PAGE_SIZE) for k in range(npg)], axis=0).astype(BF16))

    @pl.when(c == pl.num_programs(1) - 1)
    def _():
        sn = scores(lambda g: pad_rows(ksn_ref[:, g * DKP:(g + 1) * DKP]), qzp_ref)
        u = lax.broadcasted_iota(jnp.int32, (16, nl), 0)
        update(sn, (u <= tt) & (u < ts), lambda g: pad_rows(vsn_ref[:, g * NSA_DV:(g + 1) * NSA_DV]))
        m, l = ml_ref[0:1, :], ml_ref[1:2, :]
        o_slc = jnp.where(m > 0.5 * NEG, acc_ref[...] / l, 0.0)
        o_ref[0] = gate_ref[0, 0:1, :] * ocw_ref[0] + gate_ref[0, 1:2, :] * o_slc + gate_ref[0, 2:3, :] * ocw_ref[1]


def _nsa_sample(page_table, cache_k, cache_v, qz, qzp, kc, vc, ksn, vsn, kwb, vwb, kwn, vwn, gates, ts):
    db, n_pages = page_table.shape
    npg = _PAGES_PER_STEP
    steps = n_pages // npg
    past = n_pages * PAGE_SIZE
    nb = past // NSA_BLOCK
    w = kwb.shape[4]
    assert NSA_HPG * ts * NSA_GROUPS == 128 and ts <= 16

    def kpage_spec(k):
        return pl.BlockSpec((1, 1, NSA_GROUPS, NSA_DK, PAGE_SIZE), lambda b, c, pt: (0, pt[b, c * npg + k], 0, 0, 0))

    def vpage_spec(k):
        return pl.BlockSpec((1, PAGE_SIZE * NSA_GROUPS, NSA_DV), lambda b, c, pt: (pt[b, c * npg + k], 0, 0))

    per_seq = lambda *shape: pl.BlockSpec((1,) + shape, lambda b, c, pt: (b,) + (0,) * len(shape))
    rows = lambda width: pl.BlockSpec((ts, width), lambda b, c, pt: (b, 0))
    kwin = pl.BlockSpec((1, 1, NSA_GROUPS, NSA_DK, w), lambda b, c, pt: (0, b, 0, 0, 0))
    vwin = pl.BlockSpec((1, w * NSA_GROUPS, NSA_DV), lambda b, c, pt: (b, 0, 0))
    grid_spec = pltpu.PrefetchScalarGridSpec(
        num_scalar_prefetch=1,
        grid=(db, steps),
        in_specs=([kpage_spec(k) for k in range(npg)] + [vpage_spec(k) for k in range(npg)] + [
            per_seq(NSA_GROUPS, NSA_DK, 128), per_seq(NSA_GROUPS, DKP, 128),
            per_seq(NSA_GROUPS, nb, DKP), per_seq(NSA_GROUPS, nb, NSA_DV),
            rows(NSA_GROUPS * DKP), rows(NSA_GROUPS * NSA_DV),
            kwin, vwin,
            rows(NSA_GROUPS * DKP), rows(NSA_GROUPS * NSA_DV),
            per_seq(8, 128)]),
        out_specs=per_seq(NSA_DV, 128),
        scratch_shapes=[pltpu.VMEM((nb, 128), F32), pltpu.VMEM((NSA_DV, 128), F32),
                        pltpu.VMEM((8, 128), F32), pltpu.VMEM((2, NSA_DV, 128), F32)])
    return pl.pallas_call(
        functools.partial(_nsa_sample_kernel, past=past, ts=ts, n_keep=min(NSA_N_SEL - 1, nb)),
        grid_spec=grid_spec,
        out_shape=jax.ShapeDtypeStruct((db, NSA_DV, 128), F32),
        compiler_params=_cparams(("arbitrary", "arbitrary")),
        name="nsa_sample",
    )(page_table, *([cache_k] * npg), *([cache_v] * npg), qz, qzp, kc, vc, ksn, vsn, kwb, vwb, kwn, vwn, gates)


def _split3(x):
    hi = x.astype(BF16)
    r1 = x - hi.astype(F32)
    mid = r1.astype(BF16)
    lo = (r1 - mid.astype(F32)).astype(BF16)
    return jnp.concatenate([hi, mid, lo], axis=1)


_GLA_HEADS_PER_STEP = 2


def _gla_kernel(q_ref, k_ref, v_ref, a_ref, wa_ref, ba_ref, s0_ref, o_ref, sout_ref, st_ref,
                *, c, n_sub, n_valid):
    ci = pl.program_id(2)
    levels = int(math.log2(c))
    hp = _GLA_HEADS_PER_STEP
    dk, dv = GLA_DK, GLA_DV

    @pl.when(ci == 0)
    def _():
        for hh in range(hp):
            st_ref[hh] = s0_ref[0, hh].T

    t_io = lax.broadcasted_iota(jnp.int32, (c, c), 0)
    s_io = lax.broadcasted_iota(jnp.int32, (c, c), 1)
    row = lax.broadcasted_iota(jnp.int32, (c, 1), 0)
    sels = [s_io <= t_io]
    for lv in range(1, levels + 1):
        sels.append(s_io <= ((t_io >> lv) << lv) + (1 << (lv - 1)) - 1)
    sel_all = jnp.concatenate(sels, axis=0).astype(F32).astype(BF16)
    for u in range(n_sub):
        rows = slice(u * c, (u + 1) * c)
        a_bf = a_ref[rows, :].astype(BF16)
        for hh in range(hp):
            kcols = slice(hh * dk, (hh + 1) * dk)
            vcols = slice(hh * dv, (hh + 1) * dv)
            q = q_ref[rows, kcols] * (GLA_DK ** -0.5)
            k = k_ref[rows, kcols]
            v = v_ref[rows, vcols].astype(BF16)
            z = jnp.dot(a_bf, wa_ref[:, kcols], preferred_element_type=F32) + ba_ref[:, kcols]
            la = (jnp.minimum(z, 0.0) - jnp.log1p(jnp.exp(-jnp.abs(z)))) * (1.0 / GLA_TAU)
            if n_valid < c:
                la = jnp.where(row < n_valid, la, 0.0)
            big = jnp.dot(sel_all, _split3(la), preferred_element_type=F32)
            big = big[:, 0:dk] + big[:, dk:2 * dk] + big[:, 2 * dk:3 * dk]
            b = big[0:c]
            st = st_ref[hh]
            o = lax.dot_general((q * jnp.exp(b)).astype(BF16), st.astype(BF16),
                                (((1,), (1,)), ((), ())), preferred_element_type=F32)
            a_mat = jnp.where(t_io == s_io, jnp.sum(q * k, axis=1, keepdims=True), 0.0)
            for lv in range(1, levels + 1):
                bref = big[lv * c:(lv + 1) * c]
                upper = ((row >> (lv - 1)) & 1) == 1
                ql = jnp.where(upper, q * jnp.exp(b - bref), 0.0).astype(BF16)
                kl = jnp.where(upper, 0.0, k * jnp.exp(bref - b)).astype(BF16)
                al = lax.dot_general(ql, kl, (((1,), (1,)), ((), ())), preferred_element_type=F32)
                a_mat = a_mat + jnp.where((t_io >> lv) == (s_io >> lv), al, 0.0)
            o = o + jnp.dot(a_mat.astype(BF16), v, preferred_element_type=F32)
            o_ref[rows, vcols] = o
            b_last = b[c - 1:c, :]
            kd = (k * jnp.exp(b_last - b)).astype(BF16)
            st_ref[hh] = st * jnp.exp(b_last) + lax.dot_general(
                v, kd, (((0,), (0,)), ((), ())), preferred_element_type=F32)

    @pl.when(ci == pl.num_programs(2) - 1)
    def _():
        for hh in range(hp):
            sout_ref[0, hh] = st_ref[hh].T


def _gla(p, n_batch, t_len, s0, wa_pad, ba, c, n_sub, n_valid):
    step = c * n_sub
    ncs = t_len // step
    hp = _GLA_HEADS_PER_STEP
    qb, kb = _SEG["q_g"] // (hp * GLA_DK), _SEG["k_g"] // (hp * GLA_DK)
    vb, ab = _SEG["v_g"] // (hp * GLA_DV), _SEG["a_g"] // 256
    return pl.pallas_call(
        functools.partial(_gla_kernel, c=c, n_sub=n_sub, n_valid=n_valid),
        grid=(n_batch, GLA_HEADS // hp, ncs),
        in_specs=[pl.BlockSpec((step, hp * GLA_DK), lambda b, h, ci: (b * ncs + ci, qb + h)),
                  pl.BlockSpec((step, hp * GLA_DK), lambda b, h, ci: (b * ncs + ci, kb + h)),
                  pl.BlockSpec((step, hp * GLA_DV), lambda b, h, ci: (b * ncs + ci, vb + h)),
                  pl.BlockSpec((step, 256), lambda b, h, ci: (b * ncs + ci, ab)),
                  pl.BlockSpec((256, hp * GLA_DK), lambda b, h, ci: (0, h)),
                  pl.BlockSpec((1, hp * GLA_DK), lambda b, h, ci: (0, h)),
                  pl.BlockSpec((1, hp, GLA_DK, GLA_DV), lambda b, h, ci: (b, h, 0, 0))],
        out_specs=[pl.BlockSpec((step, hp * GLA_DV), lambda b, h, ci: (b * ncs + ci, h)),
                   pl.BlockSpec((1, hp, GLA_DK, GLA_DV), lambda b, h, ci: (b, h, 0, 0))],
        out_shape=[jax.ShapeDtypeStruct((n_batch * t_len, GLA_WIDTH), F32),
                   jax.ShapeDtypeStruct((n_batch, GLA_HEADS, GLA_DK, GLA_DV), F32)],
        scratch_shapes=[pltpu.VMEM((hp, GLA_DV, GLA_DK), F32)],
        compiler_params=_cparams(("arbitrary", "arbitrary", "arbitrary")),
        name="gla",
    )(p, p, p, p, wa_pad, ba, s0)


def _merge1_kernel(on_ref, zn_ref, og_ref, zg_ref, gg_ref, wa_ref, wb_ref, ma_ref, mb_ref, o_ref,
                   a_ref, b_ref):
    @pl.when(pl.program_id(1) == 0)
    def _():
        a_ref[...] = (on_ref[...] * _silu(zn_ref[...])).astype(BF16)
        og = og_ref[...]
        zg = zg_ref[...]
        for h in range(GLA_HEADS):
            sl = slice(h * GLA_DV, (h + 1) * GLA_DV)
            x = og[:, sl]
            y = x * lax.rsqrt(jnp.mean(x * x, axis=-1, keepdims=True) + NORM_EPS) * gg_ref[...]
            b_ref[:, sl] = (y * _silu(zg[:, sl])).astype(BF16)

    ua = jnp.dot(a_ref[...], wa_ref[...], preferred_element_type=F32)
    ub = jnp.dot(b_ref[...], wb_ref[...], preferred_element_type=F32)
    o_ref[...] = (jax.nn.sigmoid(ma_ref[...]) * ua + jax.nn.sigmoid(mb_ref[...]) * ub).astype(BF16)


def _merge1(o_nsa, o_gla, p, gla_g, wb_a, wb_b, tm):
    m = p.shape[0]
    tn = 512
    zn, zg, mg = _SEG["z_nsa"] // NSA_WIDTH, _SEG["z_g"] // GLA_WIDTH, _SEG["m_gate"] // tn
    row = lambda i, j: (i, 0)
    return pl.pallas_call(
        _merge1_kernel,
        grid=(m // tm, D_MODEL // tn),
        in_specs=[pl.BlockSpec((tm, NSA_WIDTH), row),
                  pl.BlockSpec((tm, NSA_WIDTH), lambda i, j: (i, zn)),
                  pl.BlockSpec((tm, GLA_WIDTH), row),
                  pl.BlockSpec((tm, GLA_WIDTH), lambda i, j: (i, zg)),
                  pl.BlockSpec((1, GLA_DV), lambda i, j: (0, 0)),
                  pl.BlockSpec((NSA_WIDTH, tn), lambda i, j: (0, j)),
                  pl.BlockSpec((GLA_WIDTH, tn), lambda i, j: (0, j)),
                  pl.BlockSpec((tm, tn), lambda i, j: (i, mg + j)),
                  pl.BlockSpec((tm, tn), lambda i, j: (i, mg + D_MODEL // tn + j))],
        out_specs=pl.BlockSpec((tm, tn), lambda i, j: (i, j)),
        out_shape=jax.ShapeDtypeStruct((m, D_MODEL), BF16),
        scratch_shapes=[pltpu.VMEM((tm, NSA_WIDTH), BF16), pltpu.VMEM((tm, GLA_WIDTH), BF16)],
        compiler_params=_cparams(("arbitrary", "arbitrary")),
        name="merge1",
    )(o_nsa, p, o_gla, p, gla_g, wb_a, wb_b, p, p)


def _merge2_kernel(x_ref, gate_ref, mg_ref, w_ref, o_ref):
    o_ref[...] = x_ref[...] + gate_ref[...] * jnp.dot(mg_ref[...], w_ref[...], preferred_element_type=F32)


def _merge2(x, gate, merged, w_out, tm):
    m, d = x.shape
    tn = 512
    per_row = gate.shape[0] != 1
    gate_spec = (pl.BlockSpec((tm, tn), lambda i, j: (i, j)) if per_row
                 else pl.BlockSpec((1, tn), lambda i, j: (0, j)))
    return pl.pallas_call(
        _merge2_kernel,
        grid=(m // tm, d // tn),
        in_specs=[pl.BlockSpec((tm, tn), lambda i, j: (i, j)), gate_spec,
                  pl.BlockSpec((tm, d), lambda i, j: (i, 0)),
                  pl.BlockSpec((d, tn), lambda i, j: (0, j))],
        out_specs=pl.BlockSpec((tm, tn), lambda i, j: (i, j)),
        out_shape=jax.ShapeDtypeStruct((m, d), F32),
        compiler_params=_cparams(("arbitrary", "arbitrary")),
        name="merge2",
    )(x, gate, merged, w_out)


def _pad_last(a, width):
    return jnp.pad(a, [(0, 0)] * (a.ndim - 1) + [(0, width - a.shape[-1])])


def _pad_heads(w, n, src, dst):
    return _pad_last(w.reshape(w.shape[0], n, src), dst).reshape(w.shape[0], n * dst)


def _pad_w_in(w_in):
    sizes = (NSA_HEADS * NSA_DK, NSA_GROUPS * NSA_DK, NSA_GROUPS * NSA_DV, NSA_GROUPS * NSA_DK,
             NSA_GROUPS * NSA_DV, NSA_GROUPS * NSA_DK, NSA_GROUPS * NSA_DV, 3 * NSA_HEADS, NSA_WIDTH,
             GLA_HEADS * GLA_DK, GLA_HEADS * GLA_DK, GLA_WIDTH, GLA_WIDTH, GLA_RANK, 2 * D_MODEL)
    cuts = [0]
    for s in sizes:
        cuts.append(cuts[-1] + s)
    names = ("q", "k_cmp", "v_cmp", "k_slc", "v_slc", "k_win", "v_win", "g_nsa", "z_nsa",
             "q_g", "k_g", "v_g", "z_g", "a_g", "m_gate")
    parts = {nm: w_in[:, cuts[n]:cuts[n + 1]] for n, nm in enumerate(names)}
    parts["q"] = _pad_heads(parts["q"], NSA_HEADS, NSA_DK, DKP)
    for nm in ("k_cmp", "k_slc", "k_win"):
        parts[nm] = _pad_heads(parts[nm], NSA_GROUPS, NSA_DK, DKP)
    parts["g_nsa"] = _pad_last(_pad_heads(parts["g_nsa"], NSA_GROUPS, 3 * NSA_HPG, 16), 256)
    parts["a_g"] = _pad_last(parts["a_g"], 256)
    return jnp.concatenate([parts[nm] for nm in _SEG], axis=1).astype(BF16)


def _unpad_heads(a, n):
    return a.reshape(a.shape[0], n, DKP)[:, :, :NSA_DK]


def kernel(x_prompt, x_sample, cache_k_cmp, cache_v_cmp, cache_k_slc, cache_v_slc, state_k_win, state_v_win, state_gla, page_table, c_prompt, c_sample, ln_g, w_ada, b_ada, w_in, q_norm_g, k_cmp_norm_g, k_slc_norm_g, k_win_norm_g, pe_k, w1_k, w2_k, pe_v, w1_v, w2_v, w_a2, b_a, gla_norm_g, w_branch, w_out):
    assert ln_g.shape[0] == 1, "single layer"
    bp, t, d = x_prompt.shape
    db, ts, _ = x_sample.shape
    assert bp == 1 and d == D_MODEL
    G = NSA_GROUPS

    w_pad = _pad_w_in(w_in[0])
    g_q = _pad_last(q_norm_g, DKP)
    g_kc = _pad_last(k_cmp_norm_g, DKP)
    g_ks = _pad_last(k_slc_norm_g, DKP)
    g_kw = _pad_last(k_win_norm_g, DKP)
    pe_k_p = _pad_last(pe_k[0], DKP)[:, None, :]
    w1_k_p = jnp.pad(w1_k[0], ((0, 0), (0, DKP - NSA_DK), (0, 0))).astype(BF16)
    w2_k_p = _pad_last(w2_k[0], DKP).astype(BF16)
    pe_v_p = pe_v[0][:, None, :]
    w1_v_p = w1_v[0].astype(BF16)
    w2_v_p = w2_v[0].astype(BF16)
    wa_pad = jnp.pad(w_a2[0], ((0, 256 - GLA_RANK), (0, 0))).astype(BF16)
    wb_a = w_branch[0, :NSA_WIDTH].astype(BF16)
    wb_b = w_branch[0, NSA_WIDTH:].astype(BF16)
    w_o = w_out[0].astype(BF16)
    ones_v = jnp.ones((1, NSA_DV), F32)

    n_c = bp + db
    c_all = jnp.pad(jnp.concatenate([c_prompt, c_sample], axis=0), ((0, (-n_c) % 8), (0, 0)))
    mod = _ada(c_all, w_ada[0], b_ada)
    shift, scale, gate = mod[:, :d], mod[:, d:2 * d], mod[:, 2 * d:]

    tm = min(1024, t)
    xp = x_prompt.reshape(t, d)
    pp = _proj(xp, scale[0:1], shift[0:1], ln_g, w_pad, tm)
    tq = min(256, t)
    tk = min(512, t)
    qt = _prep_q(pp, g_q, min(2048, t))
    ks_f, ks_b = _prep_k(pp, "k_slc", g_ks, min(2048, t))
    kw_f, kw_b = _prep_k(pp, "k_win", g_kw, min(2048, t))
    vst = _prep_vt(pp, "v_slc", tk, ONES_ROWS)
    vwt = _prep_vt(pp, "v_win", tq)
    gt = _prep_gate(pp, min(512, t))
    kc = _compress_prompt(pp, "k_cmp", DKP, pe_k_p, w1_k_p, w2_k_p, g_kc, True, False)
    vct = _compress_prompt(pp, "v_cmp", NSA_DV, pe_v_p, w1_v_p, w2_v_p, ones_v, False, True)
    o_nsa_p = _nsa_prompt(qt, kc, vct, ks_b, vst, kw_b, vwt, gt, tq, tk)
    s0_p = jnp.zeros((bp, GLA_HEADS, GLA_DK, GLA_DV), F32)
    o_gla_p, gla_p = _gla(pp, bp, t, s0_p, wa_pad, b_a, GLA_CHUNK, 4, GLA_CHUNK)
    merged_p = _merge1(o_nsa_p, o_gla_p, pp, gla_norm_g, wb_a, wb_b, min(512, t))
    y_p = _merge2(xp, gate[0:1], merged_p, w_o, min(512, t)).reshape(bp, t, d)

    wp = min(NSA_WINDOW, t)
    k_cmp_p = _unpad_heads(pp[:, _SEG["k_cmp"]:_SEG["k_cmp"] + G * DKP], G)
    v_cmp_p = pp[:, _SEG["v_cmp"]:_SEG["v_cmp"] + G * NSA_DV].reshape(t, G, NSA_DV)
    k_slc_p = _unpad_heads(ks_f, G)
    v_slc_p = pp[:, _SEG["v_slc"]:_SEG["v_slc"] + G * NSA_DV].reshape(t, G, NSA_DV)
    k_win_p = _unpad_heads(kw_f, G)[t - wp:]
    v_win_p = pp[t - wp:, _SEG["v_win"]:_SEG["v_win"] + G * NSA_DV].reshape(wp, G, NSA_DV)

    ms = db * ts
    xs = x_sample.reshape(ms, d)
    rep = lambda a: jnp.repeat(a[bp:bp + db], ts, axis=0)
    ps = _proj(xs, rep(scale), rep(shift), ln_g, w_pad, ms)
    seg = lambda name, w: ps[:, _SEG[name]:_SEG[name] + w]
    qt_s = _prep_q(ps, g_q, ms)
    kss_f, _ = _prep_k(ps, "k_slc", g_ks, ms)
    kws_f, _ = _prep_k(ps, "k_win", g_kw, ms)
    gt_s = _prep_gate(ps, ms)
    v_slc_new = seg("v_slc", G * NSA_DV)
    v_win_new = seg("v_win", G * NSA_DV)
    k_cmp_s = _unpad_heads(seg("k_cmp", G * DKP), G).reshape(db, ts, G, NSA_DK)
    v_cmp_s = seg("v_cmp", G * NSA_DV).reshape(db, ts, G, NSA_DV)
    k_slc_s = _unpad_heads(kss_f, G).reshape(db, ts, G, NSA_DK)
    v_slc_s = v_slc_new.reshape(db, ts, G, NSA_DV)
    k_win_s_new = _unpad_heads(kws_f, G).reshape(db, ts, G, NSA_DK)
    v_win_s_new = v_win_new.reshape(db, ts, G, NSA_DV)
    q4 = qt_s.reshape(G, NSA_HPG, DKP, db, ts).transpose(3, 0, 2, 1, 4).reshape(db, G, DKP, NSA_HPG * ts)
    own = (jnp.arange(G)[:, None] == jnp.arange(G)[None, :])[None, :, None, :, None]
    qzp = jnp.where(own, q4[:, :, :, None, :], jnp.zeros((), BF16)).reshape(db, G, DKP, G * NSA_HPG * ts)
    qz = qzp[:, :, :NSA_DK, :]
    gates = gt_s.reshape(G, 16, db, ts)[:, :3 * NSA_HPG].reshape(G, NSA_HPG, 3, db, ts)
    gates = gates.transpose(3, 2, 0, 1, 4).reshape(db, 3, G * NSA_HPG * ts)
    gates = jnp.pad(gates, ((0, 0), (0, 5), (0, 0)))
    keys_t = lambda a: jnp.transpose(a, (0, 1, 3, 4, 2))
    vals_2d = lambda a: a.reshape(a.shape[1], a.shape[2] * G, NSA_DV)
    kc_s = _compress_paged(keys_t(cache_k_cmp), page_table, pe_k_p[:, 0, :], w1_k_p, w2_k_p, g_kc, True)
    vc_s = _compress_paged(vals_2d(cache_v_cmp), page_table, pe_v[0], w1_v_p, w2_v_p, ones_v, False)
    o_t = _nsa_sample(page_table, keys_t(cache_k_slc), vals_2d(cache_v_slc), qz, qzp, kc_s, vc_s, kss_f,
                      v_slc_new, keys_t(state_k_win), vals_2d(state_v_win), kws_f, v_win_new, gates, ts)
    o_nsa_s = o_t.reshape(db, NSA_DV, G, NSA_HPG, ts).transpose(0, 4, 2, 3, 1)
    kw_s = jnp.concatenate([state_k_win[0][:, ts:], k_win_s_new], axis=1)
    vw_s = jnp.concatenate([state_v_win[0][:, ts:], v_win_s_new], axis=1)
    c_s = 16
    ps_pad = jnp.pad(ps.reshape(db, ts, D_PAD), ((0, 0), (0, c_s - ts), (0, 0))).reshape(db * c_s, D_PAD)
    o_gla_s, gla_s = _gla(ps_pad, db, c_s, state_gla[0], wa_pad, b_a, c_s, 1, ts)
    o_gla_s = o_gla_s.reshape(db, c_s, GLA_WIDTH)[:, :ts].reshape(ms, GLA_WIDTH)
    merged_s = _merge1(o_nsa_s.reshape(ms, NSA_WIDTH), o_gla_s, ps, gla_norm_g, wb_a, wb_b, ms)
    y_s = _merge2(xs, rep(gate), merged_s, w_o, ms).reshape(db, ts, d)

    L1 = lambda a: a[None, None]
    return (y_p, y_s,
            L1(k_cmp_p), L1(v_cmp_p), L1(k_slc_p), L1(v_slc_p), L1(k_win_p), L1(v_win_p), gla_p[None],
            k_cmp_s[None], v_cmp_s[None], k_slc_s[None], v_slc_s[None], kw_s[None], vw_s[None], gla_s[None])
```

```python
import functools
import math

import jax
import jax.numpy as jnp
from jax import lax
from jax.experimental import pallas as pl
from jax.experimental.pallas import tpu as pltpu

F32 = jnp.float32
BF16 = jnp.bfloat16

NSA_HEADS = 16
NSA_GROUPS = 4
NSA_HPG = NSA_HEADS // NSA_GROUPS
NSA_DK = 192
NSA_DV = 128
NSA_BLOCK = 64
NSA_N_SEL = 16
NSA_WINDOW = 512
NSA_CMP_HIDDEN = 256
PAGE_SIZE = 128
GLA_HEADS = 4
GLA_DK = 256
GLA_DV = 512
GLA_RANK = 16
GLA_TAU = 16.0
GLA_CHUNK = 64
NORM_EPS = 1e-6
D_MODEL = 2048
NSA_WIDTH = NSA_HEADS * NSA_DV
GLA_WIDTH = GLA_HEADS * GLA_DV

DKP = 256
LOG2E = 1.4426950408889634
NEG = -1e30
Q_SCALE = (NSA_DK ** -0.5) * LOG2E
VMEM_LIMIT = 56 * 1024 * 1024

_SEG = {}
_off = 0
for _name, _w in (("z_nsa", NSA_WIDTH), ("z_g", GLA_WIDTH), ("v_g", GLA_WIDTH), ("m_gate", 2 * D_MODEL),
                  ("q", NSA_HEADS * DKP), ("k_cmp", NSA_GROUPS * DKP), ("v_cmp", NSA_GROUPS * NSA_DV),
                  ("k_slc", NSA_GROUPS * DKP), ("v_slc", NSA_GROUPS * NSA_DV),
                  ("k_win", NSA_GROUPS * DKP), ("v_win", NSA_GROUPS * NSA_DV),
                  ("q_g", GLA_HEADS * GLA_DK), ("k_g", GLA_HEADS * GLA_DK),
                  ("g_nsa", 256), ("a_g", 256)):
    _SEG[_name] = _off
    _off += _w
D_PAD = _off


def _cparams(sem):
    return pltpu.CompilerParams(dimension_semantics=sem, vmem_limit_bytes=VMEM_LIMIT)


def _silu(x):
    return x * jax.nn.sigmoid(x)


def _ada_kernel(c_ref, w_ref, b_ref, o_ref):
    a = _silu(c_ref[...]).astype(BF16)
    o_ref[...] = jnp.dot(a, w_ref[...].astype(BF16), preferred_element_type=F32) + b_ref[...]


def _ada(c_all, w_ada, b_ada):
    m, d = c_all.shape
    n = w_ada.shape[1]
    tn = 768
    return pl.pallas_call(
        _ada_kernel,
        grid=(n // tn,),
        in_specs=[pl.BlockSpec((m, d), lambda j: (0, 0)),
                  pl.BlockSpec((d, tn), lambda j: (0, j)),
                  pl.BlockSpec((1, tn), lambda j: (0, j))],
        out_specs=pl.BlockSpec((m, tn), lambda j: (0, j)),
        out_shape=jax.ShapeDtypeStruct((m, n), F32),
        compiler_params=_cparams(("arbitrary",)),
        name="ada",
    )(c_all, w_ada, b_ada)


def _proj_kernel(x_ref, sc_ref, sh_ref, g_ref, w_ref, o_ref, h_ref):
    @pl.when(pl.program_id(1) == 0)
    def _():
        x = x_ref[...]
        y = x * lax.rsqrt(jnp.mean(x * x, axis=-1, keepdims=True) + NORM_EPS) * g_ref[...]
        h_ref[...] = (y * (1.0 + sc_ref[...]) + sh_ref[...]).astype(BF16)

    o_ref[...] = jnp.dot(h_ref[...], w_ref[...], preferred_element_type=F32)


def _proj(x, scale, shift, ln_g, w_pad, tm):
    m, d = x.shape
    n = w_pad.shape[1]
    tn = 1536
    per_row = scale.shape[0] != 1
    mod_spec = (pl.BlockSpec((tm, d), lambda i, j: (i, 0)) if per_row
                else pl.BlockSpec((1, d), lambda i, j: (0, 0)))
    return pl.pallas_call(
        _proj_kernel,
        grid=(m // tm, n // tn),
        in_specs=[pl.BlockSpec((tm, d), lambda i, j: (i, 0)), mod_spec, mod_spec,
                  pl.BlockSpec((1, d), lambda i, j: (0, 0)),
                  pl.BlockSpec((d, tn), lambda i, j: (0, j))],
        out_specs=pl.BlockSpec((tm, tn), lambda i, j: (i, j)),
        out_shape=jax.ShapeDtypeStruct((m, n), F32),
        scratch_shapes=[pltpu.VMEM((tm, d), BF16)],
        compiler_params=_cparams(("arbitrary", "arbitrary")),
        name="proj",
    )(x, scale, shift, ln_g, w_pad)


def _head_norm(x, g):
    ms = jnp.sum(x * x, axis=-1, keepdims=True) * (1.0 / NSA_DK)
    return x * lax.rsqrt(ms + NORM_EPS) * g


def _prep_q_kernel(x_ref, g_ref, o_ref):
    o_ref[0] = (_head_norm(x_ref[...], g_ref[...]) * Q_SCALE).T.astype(BF16)


def _prep_q(p, g_pad, tm):
    t = p.shape[0]
    base = _SEG["q"] // DKP
    return pl.pallas_call(
        _prep_q_kernel,
        grid=(t // tm, NSA_HEADS),
        in_specs=[pl.BlockSpec((tm, DKP), lambda i, h: (i, base + h)),
                  pl.BlockSpec((1, DKP), lambda i, h: (0, 0))],
        out_specs=pl.BlockSpec((1, DKP, tm), lambda i, h: (h, 0, i)),
        out_shape=jax.ShapeDtypeStruct((NSA_HEADS, DKP, t), BF16),
        compiler_params=_cparams(("arbitrary", "arbitrary")),
        name="prep_q",
    )(p, g_pad)


def _prep_k_kernel(x_ref, g_ref, o_ref, ob_ref):
    y = _head_norm(x_ref[...], g_ref[...])
    o_ref[...] = y
    ob_ref[...] = y.astype(BF16)


def _prep_k(p, seg, g_pad, tm):
    t = p.shape[0]
    base = _SEG[seg] // DKP
    n = NSA_GROUPS * DKP
    return pl.pallas_call(
        _prep_k_kernel,
        grid=(t // tm, NSA_GROUPS),
        in_specs=[pl.BlockSpec((tm, DKP), lambda i, g: (i, base + g)),
                  pl.BlockSpec((1, DKP), lambda i, g: (0, 0))],
        out_specs=[pl.BlockSpec((tm, DKP), lambda i, g: (i, g)),
                   pl.BlockSpec((tm, DKP), lambda i, g: (i, g))],
        out_shape=[jax.ShapeDtypeStruct((t, n), F32), jax.ShapeDtypeStruct((t, n), BF16)],
        compiler_params=_cparams(("arbitrary", "arbitrary")),
        name="prep_k_" + seg,
    )(p, g_pad)


ONES_ROWS = 16


def _prep_vt_kernel(x_ref, o_ref, *, ones_rows):
    xt = x_ref[...].T
    if ones_rows:
        xt = jnp.concatenate([xt, jnp.ones((ones_rows, xt.shape[1]), F32)], axis=0)
    o_ref[0, 0] = xt.astype(BF16)


def _prep_vt(p, seg, tile, ones_rows=0):
    t = p.shape[0]
    base = _SEG[seg] // NSA_DV
    rows = NSA_DV + ones_rows
    return pl.pallas_call(
        functools.partial(_prep_vt_kernel, ones_rows=ones_rows),
        grid=(t // tile, NSA_GROUPS),
        in_specs=[pl.BlockSpec((tile, NSA_DV), lambda i, g: (i, base + g))],
        out_specs=pl.BlockSpec((1, 1, rows, tile), lambda i, g: (g, i, 0, 0)),
        out_shape=jax.ShapeDtypeStruct((NSA_GROUPS, t // tile, rows, tile), BF16),
        compiler_params=_cparams(("arbitrary", "arbitrary")),
        name="prep_vt_" + seg,
    )(p)


def _prep_gate_kernel(x_ref, o_ref):
    o_ref[...] = jax.nn.sigmoid(x_ref[...]).T[:4 * NSA_HEADS, :]


def _prep_gate(p, tm):
    t = p.shape[0]
    base = _SEG["g_nsa"] // 256
    return pl.pallas_call(
        _prep_gate_kernel,
        grid=(t // tm,),
        in_specs=[pl.BlockSpec((tm, 256), lambda i: (i, base))],
        out_specs=pl.BlockSpec((4 * NSA_HEADS, tm), lambda i: (0, i)),
        out_shape=jax.ShapeDtypeStruct((4 * NSA_HEADS, t), F32),
        compiler_params=_cparams(("arbitrary",)),
        name="prep_gate",
    )(p)


_CMP_ROWS = 8


def _cmp_kernel(x_ref, pe_ref, w1_ref, w2_ref, g_ref, o_ref, acc_ref, *, norm, transpose_out):
    s = pl.program_id(1)

    @pl.when(s == 0)
    def _():
        acc_ref[...] = jnp.zeros_like(acc_ref)

    acc = acc_ref[...]
    for r in range(_CMP_ROWS):
        xb = (x_ref[:, r, :] + pe_ref[r]).astype(BF16)
        acc = acc + jnp.dot(xb, w1_ref[r], preferred_element_type=F32)
    acc_ref[...] = acc

    @pl.when(s == pl.num_programs(1) - 1)
    def _():
        hid = _silu(acc_ref[...]).astype(BF16)
        y = jnp.dot(hid, w2_ref[...], preferred_element_type=F32)
        if norm:
            y = _head_norm(y, g_ref[...])
        if transpose_out:
            y = y.T
        o_ref[0] = y.astype(o_ref.dtype)


def _compress_prompt(p, seg, dp, pe_pad, w1_pad, w2_pad, g_pad, norm, transpose_out):
    t = p.shape[0]
    nb = t // NSA_BLOCK
    pv = p.reshape(nb, NSA_BLOCK, D_PAD)
    base = _SEG[seg] // dp
    out_shape = (NSA_GROUPS, dp, nb) if transpose_out else (NSA_GROUPS, nb, dp)
    return pl.pallas_call(
        functools.partial(_cmp_kernel, norm=norm, transpose_out=transpose_out),
        grid=(NSA_GROUPS, NSA_BLOCK // _CMP_ROWS),
        in_specs=[pl.BlockSpec((nb, _CMP_ROWS, dp), lambda g, s: (0, s, base + g)),
                  pl.BlockSpec((_CMP_ROWS, 1, dp), lambda g, s: (s, 0, 0)),
                  pl.BlockSpec((_CMP_ROWS, dp, NSA_CMP_HIDDEN), lambda g, s: (s, 0, 0)),
                  pl.BlockSpec((NSA_CMP_HIDDEN, dp), lambda g, s: (0, 0)),
                  pl.BlockSpec((1, dp), lambda g, s: (0, 0))],
        out_specs=pl.BlockSpec((1,) + out_shape[1:], lambda g, s: (g, 0, 0)),
        out_shape=jax.ShapeDtypeStruct(out_shape, BF16),
        scratch_shapes=[pltpu.VMEM((nb, NSA_CMP_HIDDEN), F32)],
        compiler_params=_cparams(("arbitrary", "arbitrary")),
        name="cmp_" + seg,
    )(pv, pe_pad, w1_pad, w2_pad, g_pad)


_PAGES_PER_STEP = 8
_STAGE_PITCH = 72


def _cmp_paged_kernel(pt_ref, *refs, dh, keys_on_lanes, norm):
    del pt_ref
    npg = _PAGES_PER_STEP
    pages = refs[:npg]
    pe_ref, w1_ref, w2_ref, g_ref, o_ref, stage_ref, xs_ref = refs[npg:]
    c = pl.program_id(2)
    nbc = 2 * npg
    n_slab = stage_ref.shape[1]

    @pl.when((pl.program_id(0) == 0) & (pl.program_id(1) == 0) & (c == 0))
    def _():
        stage_ref[...] = jnp.zeros_like(stage_ref)

    for k in range(npg):
        for g in range(NSA_GROUPS):
            if keys_on_lanes:
                x = pages[k][0, 0, g].T
            else:
                x = pages[k][0, pl.ds(g, PAGE_SIZE, stride=NSA_GROUPS), :]
            for blk in range(2):
                r0 = (2 * k + blk) * _STAGE_PITCH
                xb = x[blk * NSA_BLOCK:(blk + 1) * NSA_BLOCK]
                stage_ref[g, 0, r0:r0 + NSA_BLOCK, :] = xb[:, 0:128]
                if n_slab == 2:
                    stage_ref[g, 1, r0:r0 + NSA_BLOCK, 0:dh - 128] = xb[:, 128:dh]
    dp = pe_ref.shape[1]
    row0 = pl.multiple_of(c * nbc, nbc)
    for s in range(NSA_BLOCK):
        for g in range(NSA_GROUPS):
            parts = [stage_ref[g, j, pl.ds(s, nbc, stride=_STAGE_PITCH), :] for j in range(n_slab)]
            x = parts[0] if n_slab == 1 else jnp.concatenate(parts, axis=1)
            xs_ref[g, pl.ds(row0, nbc), s * dp:(s + 1) * dp] = (x + pe_ref[s:s + 1, :]).astype(BF16)

    @pl.when(c == pl.num_programs(2) - 1)
    def _():
        for g in range(NSA_GROUPS):
            acc = jnp.dot(xs_ref[g], w1_ref[...], preferred_element_type=F32)
            hid = _silu(acc).astype(BF16)
            y = jnp.dot(hid, w2_ref[...], preferred_element_type=F32)
            if norm:
                y = _head_norm(y, g_ref[...])
            o_ref[0, g] = y.astype(BF16)


def _compress_paged(cache, page_table, pe_pad, w1_pad, w2_pad, g_pad, norm):
    keys_on_lanes = cache.ndim == 5
    dh = cache.shape[3] if keys_on_lanes else cache.shape[2]
    dp = pe_pad.shape[-1]
    db, n_pages = page_table.shape
    npg = _PAGES_PER_STEP
    steps = n_pages // npg
    spp = min(8, steps)
    m_blocks = spp * 2 * npg
    n_ph = steps // spp
    n_slab = dp // 128

    def page_spec(k):
        page = lambda b, ph, c, pt: pt[b, (ph * spp + c) * npg + k]
        if keys_on_lanes:
            return pl.BlockSpec((1, 1, NSA_GROUPS, dh, PAGE_SIZE), lambda b, ph, c, pt: (0, page(b, ph, c, pt), 0, 0, 0))
        return pl.BlockSpec((1, PAGE_SIZE * NSA_GROUPS, dh), lambda b, ph, c, pt: (page(b, ph, c, pt), 0, 0))

    const = lambda *shape: pl.BlockSpec(shape, lambda b, ph, c, pt: (0,) * len(shape))
    grid_spec = pltpu.PrefetchScalarGridSpec(
        num_scalar_prefetch=1,
        grid=(db, n_ph, spp),
        in_specs=[page_spec(k) for k in range(npg)] + [
            const(NSA_BLOCK, dp),
            pl.BlockSpec((NSA_BLOCK * dp, NSA_CMP_HIDDEN), lambda b, ph, c, pt: (0, 0),
                         pipeline_mode=pl.Buffered(1)),
            const(NSA_CMP_HIDDEN, dp), const(1, dp)],
        out_specs=pl.BlockSpec((1, NSA_GROUPS, m_blocks, dp), lambda b, ph, c, pt: (b, 0, ph, 0)),
        scratch_shapes=[pltpu.VMEM((NSA_GROUPS, n_slab, 2 * npg * _STAGE_PITCH, 128), F32),
                        pltpu.VMEM((NSA_GROUPS, m_blocks, NSA_BLOCK * dp), BF16)])
    return pl.pallas_call(
        functools.partial(_cmp_paged_kernel, dh=dh, keys_on_lanes=keys_on_lanes, norm=norm),
        grid_spec=grid_spec,
        out_shape=jax.ShapeDtypeStruct((db, NSA_GROUPS, n_pages * 2, dp), BF16),
        compiler_params=_cparams(("arbitrary", "arbitrary", "arbitrary")),
        name="cmp_paged_%d" % dh,
    )(page_table, *([cache] * npg), pe_pad, w1_pad.reshape(NSA_BLOCK * dp, NSA_CMP_HIDDEN), w2_pad, g_pad)


def _masked_softmax_cols(s, valid):
    s = jnp.where(valid, s, NEG)
    m = jnp.max(s, axis=0, keepdims=True)
    e = jnp.where(valid, jnp.exp2(s - m), 0.0)
    return e / jnp.maximum(jnp.sum(e, axis=0, keepdims=True), 1e-30)


_LANE_CHUNK = 256


def _select_blocks(score, cb, n_keep):
    nb, w = score.shape
    n_q = lax.broadcasted_iota(jnp.int32, (nb, w), 0)
    n_f = n_q.astype(F32)
    allowed = n_q <= cb
    forced = allowed & ((n_q == 0) | (n_q == cb) | (n_q == cb - 1))
    val0 = jnp.where(forced, jnp.inf, jnp.where(allowed, score, -jnp.inf))

    def pick(_, val):
        mx = jnp.max(val, axis=0, keepdims=True)
        idx = jnp.min(jnp.where(val == mx, n_f, float(nb)), axis=0, keepdims=True)
        return jnp.where(n_f == idx, -jnp.inf, val)

    val = lax.fori_loop(0, n_keep, pick, val0, unroll=True)
    return jnp.where(allowed & (val == -jnp.inf), 1.0, 0.0)


def _nsa_prompt_kernel(qt_ref, kc_ref, vct_ref, ks_ref, vst_ref, kw_ref, vwt_ref, gt_ref,
                       o_ref, sel_ref, acc_ref, s_ref, st_ref, ocw_ref, *, tq, tk, n_keep):
    i = pl.program_id(1)
    r = NSA_HPG * tq
    t0 = i * tq
    qt = jnp.concatenate([qt_ref[h] for h in range(NSA_HPG)], axis=1)
    t_row = t0 + (lax.broadcasted_iota(jnp.int32, (1, r), 1) & (tq - 1))

    nb = kc_ref.shape[1]
    s = jnp.dot(kc_ref[0], qt, preferred_element_type=F32)
    n_io = lax.broadcasted_iota(jnp.int32, (nb, r), 0)
    p = _masked_softmax_cols(s, n_io < ((t_row + 1) >> 6))
    ocw_ref[0] = jnp.dot(vct_ref[0], p.astype(BF16), preferred_element_type=F32)
    score = p[:, 0:tq]
    for h in range(1, NSA_HPG):
        score = score + p[:, h * tq:(h + 1) * tq]

    cb = (t0 + lax.broadcasted_iota(jnp.int32, (1, tq), 1)) >> 6
    sel = _select_blocks(score, cb, n_keep)
    sel_ref[...] = jnp.concatenate([sel] * NSA_HPG, axis=1)

    n_w = NSA_WINDOW // tq + 1
    k_parts, v_parts = [], []
    for d in range(n_w):
        start = jnp.maximum(t0 - NSA_WINDOW + d * tq, 0)
        k_parts.append(kw_ref[pl.ds(pl.multiple_of(start, tq), tq), :])
        v_parts.append(vwt_ref[0, jnp.maximum(i - (n_w - 1) + d, 0)])
    kcat = jnp.concatenate(k_parts, axis=0)
    sw = jnp.dot(kcat, qt, preferred_element_type=F32)
    spos = t0 - NSA_WINDOW + lax.broadcasted_iota(jnp.int32, (n_w * tq, r), 0)
    dt = t_row - spos
    pw = _masked_softmax_cols(sw, (dt >= 0) & (dt <= NSA_WINDOW) & (spos >= 0))
    ocw_ref[1] = jnp.dot(jnp.concatenate(v_parts, axis=1), pw.astype(BF16), preferred_element_type=F32)

    nbk = tk // NSA_BLOCK
    cw = min(r, _LANE_CHUNK)
    acc_ref[...] = jnp.zeros_like(acc_ref)
    st_ref[...] = jnp.full(st_ref.shape, NEG, F32)

    def tile_scores(j, buf, causal, m_in):
        k = ks_ref[pl.ds(pl.multiple_of(j * tk, tk), tk), :]
        bias = (sel_ref[pl.ds(pl.multiple_of(j * nbk, nbk), nbk), :] - 1.0) * (-NEG)
        for c0 in range(0, r, cw):
            cs = slice(c0, c0 + cw)
            sc = jnp.dot(k, qt[:, cs], preferred_element_type=F32)
            sc = sc + jnp.broadcast_to(bias[:, None, cs], (nbk, NSA_BLOCK, cw)).reshape(tk, cw)
            if causal:
                pos = j * tk + lax.broadcasted_iota(jnp.int32, (tk, cw), 0)
                sc = jnp.where(pos <= t_row[:, cs], sc, NEG)
            s_ref[buf, :, cs] = sc
            st_ref[buf, 0:1, cs] = m_in[:, cs]
            st_ref[buf, 1:2, cs] = jnp.maximum(m_in[:, cs], jnp.max(sc, axis=0, keepdims=True))

    def tile_accumulate(j, buf):
        for c0 in range(0, r, cw):
            cs = slice(c0, c0 + cw)
            m_in, m_out = st_ref[buf, 0:1, cs], st_ref[buf, 1:2, cs]
            alpha = jnp.exp2(m_in - m_out)
            pp = jnp.exp2(s_ref[buf, :, cs] - m_out)
            pv = jnp.dot(vst_ref[0, j], pp.astype(BF16), preferred_element_type=F32)
            acc_ref[:, cs] = acc_ref[:, cs] * alpha + pv

    m_after = lambda buf: st_ref[buf, 1:2, :]
    n_full = t0 // tk
    neg_row = jnp.full((1, r), NEG, F32)

    @pl.when(n_full == 0)
    def _():
        tile_scores(0, 0, True, neg_row)
        tile_accumulate(0, 0)

    @pl.when(n_full > 0)
    def _():
        tile_scores(0, 0, False, neg_row)

    n_pairs = jnp.maximum(n_full - 1, 0) // 2

    def pair(u, carry):
        j = 2 * u
        tile_accumulate(j, 0)
        tile_scores(j + 1, 1, False, m_after(0))
        tile_accumulate(j + 1, 1)
        tile_scores(j + 2, 0, False, m_after(1))
        return carry

    lax.fori_loop(0, n_pairs, pair, 0)
    done = 2 * n_pairs

    @pl.when((n_full > 0) & (n_full - done == 1))
    def _():
        tile_accumulate(done, 0)
        tile_scores(n_full, 1, True, m_after(0))
        tile_accumulate(n_full, 1)

    @pl.when((n_full > 0) & (n_full - done == 2))
    def _():
        tile_accumulate(done, 0)
        tile_scores(done + 1, 1, False, m_after(0))
        tile_accumulate(done + 1, 1)
        tile_scores(n_full, 0, True, m_after(1))
        tile_accumulate(n_full, 0)

    m = jnp.where(n_full % 2 == 1, m_after(1), m_after(0))
    o_slc = jnp.where(m > 0.5 * NEG, acc_ref[0:NSA_DV, :] / acc_ref[NSA_DV:NSA_DV + 1, :], 0.0)

    for h in range(NSA_HPG):
        sl = slice(h * tq, (h + 1) * tq)
        o = (gt_ref[3 * h:3 * h + 1, :] * ocw_ref[0, :, sl] + gt_ref[3 * h + 1:3 * h + 2, :] * o_slc[:, sl]
             + gt_ref[3 * h + 2:3 * h + 3, :] * ocw_ref[1, :, sl])
        o_ref[:, h * NSA_DV:(h + 1) * NSA_DV] = o.T


def _nsa_prompt(qt, kc, vct, ks, vst, kw, vwt, gt, tq, tk):
    t = ks.shape[0]
    nb = t // NSA_BLOCK
    r = NSA_HPG * tq
    resident = dict(pipeline_mode=pl.Buffered(1))
    return pl.pallas_call(
        functools.partial(_nsa_prompt_kernel, tq=tq, tk=tk, n_keep=min(NSA_N_SEL, nb)),
        grid=(NSA_GROUPS, t // tq),
        in_specs=[pl.BlockSpec((NSA_HPG, DKP, tq), lambda g, i: (g, 0, i)),
                  pl.BlockSpec((1, nb, DKP), lambda g, i: (g, 0, 0)),
                  pl.BlockSpec((1, NSA_DV, nb), lambda g, i: (g, 0, 0)),
                  pl.BlockSpec((t, DKP), lambda g, i: (0, g), **resident),
                  pl.BlockSpec((1, t // tk, NSA_DV + ONES_ROWS, tk), lambda g, i: (g, 0, 0, 0), **resident),
                  pl.BlockSpec((t, DKP), lambda g, i: (0, g), **resident),
                  pl.BlockSpec((1, t // tq, NSA_DV, tq), lambda g, i: (g, 0, 0, 0), **resident),
                  pl.BlockSpec((16, tq), lambda g, i: (g, i))],
        out_specs=pl.BlockSpec((tq, NSA_HPG * NSA_DV), lambda g, i: (i, g)),
        out_shape=jax.ShapeDtypeStruct((t, NSA_WIDTH), F32),
        scratch_shapes=[pltpu.VMEM((nb, r), F32), pltpu.VMEM((NSA_DV + ONES_ROWS, r), F32),
                        pltpu.VMEM((2, tk, r), F32), pltpu.VMEM((2, 8, r), F32),
                        pltpu.VMEM((2, NSA_DV, r), F32)],
        compiler_params=_cparams(("arbitrary", "arbitrary")),
        name="nsa_prompt",
    )(qt, kc, vct, ks, vst, kw, vwt, gt)


def _nsa_sample_kernel(pt_ref, *refs, past, ts, n_keep):
    del pt_ref
    npg = _PAGES_PER_STEP
    kp, vp = refs[:npg], refs[npg:2 * npg]
    (qz_ref, qzp_ref, kc_ref, vc_ref, ksn_ref, vsn_ref, kwb_ref, vwb_ref, kwn_ref, vwn_ref, gate_ref,
     o_ref, sel_ref, acc_ref, ml_ref, ocw_ref) = refs[2 * npg:]
    c = pl.program_id(1)
    ng, nl = NSA_GROUPS, 128
    lg = nl // ng
    lane = lax.broadcasted_iota(jnp.int32, (1, nl), 1)
    tt = lane % ts
    t_row = past + tt
    lane_g = lane // lg

    def scores(k_of_g, q_ref, keys_on_lanes=False):
        dims = (((0,), (0,)), ((), ())) if keys_on_lanes else (((1,), (0,)), ((), ()))
        s = lax.dot_general(k_of_g(0), q_ref[0, 0], dims, preferred_element_type=F32)
        for g in range(1, ng):
            s = s + lax.dot_general(k_of_g(g), q_ref[0, g], dims, preferred_element_type=F32)
        return s

    def group_rows(ref, g, n):
        return ref[0, pl.ds(g, n, stride=ng), :]

    def pv(v_of_g, p):
        o = None
        for g in range(ng):
            pg = jnp.where(lane_g == g, p, 0.0).astype(BF16)
            d = lax.dot_general(v_of_g(g), pg, (((0,), (0,)), ((), ())), preferred_element_type=F32)
            o = d if o is None else o + d
        return o

    def pad_rows(x):
        return jnp.concatenate([x, jnp.zeros((16 - ts, x.shape[1]), x.dtype)], axis=0).astype(BF16)

    def update(sc, mask, v_of_g):
        sc = jnp.where(mask, sc, NEG)
        m, l = ml_ref[0:1, :], ml_ref[1:2, :]
        m_new = jnp.maximum(m, jnp.max(sc, axis=0, keepdims=True))
        alpha = jnp.exp2(m - m_new)
        pp = jnp.exp2(sc - m_new)
        ml_ref[0:1, :] = m_new
        ml_ref[1:2, :] = l * alpha + jnp.sum(pp, axis=0, keepdims=True)
        acc_ref[...] = acc_ref[...] * alpha + pv(v_of_g, pp)

    @pl.when(c == 0)
    def _():
        nb = kc_ref.shape[2]
        s = scores(lambda g: kc_ref[0, g], qzp_ref)
        n_io = lax.broadcasted_iota(jnp.int32, (nb, nl), 0)
        p = _masked_softmax_cols(s, n_io < ((t_row + 1) >> 6))
        ocw_ref[0] = pv(lambda g: vc_ref[0, g], p)
        li = lax.broadcasted_iota(jnp.int32, (nl, nl), 0)
        lj = lax.broadcasted_iota(jnp.int32, (nl, nl), 1)
        same = ((li // lg) == (lj // lg)) & ((li % ts) == (lj % ts))
        mm = same.astype(F32).astype(BF16)
        hi = p.astype(BF16)
        r1 = p - hi.astype(F32)
        mid = r1.astype(BF16)
        lo = (r1 - mid.astype(F32)).astype(BF16)
        score = (jnp.dot(hi, mm, preferred_element_type=F32) + jnp.dot(mid, mm, preferred_element_type=F32)
                 + jnp.dot(lo, mm, preferred_element_type=F32))
        sel_ref[...] = _select_blocks(score, t_row >> 6, n_keep)
        w = kwb_ref.shape[4]
        s_buf = scores(lambda g: kwb_ref[0, 0, g].astype(BF16), qz_ref, True)
        s_new = scores(lambda g: pad_rows(kwn_ref[:, g * DKP:(g + 1) * DKP]), qzp_ref)
        sw = jnp.concatenate([s_buf, s_new], axis=0)
        r_io = lax.broadcasted_iota(jnp.int32, (w + 16, nl), 0)
        spos = past - w + r_io
        dt = t_row - spos
        pw = _masked_softmax_cols(sw, (dt >= 0) & (dt <= NSA_WINDOW) & (spos >= 0) & (r_io < w + ts))
        ocw_ref[1] = (pv(lambda g: group_rows(vwb_ref, g, w).astype(BF16), pw[:w])
                      + pv(lambda g: pad_rows(vwn_ref[:, g * NSA_DV:(g + 1) * NSA_DV]), pw[w:]))
        acc_ref[...] = jnp.zeros_like(acc_ref)
        ml_ref[...] = jnp.zeros_like(ml_ref)
        ml_ref[0:1, :] = jnp.full((1, nl), NEG, F32)

    nbk = 2 * npg
    sc = scores(lambda g: jnp.concatenate([kp[k][0, 0, g] for k in range(npg)], axis=1).astype(BF16),
                qz_ref, True)
    selt = sel_ref[pl.ds(pl.multiple_of(c * nbk, nbk), nbk), :]
    mask = jnp.broadcast_to(selt[:, None, :], (nbk, NSA_BLOCK, nl)).reshape(nbk * NSA_BLOCK, nl) > 0.0
    update(sc, mask,
           lambda g: jnp.concatenate([group_rows(vp[k], g, PAGE_SIZE) for k in range(npg)], axis=0).astype(BF16))

    @pl.when(c == pl.num_programs(1) - 1)
    def _():
        sn = scores(lambda g: pad_rows(ksn_ref[:, g * DKP:(g + 1) * DKP]), qzp_ref)
        u = lax.broadcasted_iota(jnp.int32, (16, nl), 0)
        update(sn, (u <= tt) & (u < ts), lambda g: pad_rows(vsn_ref[:, g * NSA_DV:(g + 1) * NSA_DV]))
        m, l = ml_ref[0:1, :], ml_ref[1:2, :]
        o_slc = jnp.where(m > 0.5 * NEG, acc_ref[...] / l, 0.0)
        o_ref[0] = gate_ref[0, 0:1, :] * ocw_ref[0] + gate_ref[0, 1:2, :] * o_slc + gate_ref[0, 2:3, :] * ocw_ref[1]


def _nsa_sample(page_table, cache_k, cache_v, qz, qzp, kc, vc, ksn, vsn, kwb, vwb, kwn, vwn, gates, ts):
    db, n_pages = page_table.shape
    npg = _PAGES_PER_STEP
    steps = n_pages // npg
    past = n_pages * PAGE_SIZE
    nb = past // NSA_BLOCK
    w = kwb.shape[4]
    assert NSA_HPG * ts * NSA_GROUPS == 128 and ts <= 16

    def kpage_spec(k):
        return pl.BlockSpec((1, 1, NSA_GROUPS, NSA_DK, PAGE_SIZE), lambda b, c, pt: (0, pt[b, c * npg + k], 0, 0, 0))

    def vpage_spec(k):
        return pl.BlockSpec((1, PAGE_SIZE * NSA_GROUPS, NSA_DV), lambda b, c, pt: (pt[b, c * npg + k], 0, 0))

    per_seq = lambda *shape: pl.BlockSpec((1,) + shape, lambda b, c, pt: (b,) + (0,) * len(shape))
    rows = lambda width: pl.BlockSpec((ts, width), lambda b, c, pt: (b, 0))
    kwin = pl.BlockSpec((1, 1, NSA_GROUPS, NSA_DK, w), lambda b, c, pt: (0, b, 0, 0, 0))
    vwin = pl.BlockSpec((1, w * NSA_GROUPS, NSA_DV), lambda b, c, pt: (b, 0, 0))
    grid_spec = pltpu.PrefetchScalarGridSpec(
        num_scalar_prefetch=1,
        grid=(db, steps),
        in_specs=([kpage_spec(k) for k in range(npg)] + [vpage_spec(k) for k in range(npg)] + [
            per_seq(NSA_GROUPS, NSA_DK, 128), per_seq(NSA_GROUPS, DKP, 128),
            per_seq(NSA_GROUPS, nb, DKP), per_seq(NSA_GROUPS, nb, NSA_DV),
            rows(NSA_GROUPS * DKP), rows(NSA_GROUPS * NSA_DV),
            kwin, vwin,
            rows(NSA_GROUPS * DKP), rows(NSA_GROUPS * NSA_DV),
            per_seq(8, 128)]),
        out_specs=per_seq(NSA_DV, 128),
        scratch_shapes=[pltpu.VMEM((nb, 128), F32), pltpu.VMEM((NSA_DV, 128), F32),
                        pltpu.VMEM((8, 128), F32), pltpu.VMEM((2, NSA_DV, 128), F32)])
    return pl.pallas_call(
        functools.partial(_nsa_sample_kernel, past=past, ts=ts, n_keep=min(NSA_N_SEL - 1, nb)),
        grid_spec=grid_spec,
        out_shape=jax.ShapeDtypeStruct((db, NSA_DV, 128), F32),
        compiler_params=_cparams(("arbitrary", "arbitrary")),
        name="nsa_sample",
    )(page_table, *([cache_k] * npg), *([cache_v] * npg), qz, qzp, kc, vc, ksn, vsn, kwb, vwb, kwn, vwn, gates)


def _split3(x):
    hi = x.astype(BF16)
    r1 = x - hi.astype(F32)
    mid = r1.astype(BF16)
    lo = (r1 - mid.astype(F32)).astype(BF16)
    return jnp.concatenate([hi, mid, lo], axis=1)


_GLA_HEADS_PER_STEP = 2


def _gla_kernel(q_ref, k_ref, v_ref, a_ref, wa_ref, ba_ref, s0_ref, o_ref, sout_ref, st_ref,
                *, c, n_sub, n_valid):
    ci = pl.program_id(2)
    levels = int(math.log2(c))
    hp = _GLA_HEADS_PER_STEP
    dk, dv = GLA_DK, GLA_DV

    @pl.when(ci == 0)
    def _():
        for hh in range(hp):
            st_ref[hh] = s0_ref[0, hh].T

    t_io = lax.broadcasted_iota(jnp.int32, (c, c), 0)
    s_io = lax.broadcasted_iota(jnp.int32, (c, c), 1)
    row = lax.broadcasted_iota(jnp.int32, (c, 1), 0)
    sels = [s_io <= t_io]
    for lv in range(1, levels + 1):
        sels.append(s_io <= ((t_io >> lv) << lv) + (1 << (lv - 1)) - 1)
    sel_all = jnp.concatenate(sels, axis=0).astype(F32).astype(BF16)
    for u in range(n_sub):
        rows = slice(u * c, (u + 1) * c)
        a_bf = a_ref[rows, :].astype(BF16)
        for hh in range(hp):
            kcols = slice(hh * dk, (hh + 1) * dk)
            vcols = slice(hh * dv, (hh + 1) * dv)
            q = q_ref[rows, kcols] * (GLA_DK ** -0.5)
            k = k_ref[rows, kcols]
            v = v_ref[rows, vcols].astype(BF16)
            z = jnp.dot(a_bf, wa_ref[:, kcols], preferred_element_type=F32) + ba_ref[:, kcols]
            la = (jnp.minimum(z, 0.0) - jnp.log1p(jnp.exp(-jnp.abs(z)))) * (1.0 / GLA_TAU)
            if n_valid < c:
                la = jnp.where(row < n_valid, la, 0.0)
            big = jnp.dot(sel_all, _split3(la), preferred_element_type=F32)
            big = big[:, 0:dk] + big[:, dk:2 * dk] + big[:, 2 * dk:3 * dk]
            b = big[0:c]
            st = st_ref[hh]
            o = lax.dot_general((q * jnp.exp(b)).astype(BF16), st.astype(BF16),
                                (((1,), (1,)), ((), ())), preferred_element_type=F32)
            a_mat = jnp.where(t_io == s_io, jnp.sum(q * k, axis=1, keepdims=True), 0.0)
            for lv in range(1, levels + 1):
                bref = big[lv * c:(lv + 1) * c]
                upper = ((row >> (lv - 1)) & 1) == 1
                ql = jnp.where(upper, q * jnp.exp(b - bref), 0.0).astype(BF16)
                kl = jnp.where(upper, 0.0, k * jnp.exp(bref - b)).astype(BF16)
                al = lax.dot_general(ql, kl, (((1,), (1,)), ((), ())), preferred_element_type=F32)
                a_mat = a_mat + jnp.where((t_io >> lv) == (s_io >> lv), al, 0.0)
            o = o + jnp.dot(a_mat.astype(BF16), v, preferred_element_type=F32)
            o_ref[rows, vcols] = o
            b_last = b[c - 1:c, :]
            kd = (k * jnp.exp(b_last - b)).astype(BF16)
            st_ref[hh] = st * jnp.exp(b_last) + lax.dot_general(
                v, kd, (((0,), (0,)), ((), ())), preferred_element_type=F32)

    @pl.when(ci == pl.num_programs(2) - 1)
    def _():
        for hh in range(hp):
            sout_ref[0, hh] = st_ref[hh].T


def _gla(p, n_batch, t_len, s0, wa_pad, ba, c, n_sub, n_valid):
    step = c * n_sub
    ncs = t_len // step
    hp = _GLA_HEADS_PER_STEP
    qb, kb = _SEG["q_g"] // (hp * GLA_DK), _SEG["k_g"] // (hp * GLA_DK)
    vb, ab = _SEG["v_g"] // (hp * GLA_DV), _SEG["a_g"] // 256
    return pl.pallas_call(
        functools.partial(_gla_kernel, c=c, n_sub=n_sub, n_valid=n_valid),
        grid=(n_batch, GLA_HEADS // hp, ncs),
        in_specs=[pl.BlockSpec((step, hp * GLA_DK), lambda b, h, ci: (b * ncs + ci, qb + h)),
                  pl.BlockSpec((step, hp * GLA_DK), lambda b, h, ci: (b * ncs + ci, kb + h)),
                  pl.BlockSpec((step, hp * GLA_DV), lambda b, h, ci: (b * ncs + ci, vb + h)),
                  pl.BlockSpec((step, 256), lambda b, h, ci: (b * ncs + ci, ab)),
                  pl.BlockSpec((256, hp * GLA_DK), lambda b, h, ci: (0, h)),
                  pl.BlockSpec((1, hp * GLA_DK), lambda b, h, ci: (0, h)),
                  pl.BlockSpec((1, hp, GLA_DK, GLA_DV), lambda b, h, ci: (b, h, 0, 0))],
        out_specs=[pl.BlockSpec((step, hp * GLA_DV), lambda b, h, ci: (b * ncs + ci, h)),
                   pl.BlockSpec((1, hp, GLA_DK, GLA_DV), lambda b, h, ci: (b, h, 0, 0))],
        out_shape=[jax.ShapeDtypeStruct((n_batch * t_len, GLA_WIDTH), F32),
                   jax.ShapeDtypeStruct((n_batch, GLA_HEADS, GLA_DK, GLA_DV), F32)],
        scratch_shapes=[pltpu.VMEM((hp, GLA_DV, GLA_DK), F32)],
        compiler_params=_cparams(("arbitrary", "arbitrary", "arbitrary")),
        name="gla",
    )(p, p, p, p, wa_pad, ba, s0)


def _merge1_kernel(on_ref, zn_ref, og_ref, zg_ref, gg_ref, wa_ref, wb_ref, ma_ref, mb_ref, o_ref,
                   a_ref, b_ref):
    @pl.when(pl.program_id(1) == 0)
    def _():
        a_ref[...] = (on_ref[...] * _silu(zn_ref[...])).astype(BF16)
        og = og_ref[...]
        zg = zg_ref[...]
        for h in range(GLA_HEADS):
            sl = slice(h * GLA_DV, (h + 1) * GLA_DV)
            x = og[:, sl]
            y = x * lax.rsqrt(jnp.mean(x * x, axis=-1, keepdims=True) + NORM_EPS) * gg_ref[...]
            b_ref[:, sl] = (y * _silu(zg[:, sl])).astype(BF16)

    ua = jnp.dot(a_ref[...], wa_ref[...], preferred_element_type=F32)
    ub = jnp.dot(b_ref[...], wb_ref[...], preferred_element_type=F32)
    o_ref[...] = (jax.nn.sigmoid(ma_ref[...]) * ua + jax.nn.sigmoid(mb_ref[...]) * ub).astype(BF16)


def _merge1(o_nsa, o_gla, p, gla_g, wb_a, wb_b, tm):
    m = p.shape[0]
    tn = 512
    zn, zg, mg = _SEG["z_nsa"] // NSA_WIDTH, _SEG["z_g"] // GLA_WIDTH, _SEG["m_gate"] // tn
    row = lambda i, j: (i, 0)
    return pl.pallas_call(
        _merge1_kernel,
        grid=(m // tm, D_MODEL // tn),
        in_specs=[pl.BlockSpec((tm, NSA_WIDTH), row),
                  pl.BlockSpec((tm, NSA_WIDTH), lambda i, j: (i, zn)),
                  pl.BlockSpec((tm, GLA_WIDTH), row),
                  pl.BlockSpec((tm, GLA_WIDTH), lambda i, j: (i, zg)),
                  pl.BlockSpec((1, GLA_DV), lambda i, j: (0, 0)),
                  pl.BlockSpec((NSA_WIDTH, tn), lambda i, j: (0, j)),
                  pl.BlockSpec((GLA_WIDTH, tn), lambda i, j: (0, j)),
                  pl.BlockSpec((tm, tn), lambda i, j: (i, mg + j)),
                  pl.BlockSpec((tm, tn), lambda i, j: (i, mg + D_MODEL // tn + j))],
        out_specs=pl.BlockSpec((tm, tn), lambda i, j: (i, j)),
        out_shape=jax.ShapeDtypeStruct((m, D_MODEL), BF16),
        scratch_shapes=[pltpu.VMEM((tm, NSA_WIDTH), BF16), pltpu.VMEM((tm, GLA_WIDTH), BF16)],
        compiler_params=_cparams(("arbitrary", "arbitrary")),
        name="merge1",
    )(o_nsa, p, o_gla, p, gla_g, wb_a, wb_b, p, p)


def _merge2_kernel(x_ref, gate_ref, mg_ref, w_ref, o_ref):
    o_ref[...] = x_ref[...] + gate_ref[...] * jnp.dot(mg_ref[...], w_ref[...], preferred_element_type=F32)


def _merge2(x, gate, merged, w_out, tm):
    m, d = x.shape
    tn = 512
    per_row = gate.shape[0] != 1
    gate_spec = (pl.BlockSpec((tm, tn), lambda i, j: (i, j)) if per_row
                 else pl.BlockSpec((1, tn), lambda i, j: (0, j)))
    return pl.pallas_call(
        _merge2_kernel,
        grid=(m // tm, d // tn),
        in_specs=[pl.BlockSpec((tm, tn), lambda i, j: (i, j)), gate_spec,
                  pl.BlockSpec((tm, d), lambda i, j: (i, 0)),
                  pl.BlockSpec((d, tn), lambda i, j: (0, j))],
        out_specs=pl.BlockSpec((tm, tn), lambda i, j: (i, j)),
        out_shape=jax.ShapeDtypeStruct((m, d), F32),
        compiler_params=_cparams(("arbitrary", "arbitrary")),
        name="merge2",
    )(x, gate, merged, w_out)


def _pad_last(a, width):
    return jnp.pad(a, [(0, 0)] * (a.ndim - 1) + [(0, width - a.shape[-1])])


def _pad_heads(w, n, src, dst):
    return _pad_last(w.reshape(w.shape[0], n, src), dst).reshape(w.shape[0], n * dst)


def _pad_w_in(w_in):
    sizes = (NSA_HEADS * NSA_DK, NSA_GROUPS * NSA_DK, NSA_GROUPS * NSA_DV, NSA_GROUPS * NSA_DK,
             NSA_GROUPS * NSA_DV, NSA_GROUPS * NSA_DK, NSA_GROUPS * NSA_DV, 3 * NSA_HEADS, NSA_WIDTH,
             GLA_HEADS * GLA_DK, GLA_HEADS * GLA_DK, GLA_WIDTH, GLA_WIDTH, GLA_RANK, 2 * D_MODEL)
    cuts = [0]
    for s in sizes:
        cuts.append(cuts[-1] + s)
    names = ("q", "k_cmp", "v_cmp", "k_slc", "v_slc", "k_win", "v_win", "g_nsa", "z_nsa",
             "q_g", "k_g", "v_g", "z_g", "a_g", "m_gate")
    parts = {nm: w_in[:, cuts[n]:cuts[n + 1]] for n, nm in enumerate(names)}
    parts["q"] = _pad_heads(parts["q"], NSA_HEADS, NSA_DK, DKP)
    for nm in ("k_cmp", "k_slc", "k_win"):
        parts[nm] = _pad_heads(parts[nm], NSA_GROUPS, NSA_DK, DKP)
    parts["g_nsa"] = _pad_last(_pad_heads(parts["g_nsa"], NSA_GROUPS, 3 * NSA_HPG, 16), 256)
    parts["a_g"] = _pad_last(parts["a_g"], 256)
    return jnp.concatenate([parts[nm] for nm in _SEG], axis=1).astype(BF16)


def _unpad_heads(a, n):
    return a.reshape(a.shape[0], n, DKP)[:, :, :NSA_DK]


def kernel(x_prompt, x_sample, cache_k_cmp, cache_v_cmp, cache_k_slc, cache_v_slc, state_k_win, state_v_win, state_gla, page_table, c_prompt, c_sample, ln_g, w_ada, b_ada, w_in, q_norm_g, k_cmp_norm_g, k_slc_norm_g, k_win_norm_g, pe_k, w1_k, w2_k, pe_v, w1_v, w2_v, w_a2, b_a, gla_norm_g, w_branch, w_out):
    assert ln_g.shape[0] == 1, "single layer"
    bp, t, d = x_prompt.shape
    db, ts, _ = x_sample.shape
    assert bp == 1 and d == D_MODEL
    G = NSA_GROUPS

    w_pad = _pad_w_in(w_in[0])
    g_q = _pad_last(q_norm_g, DKP)
    g_kc = _pad_last(k_cmp_norm_g, DKP)
    g_ks = _pad_last(k_slc_norm_g, DKP)
    g_kw = _pad_last(k_win_norm_g, DKP)
    pe_k_p = _pad_last(pe_k[0], DKP)[:, None, :]
    w1_k_p = jnp.pad(w1_k[0], ((0, 0), (0, DKP - NSA_DK), (0, 0))).astype(BF16)
    w2_k_p = _pad_last(w2_k[0], DKP).astype(BF16)
    pe_v_p = pe_v[0][:, None, :]
    w1_v_p = w1_v[0].astype(BF16)
    w2_v_p = w2_v[0].astype(BF16)
    wa_pad = jnp.pad(w_a2[0], ((0, 256 - GLA_RANK), (0, 0))).astype(BF16)
    wb_a = w_branch[0, :NSA_WIDTH].astype(BF16)
    wb_b = w_branch[0, NSA_WIDTH:].astype(BF16)
    w_o = w_out[0].astype(BF16)
    ones_v = jnp.ones((1, NSA_DV), F32)

    n_c = bp + db
    c_all = jnp.pad(jnp.concatenate([c_prompt, c_sample], axis=0), ((0, (-n_c) % 8), (0, 0)))
    mod = _ada(c_all, w_ada[0], b_ada)
    shift, scale, gate = mod[:, :d], mod[:, d:2 * d], mod[:, 2 * d:]

    tm = min(1024, t)
    xp = x_prompt.reshape(t, d)
    pp = _proj(xp, scale[0:1], shift[0:1], ln_g, w_pad, tm)
    tq = min(256, t)
    tk = min(512, t)
    qt = _prep_q(pp, g_q, min(2048, t))
    ks_f, ks_b = _prep_k(pp, "k_slc", g_ks, min(2048, t))
    kw_f, kw_b = _prep_k(pp, "k_win", g_kw, min(2048, t))
    vst = _prep_vt(pp, "v_slc", tk, ONES_ROWS)
    vwt = _prep_vt(pp, "v_win", tq)
    gt = _prep_gate(pp, min(512, t))
    kc = _compress_prompt(pp, "k_cmp", DKP, pe_k_p, w1_k_p, w2_k_p, g_kc, True, False)
    vct = _compress_prompt(pp, "v_cmp", NSA_DV, pe_v_p, w1_v_p, w2_v_p, ones_v, False, True)
    o_nsa_p = _nsa_prompt(qt, kc, vct, ks_b, vst, kw_b, vwt, gt, tq, tk)
    s0_p = jnp.zeros((bp, GLA_HEADS, GLA_DK, GLA_DV), F32)
    o_gla_p, gla_p = _gla(pp, bp, t, s0_p, wa_pad, b_a, GLA_CHUNK, 4, GLA_CHUNK)
    merged_p = _merge1(o_nsa_p, o_gla_p, pp, gla_norm_g, wb_a, wb_b, min(512, t))
    y_p = _merge2(xp, gate[0:1], merged_p, w_o, min(512, t)).reshape(bp, t, d)

    wp = min(NSA_WINDOW, t)
    k_cmp_p = _unpad_heads(pp[:, _SEG["k_cmp"]:_SEG["k_cmp"] + G * DKP], G)
    v_cmp_p = pp[:, _SEG["v_cmp"]:_SEG["v_cmp"] + G * NSA_DV].reshape(t, G, NSA_DV)
    k_slc_p = _unpad_heads(ks_f, G)
    v_slc_p = pp[:, _SEG["v_slc"]:_SEG["v_slc"] + G * NSA_DV].reshape(t, G, NSA_DV)
    k_win_p = _unpad_heads(kw_f, G)[t - wp:]
    v_win_p = pp[t - wp:, _SEG["v_win"]:_SEG["v_win"] + G * NSA_DV].reshape(wp, G, NSA_DV)

    ms = db * ts
    xs = x_sample.reshape(ms, d)
    rep = lambda a: jnp.repeat(a[bp:bp + db], ts, axis=0)
    ps = _proj(xs, rep(scale), rep(shift), ln_g, w_pad, ms)
    seg = lambda name, w: ps[:, _SEG[name]:_SEG[name] + w]
    qt_s = _prep_q(ps, g_q, ms)
    kss_f, _ = _prep_k(ps, "k_slc", g_ks, ms)
    kws_f, _ = _prep_k(ps, "k_win", g_kw, ms)
    gt_s = _prep_gate(ps, ms)
    v_slc_new = seg("v_slc", G * NSA_DV)
    v_win_new = seg("v_win", G * NSA_DV)
    k_cmp_s = _unpad_heads(seg("k_cmp", G * DKP), G).reshape(db, ts, G, NSA_DK)
    v_cmp_s = seg("v_cmp", G * NSA_DV).reshape(db, ts, G, NSA_DV)
    k_slc_s = _unpad_heads(kss_f, G).reshape(db, ts, G, NSA_DK)
    v_slc_s = v_slc_new.reshape(db, ts, G, NSA_DV)
    k_win_s_new = _unpad_heads(kws_f, G).reshape(db, ts, G, NSA_DK)
    v_win_s_new = v_win_new.reshape(db, ts, G, NSA_DV)
    q4 = qt_s.reshape(G, NSA_HPG, DKP, db, ts).transpose(3, 0, 2, 1, 4).reshape(db, G, DKP, NSA_HPG * ts)
    own = (jnp.arange(G)[:, None] == jnp.arange(G)[None, :])[None, :, None, :, None]
    qzp = jnp.where(own, q4[:, :, :, None, :], jnp.zeros((), BF16)).reshape(db, G, DKP, G * NSA_HPG * ts)
    qz = qzp[:, :, :NSA_DK, :]
    gates = gt_s.reshape(G, 16, db, ts)[:, :3 * NSA_HPG].reshape(G, NSA_HPG, 3, db, ts)
    gates = gates.transpose(3, 2, 0, 1, 4).reshape(db, 3, G * NSA_HPG * ts)
    gates = jnp.pad(gates, ((0, 0), (0, 5), (0, 0)))
    keys_t = lambda a: jnp.transpose(a, (0, 1, 3, 4, 2))
    vals_2d = lambda a: a.reshape(a.shape[1], a.shape[2] * G, NSA_DV)
    kc_s = _compress_paged(keys_t(cache_k_cmp), page_table, pe_k_p[:, 0, :], w1_k_p, w2_k_p, g_kc, True)
    vc_s = _compress_paged(vals_2d(cache_v_cmp), page_table, pe_v[0], w1_v_p, w2_v_p, ones_v, False)
    o_t = _nsa_sample(page_table, keys_t(cache_k_slc), vals_2d(cache_v_slc), qz, qzp, kc_s, vc_s, kss_f,
                      v_slc_new, keys_t(state_k_win), vals_2d(state_v_win), kws_f, v_win_new, gates, ts)
    o_nsa_s = o_t.reshape(db, NSA_DV, G, NSA_HPG, ts).transpose(0, 4, 2, 3, 1)
    kw_s = jnp.concatenate([state_k_win[0][:, ts:], k_win_s_new], axis=1)
    vw_s = jnp.concatenate([state_v_win[0][:, ts:], v_win_s_new], axis=1)
    c_s = 16
    ps_pad = jnp.pad(ps.reshape(db, ts, D_PAD), ((0, 0), (0, c_s - ts), (0, 0))).reshape(db * c_s, D_PAD)
    o_gla_s, gla_s = _gla(ps_pad, db, c_s, state_gla[0], wa_pad, b_a, c_s, 1, ts)
    o_gla_s = o_gla_s.reshape(db, c_s, GLA_WIDTH)[:, :ts].reshape(ms, GLA_WIDTH)
    merged_s = _merge1(o_nsa_s.reshape(ms, NSA_WIDTH), o_gla_s, ps, gla_norm_g, wb_a, wb_b, ms)
    y_s = _merge2(xs, rep(gate), merged_s, w_o, ms).reshape(db, ts, d)

    L1 = lambda a: a[None, None]
    return (y_p, y_s,
            L1(k_cmp_p), L1(v_cmp_p), L1(k_slc_p), L1(v_slc_p), L1(k_win_p), L1(v_win_p), gla_p[None],
            k_cmp_s[None], v_cmp_s[None], k_slc_s[None], v_slc_s[None], kw_s[None], vw_s[None], gla_s[None])
```

```python
import functools
import math

import jax
import jax.numpy as jnp
from jax import lax
from jax.experimental import pallas as pl
from jax.experimental.pallas import tpu as pltpu

F32 = jnp.float32
BF16 = jnp.bfloat16

NSA_HEADS = 16
NSA_GROUPS = 4
NSA_HPG = NSA_HEADS // NSA_GROUPS
NSA_DK = 192
NSA_DV = 128
NSA_BLOCK = 64
NSA_N_SEL = 16
NSA_WINDOW = 512
NSA_CMP_HIDDEN = 256
PAGE_SIZE = 128
GLA_HEADS = 4
GLA_DK = 256
GLA_DV = 512
GLA_RANK = 16
GLA_TAU = 16.0
GLA_CHUNK = 64
NORM_EPS = 1e-6
D_MODEL = 2048
NSA_WIDTH = NSA_HEADS * NSA_DV
GLA_WIDTH = GLA_HEADS * GLA_DV

DKP = 256
LOG2E = 1.4426950408889634
NEG = -1e30
Q_SCALE = (NSA_DK ** -0.5) * LOG2E
VMEM_LIMIT = 56 * 1024 * 1024

_SEG = {}
_off = 0
for _name, _w in (("z_nsa", NSA_WIDTH), ("z_g", GLA_WIDTH), ("v_g", GLA_WIDTH), ("m_gate", 2 * D_MODEL),
                  ("q", NSA_HEADS * DKP), ("k_cmp", NSA_GROUPS * DKP), ("v_cmp", NSA_GROUPS * NSA_DV),
                  ("k_slc", NSA_GROUPS * DKP), ("v_slc", NSA_GROUPS * NSA_DV),
                  ("k_win", NSA_GROUPS * DKP), ("v_win", NSA_GROUPS * NSA_DV),
                  ("q_g", GLA_HEADS * GLA_DK), ("k_g", GLA_HEADS * GLA_DK),
                  ("g_nsa", 256), ("a_g", 256)):
    _SEG[_name] = _off
    _off += _w
D_PAD = _off


def _cparams(sem):
    return pltpu.CompilerParams(dimension_semantics=sem, vmem_limit_bytes=VMEM_LIMIT)


def _silu(x):
    return x * jax.nn.sigmoid(x)


def _ada_kernel(c_ref, w_ref, b_ref, o_ref):
    a = _silu(c_ref[...]).astype(BF16)
    o_ref[...] = jnp.dot(a, w_ref[...].astype(BF16), preferred_element_type=F32) + b_ref[...]


def _ada(c_all, w_ada, b_ada):
    m, d = c_all.shape
    n = w_ada.shape[1]
    tn = 768
    return pl.pallas_call(
        _ada_kernel,
        grid=(n // tn,),
        in_specs=[pl.BlockSpec((m, d), lambda j: (0, 0)),
                  pl.BlockSpec((d, tn), lambda j: (0, j)),
                  pl.BlockSpec((1, tn), lambda j: (0, j))],
        out_specs=pl.BlockSpec((m, tn), lambda j: (0, j)),
        out_shape=jax.ShapeDtypeStruct((m, n), F32),
        compiler_params=_cparams(("arbitrary",)),
        name="ada",
    )(c_all, w_ada, b_ada)


def _proj_kernel(x_ref, sc_ref, sh_ref, g_ref, w_ref, o_ref, h_ref):
    @pl.when(pl.program_id(1) == 0)
    def _():
        x = x_ref[...]
        y = x * lax.rsqrt(jnp.mean(x * x, axis=-1, keepdims=True) + NORM_EPS) * g_ref[...]
        h_ref[...] = (y * (1.0 + sc_ref[...]) + sh_ref[...]).astype(BF16)

    o_ref[...] = jnp.dot(h_ref[...], w_ref[...], preferred_element_type=F32)


def _proj(x, scale, shift, ln_g, w_pad, tm):
    m, d = x.shape
    n = w_pad.shape[1]
    tn = 1536
    per_row = scale.shape[0] != 1
    mod_spec = (pl.BlockSpec((tm, d), lambda i, j: (i, 0)) if per_row
                else pl.BlockSpec((1, d), lambda i, j: (0, 0)))
    return pl.pallas_call(
        _proj_kernel,
        grid=(m // tm, n // tn),
        in_specs=[pl.BlockSpec((tm, d), lambda i, j: (i, 0)), mod_spec, mod_spec,
                  pl.BlockSpec((1, d), lambda i, j: (0, 0)),
                  pl.BlockSpec((d, tn), lambda i, j: (0, j))],
        out_specs=pl.BlockSpec((tm, tn), lambda i, j: (i, j)),
        out_shape=jax.ShapeDtypeStruct((m, n), F32),
        scratch_shapes=[pltpu.VMEM((tm, d), BF16)],
        compiler_params=_cparams(("arbitrary", "arbitrary")),
        name="proj",
    )(x, scale, shift, ln_g, w_pad)


def _head_norm(x, g):
    ms = jnp.sum(x * x, axis=-1, keepdims=True) * (1.0 / NSA_DK)
    return x * lax.rsqrt(ms + NORM_EPS) * g


def _prep_q_kernel(x_ref, g_ref, o_ref):
    o_ref[0] = (_head_norm(x_ref[...], g_ref[...]) * Q_SCALE).T.astype(BF16)


def _prep_q(p, g_pad, tm):
    t = p.shape[0]
    base = _SEG["q"] // DKP
    return pl.pallas_call(
        _prep_q_kernel,
        grid=(t // tm, NSA_HEADS),
        in_specs=[pl.BlockSpec((tm, DKP), lambda i, h: (i, base + h)),
                  pl.BlockSpec((1, DKP), lambda i, h: (0, 0))],
        out_specs=pl.BlockSpec((1, DKP, tm), lambda i, h: (h, 0, i)),
        out_shape=jax.ShapeDtypeStruct((NSA_HEADS, DKP, t), BF16),
        compiler_params=_cparams(("arbitrary", "arbitrary")),
        name="prep_q",
    )(p, g_pad)


def _prep_k_kernel(x_ref, g_ref, o_ref, ob_ref):
    y = _head_norm(x_ref[...], g_ref[...])
    o_ref[...] = y
    ob_ref[...] = y.astype(BF16)


def _prep_k(p, seg, g_pad, tm):
    t = p.shape[0]
    base = _SEG[seg] // DKP
    n = NSA_GROUPS * DKP
    return pl.pallas_call(
        _prep_k_kernel,
        grid=(t // tm, NSA_GROUPS),
        in_specs=[pl.BlockSpec((tm, DKP), lambda i, g: (i, base + g)),
                  pl.BlockSpec((1, DKP), lambda i, g: (0, 0))],
        out_specs=[pl.BlockSpec((tm, DKP), lambda i, g: (i, g)),
                   pl.BlockSpec((tm, DKP), lambda i, g: (i, g))],
        out_shape=[jax.ShapeDtypeStruct((t, n), F32), jax.ShapeDtypeStruct((t, n), BF16)],
        compiler_params=_cparams(("arbitrary", "arbitrary")),
        name="prep_k_" + seg,
    )(p, g_pad)


ONES_ROWS = 16


def _prep_vt_kernel(x_ref, o_ref, *, ones_rows):
    xt = x_ref[...].T
    if ones_rows:
        xt = jnp.concatenate([xt, jnp.ones((ones_rows, xt.shape[1]), F32)], axis=0)
    o_ref[0, 0] = xt.astype(BF16)


def _prep_vt(p, seg, tile, ones_rows=0):
    t = p.shape[0]
    base = _SEG[seg] // NSA_DV
    rows = NSA_DV + ones_rows
    return pl.pallas_call(
        functools.partial(_prep_vt_kernel, ones_rows=ones_rows),
        grid=(t // tile, NSA_GROUPS),
        in_specs=[pl.BlockSpec((tile, NSA_DV), lambda i, g: (i, base + g))],
        out_specs=pl.BlockSpec((1, 1, rows, tile), lambda i, g: (g, i, 0, 0)),
        out_shape=jax.ShapeDtypeStruct((NSA_GROUPS, t // tile, rows, tile), BF16),
        compiler_params=_cparams(("arbitrary", "arbitrary")),
        name="prep_vt_" + seg,
    )(p)


def _prep_gate_kernel(x_ref, o_ref):
    o_ref[...] = jax.nn.sigmoid(x_ref[...]).T[:4 * NSA_HEADS, :]


def _prep_gate(p, tm):
    t = p.shape[0]
    base = _SEG["g_nsa"] // 256
    return pl.pallas_call(
        _prep_gate_kernel,
        grid=(t // tm,),
        in_specs=[pl.BlockSpec((tm, 256), lambda i: (i, base))],
        out_specs=pl.BlockSpec((4 * NSA_HEADS, tm), lambda i: (0, i)),
        out_shape=jax.ShapeDtypeStruct((4 * NSA_HEADS, t), F32),
        compiler_params=_cparams(("arbitrary",)),
        name="prep_gate",
    )(p)


_CMP_ROWS = 8


def _cmp_kernel(x_ref, pe_ref, w1_ref, w2_ref, g_ref, o_ref, acc_ref, *, norm, transpose_out):
    s = pl.program_id(1)

    @pl.when(s == 0)
    def _():
        acc_ref[...] = jnp.zeros_like(acc_ref)

    acc = acc_ref[...]
    for r in range(_CMP_ROWS):
        xb = (x_ref[:, r, :] + pe_ref[r]).astype(BF16)
        acc = acc + jnp.dot(xb, w1_ref[r], preferred_element_type=F32)
    acc_ref[...] = acc

    @pl.when(s == pl.num_programs(1) - 1)
    def _():
        hid = _silu(acc_ref[...]).astype(BF16)
        y = jnp.dot(hid, w2_ref[...], preferred_element_type=F32)
        if norm:
            y = _head_norm(y, g_ref[...])
        if transpose_out:
            y = y.T
        o_ref[0] = y.astype(o_ref.dtype)


def _compress_prompt(p, seg, dp, pe_pad, w1_pad, w2_pad, g_pad, norm, transpose_out):
    t = p.shape[0]
    nb = t // NSA_BLOCK
    pv = p.reshape(nb, NSA_BLOCK, D_PAD)
    base = _SEG[seg] // dp
    out_shape = (NSA_GROUPS, dp, nb) if transpose_out else (NSA_GROUPS, nb, dp)
    return pl.pallas_call(
        functools.partial(_cmp_kernel, norm=norm, transpose_out=transpose_out),
        grid=(NSA_GROUPS, NSA_BLOCK // _CMP_ROWS),
        in_specs=[pl.BlockSpec((nb, _CMP_ROWS, dp), lambda g, s: (0, s, base + g)),
                  pl.BlockSpec((_CMP_ROWS, 1, dp), lambda g, s: (s, 0, 0)),
                  pl.BlockSpec((_CMP_ROWS, dp, NSA_CMP_HIDDEN), lambda g, s: (s, 0, 0)),
                  pl.BlockSpec((NSA_CMP_HIDDEN, dp), lambda g, s: (0, 0)),
                  pl.BlockSpec((1, dp), lambda g, s: (0, 0))],
        out_specs=pl.BlockSpec((1,) + out_shape[1:], lambda g, s: (g, 0, 0)),
        out_shape=jax.ShapeDtypeStruct(out_shape, BF16),
        scratch_shapes=[pltpu.VMEM((nb, NSA_CMP_HIDDEN), F32)],
        compiler_params=_cparams(("arbitrary", "arbitrary")),
        name="cmp_" + seg,
    )(pv, pe_pad, w1_pad, w2_pad, g_pad)


_PAGES_PER_STEP = 16
_PHASE_BLOCKS = 128
_STAGE_PITCH = 72


def _cmp_paged_kernel(pt_ref, *refs, dh, keys_on_lanes, norm):
    del pt_ref
    npg = _PAGES_PER_STEP
    pages = refs[:npg]
    pe_ref, w1_ref, w2_ref, g_ref, o_ref, stage_ref, xs_ref = refs[npg:]
    c = pl.program_id(2)
    nbc = 2 * npg
    n_slab = stage_ref.shape[1]

    @pl.when((pl.program_id(0) == 0) & (pl.program_id(1) == 0) & (c == 0))
    def _():
        stage_ref[...] = jnp.zeros_like(stage_ref)

    for k in range(npg):
        for g in range(NSA_GROUPS):
            if keys_on_lanes:
                x = pages[k][0, 0, g].T
            else:
                x = pages[k][0, pl.ds(g, PAGE_SIZE, stride=NSA_GROUPS), :]
            for blk in range(2):
                r0 = (2 * k + blk) * _STAGE_PITCH
                xb = x[blk * NSA_BLOCK:(blk + 1) * NSA_BLOCK]
                stage_ref[g, 0, r0:r0 + NSA_BLOCK, :] = xb[:, 0:128]
                if n_slab == 2:
                    stage_ref[g, 1, r0:r0 + NSA_BLOCK, 0:dh - 128] = xb[:, 128:dh]
    dp = pe_ref.shape[1]
    row0 = pl.multiple_of(c * nbc, nbc)
    for s in range(NSA_BLOCK):
        for g in range(NSA_GROUPS):
            parts = [stage_ref[g, j, pl.ds(s, nbc, stride=_STAGE_PITCH), :] for j in range(n_slab)]
            x = parts[0] if n_slab == 1 else jnp.concatenate(parts, axis=1)
            xs_ref[g, pl.ds(row0, nbc), s * dp:(s + 1) * dp] = (x + pe_ref[s:s + 1, :]).astype(BF16)

    @pl.when(c == pl.num_programs(2) - 1)
    def _():
        for g in range(NSA_GROUPS):
            acc = jnp.dot(xs_ref[g], w1_ref[...], preferred_element_type=F32)
            hid = _silu(acc).astype(BF16)
            y = jnp.dot(hid, w2_ref[...], preferred_element_type=F32)
            if norm:
                y = _head_norm(y, g_ref[...])
            o_ref[0, g] = y.astype(BF16)


def _compress_paged(cache, page_table, pe_pad, w1_pad, w2_pad, g_pad, norm):
    keys_on_lanes = cache.ndim == 5
    dh = cache.shape[3] if keys_on_lanes else cache.shape[2]
    dp = pe_pad.shape[-1]
    db, n_pages = page_table.shape
    npg = _PAGES_PER_STEP
    steps = n_pages // npg
    spp = min(_PHASE_BLOCKS // (2 * npg), steps)
    m_blocks = spp * 2 * npg
    n_ph = steps // spp
    n_slab = dp // 128

    def page_spec(k):
        page = lambda b, ph, c, pt: pt[b, (ph * spp + c) * npg + k]
        if keys_on_lanes:
            return pl.BlockSpec((1, 1, NSA_GROUPS, dh, PAGE_SIZE), lambda b, ph, c, pt: (0, page(b, ph, c, pt), 0, 0, 0))
        return pl.BlockSpec((1, PAGE_SIZE * NSA_GROUPS, dh), lambda b, ph, c, pt: (page(b, ph, c, pt), 0, 0))

    const = lambda *shape: pl.BlockSpec(shape, lambda b, ph, c, pt: (0,) * len(shape))
    grid_spec = pltpu.PrefetchScalarGridSpec(
        num_scalar_prefetch=1,
        grid=(db, n_ph, spp),
        in_specs=[page_spec(k) for k in range(npg)] + [
            const(NSA_BLOCK, dp),
            pl.BlockSpec((NSA_BLOCK * dp, NSA_CMP_HIDDEN), lambda b, ph, c, pt: (0, 0),
                         pipeline_mode=pl.Buffered(1)),
            const(NSA_CMP_HIDDEN, dp), const(1, dp)],
        out_specs=pl.BlockSpec((1, NSA_GROUPS, m_blocks, dp), lambda b, ph, c, pt: (b, 0, ph, 0)),
        scratch_shapes=[pltpu.VMEM((NSA_GROUPS, n_slab, 2 * npg * _STAGE_PITCH, 128), F32),
                        pltpu.VMEM((NSA_GROUPS, m_blocks, NSA_BLOCK * dp), BF16)])
    return pl.pallas_call(
        functools.partial(_cmp_paged_kernel, dh=dh, keys_on_lanes=keys_on_lanes, norm=norm),
        grid_spec=grid_spec,
        out_shape=jax.ShapeDtypeStruct((db, NSA_GROUPS, n_pages * 2, dp), BF16),
        compiler_params=_cparams(("arbitrary", "arbitrary", "arbitrary")),
        name="cmp_paged_%d" % dh,
    )(page_table, *([cache] * npg), pe_pad, w1_pad.reshape(NSA_BLOCK * dp, NSA_CMP_HIDDEN), w2_pad, g_pad)


def _masked_softmax_cols(s, valid):
    s = jnp.where(valid, s, NEG)
    m = jnp.max(s, axis=0, keepdims=True)
    e = jnp.where(valid, jnp.exp2(s - m), 0.0)
    return e / jnp.maximum(jnp.sum(e, axis=0, keepdims=True), 1e-30)


_LANE_CHUNK = 256


def _select_blocks(score, cb, n_keep):
    nb, w = score.shape
    n_q = lax.broadcasted_iota(jnp.int32, (nb, w), 0)
    n_f = n_q.astype(F32)
    allowed = n_q <= cb
    forced = allowed & ((n_q == 0) | (n_q == cb) | (n_q == cb - 1))
    val0 = jnp.where(forced, jnp.inf, jnp.where(allowed, score, -jnp.inf))

    def pick(_, val):
        mx = jnp.max(val, axis=0, keepdims=True)
        idx = jnp.min(jnp.where(val == mx, n_f, float(nb)), axis=0, keepdims=True)
        return jnp.where(n_f == idx, -jnp.inf, val)

    val = lax.fori_loop(0, n_keep, pick, val0, unroll=True)
    return jnp.where(allowed & (val == -jnp.inf), 1.0, 0.0)


def _nsa_prompt_kernel(qt_ref, kc_ref, vct_ref, ks_ref, vst_ref, kw_ref, vwt_ref, gt_ref,
                       o_ref, sel_ref, acc_ref, s_ref, st_ref, ocw_ref, *, tq, tk, n_keep):
    i = pl.program_id(1)
    r = NSA_HPG * tq
    t0 = i * tq
    qt = jnp.concatenate([qt_ref[h] for h in range(NSA_HPG)], axis=1)
    t_row = t0 + (lax.broadcasted_iota(jnp.int32, (1, r), 1) & (tq - 1))

    nb = kc_ref.shape[1]
    s = jnp.dot(kc_ref[0], qt, preferred_element_type=F32)
    n_io = lax.broadcasted_iota(jnp.int32, (nb, r), 0)
    p = _masked_softmax_cols(s, n_io < ((t_row + 1) >> 6))
    ocw_ref[0] = jnp.dot(vct_ref[0], p.astype(BF16), preferred_element_type=F32)
    score = p[:, 0:tq]
    for h in range(1, NSA_HPG):
        score = score + p[:, h * tq:(h + 1) * tq]

    cb = (t0 + lax.broadcasted_iota(jnp.int32, (1, tq), 1)) >> 6
    sel = _select_blocks(score, cb, n_keep)
    sel_ref[...] = jnp.concatenate([sel] * NSA_HPG, axis=1)

    n_w = NSA_WINDOW // tq + 1
    k_parts, v_parts = [], []
    for d in range(n_w):
        start = jnp.maximum(t0 - NSA_WINDOW + d * tq, 0)
        k_parts.append(kw_ref[pl.ds(pl.multiple_of(start, tq), tq), :])
        v_parts.append(vwt_ref[0, jnp.maximum(i - (n_w - 1) + d, 0)])
    kcat = jnp.concatenate(k_parts, axis=0)
    sw = jnp.dot(kcat, qt, preferred_element_type=F32)
    spos = t0 - NSA_WINDOW + lax.broadcasted_iota(jnp.int32, (n_w * tq, r), 0)
    dt = t_row - spos
    pw = _masked_softmax_cols(sw, (dt >= 0) & (dt <= NSA_WINDOW) & (spos >= 0))
    ocw_ref[1] = jnp.dot(jnp.concatenate(v_parts, axis=1), pw.astype(BF16), preferred_element_type=F32)

    nbk = tk // NSA_BLOCK
    cw = min(r, _LANE_CHUNK)
    acc_ref[...] = jnp.zeros_like(acc_ref)
    st_ref[...] = jnp.full(st_ref.shape, NEG, F32)

    def tile_scores(j, buf, causal, m_in):
        k = ks_ref[pl.ds(pl.multiple_of(j * tk, tk), tk), :]
        bias = (sel_ref[pl.ds(pl.multiple_of(j * nbk, nbk), nbk), :] - 1.0) * (-NEG)
        for c0 in range(0, r, cw):
            cs = slice(c0, c0 + cw)
            sc = jnp.dot(k, qt[:, cs], preferred_element_type=F32)
            sc = sc + jnp.broadcast_to(bias[:, None, cs], (nbk, NSA_BLOCK, cw)).reshape(tk, cw)
            if causal:
                pos = j * tk + lax.broadcasted_iota(jnp.int32, (tk, cw), 0)
                sc = jnp.where(pos <= t_row[:, cs], sc, NEG)
            s_ref[buf, :, cs] = sc
            st_ref[buf, 0:1, cs] = m_in[:, cs]
            st_ref[buf, 1:2, cs] = jnp.maximum(m_in[:, cs], jnp.max(sc, axis=0, keepdims=True))

    def tile_accumulate(j, buf):
        for c0 in range(0, r, cw):
            cs = slice(c0, c0 + cw)
            m_in, m_out = st_ref[buf, 0:1, cs], st_ref[buf, 1:2, cs]
            alpha = jnp.exp2(m_in - m_out)
            pp = jnp.exp2(s_ref[buf, :, cs] - m_out)
            pv = jnp.dot(vst_ref[0, j], pp.astype(BF16), preferred_element_type=F32)
            acc_ref[:, cs] = acc_ref[:, cs] * alpha + pv

    m_after = lambda buf: st_ref[buf, 1:2, :]
    n_full = t0 // tk
    neg_row = jnp.full((1, r), NEG, F32)

    @pl.when(n_full == 0)
    def _():
        tile_scores(0, 0, True, neg_row)
        tile_accumulate(0, 0)

    @pl.when(n_full > 0)
    def _():
        tile_scores(0, 0, False, neg_row)

    n_pairs = jnp.maximum(n_full - 1, 0) // 2

    def pair(u, carry):
        j = 2 * u
        tile_accumulate(j, 0)
        tile_scores(j + 1, 1, False, m_after(0))
        tile_accumulate(j + 1, 1)
        tile_scores(j + 2, 0, False, m_after(1))
        return carry

    lax.fori_loop(0, n_pairs, pair, 0)
    done = 2 * n_pairs

    @pl.when((n_full > 0) & (n_full - done == 1))
    def _():
        tile_accumulate(done, 0)
        tile_scores(n_full, 1, True, m_after(0))
        tile_accumulate(n_full, 1)

    @pl.when((n_full > 0) & (n_full - done == 2))
    def _():
        tile_accumulate(done, 0)
        tile_scores(done + 1, 1, False, m_after(0))
        tile_accumulate(done + 1, 1)
        tile_scores(n_full, 0, True, m_after(1))
        tile_accumulate(n_full, 0)

    m = jnp.where(n_full % 2 == 1, m_after(1), m_after(0))
    o_slc = jnp.where(m > 0.5 * NEG, acc_ref[0:NSA_DV, :] / acc_ref[NSA_DV:NSA_DV + 1, :], 0.0)

    for h in range(NSA_HPG):
        sl = slice(h * tq, (h + 1) * tq)
        o = (gt_ref[3 * h:3 * h + 1, :] * ocw_ref[0, :, sl] + gt_ref[3 * h + 1:3 * h + 2, :] * o_slc[:, sl]
             + gt_ref[3 * h + 2:3 * h + 3, :] * ocw_ref[1, :, sl])
        o_ref[:, h * NSA_DV:(h + 1) * NSA_DV] = o.T


def _nsa_prompt(qt, kc, vct, ks, vst, kw, vwt, gt, tq, tk):
    t = ks.shape[0]
    nb = t // NSA_BLOCK
    r = NSA_HPG * tq
    resident = dict(pipeline_mode=pl.Buffered(1))
    return pl.pallas_call(
        functools.partial(_nsa_prompt_kernel, tq=tq, tk=tk, n_keep=min(NSA_N_SEL, nb)),
        grid=(NSA_GROUPS, t // tq),
        in_specs=[pl.BlockSpec((NSA_HPG, DKP, tq), lambda g, i: (g, 0, i)),
                  pl.BlockSpec((1, nb, DKP), lambda g, i: (g, 0, 0)),
                  pl.BlockSpec((1, NSA_DV, nb), lambda g, i: (g, 0, 0)),
                  pl.BlockSpec((t, DKP), lambda g, i: (0, g), **resident),
                  pl.BlockSpec((1, t // tk, NSA_DV + ONES_ROWS, tk), lambda g, i: (g, 0, 0, 0), **resident),
                  pl.BlockSpec((t, DKP), lambda g, i: (0, g), **resident),
                  pl.BlockSpec((1, t // tq, NSA_DV, tq), lambda g, i: (g, 0, 0, 0), **resident),
                  pl.BlockSpec((16, tq), lambda g, i: (g, i))],
        out_specs=pl.BlockSpec((tq, NSA_HPG * NSA_DV), lambda g, i: (i, g)),
        out_shape=jax.ShapeDtypeStruct((t, NSA_WIDTH), F32),
        scratch_shapes=[pltpu.VMEM((nb, r), F32), pltpu.VMEM((NSA_DV + ONES_ROWS, r), F32),
                        pltpu.VMEM((2, tk, r), F32), pltpu.VMEM((2, 8, r), F32),
                        pltpu.VMEM((2, NSA_DV, r), F32)],
        compiler_params=_cparams(("arbitrary", "arbitrary")),
        name="nsa_prompt",
    )(qt, kc, vct, ks, vst, kw, vwt, gt)


def _nsa_sample_kernel(pt_ref, *refs, past, ts, n_keep):
    del pt_ref
    npg = _PAGES_PER_STEP
    kp, vp = refs[:npg], refs[npg:2 * npg]
    (qz_ref, qzp_ref, kc_ref, vc_ref, ksn_ref, vsn_ref, kwb_ref, vwb_ref, kwn_ref, vwn_ref, gate_ref,
     o_ref, sel_ref, acc_ref, ml_ref, ocw_ref) = refs[2 * npg:]
    c = pl.program_id(1)
    ng, nl = NSA_GROUPS, 128
    lg = nl // ng
    lane = lax.broadcasted_iota(jnp.int32, (1, nl), 1)
    tt = lane % ts
    t_row = past + tt
    lane_g = lane // lg

    def scores(k_of_g, q_ref, keys_on_lanes=False):
        dims = (((0,), (0,)), ((), ())) if keys_on_lanes else (((1,), (0,)), ((), ()))
        s = lax.dot_general(k_of_g(0), q_ref[0, 0], dims, preferred_element_type=F32)
        for g in range(1, ng):
            s = s + lax.dot_general(k_of_g(g), q_ref[0, g], dims, preferred_element_type=F32)
        return s

    def group_rows(ref, g, n):
        return ref[0, pl.ds(g, n, stride=ng), :]

    def pv(v_of_g, p):
        o = None
        for g in range(ng):
            pg = jnp.where(lane_g == g, p, 0.0).astype(BF16)
            d = lax.dot_general(v_of_g(g), pg, (((0,), (0,)), ((), ())), preferred_element_type=F32)
            o = d if o is None else o + d
        return o

    def pad_rows(x):
        return jnp.concatenate([x, jnp.zeros((16 - ts, x.shape[1]), x.dtype)], axis=0).astype(BF16)

    def update(sc, mask, v_of_g):
        sc = jnp.where(mask, sc, NEG)
        m, l = ml_ref[0:1, :], ml_ref[1:2, :]
        m_new = jnp.maximum(m, jnp.max(sc, axis=0, keepdims=True))
        alpha = jnp.exp2(m - m_new)
        pp = jnp.exp2(sc - m_new)
        ml_ref[0:1, :] = m_new
        ml_ref[1:2, :] = l * alpha + jnp.sum(pp, axis=0, keepdims=True)
        acc_ref[...] = acc_ref[...] * alpha + pv(v_of_g, pp)

    @pl.when(c == 0)
    def _():
        nb = kc_ref.shape[2]
        s = scores(lambda g: kc_ref[0, g], qzp_ref)
        n_io = lax.broadcasted_iota(jnp.int32, (nb, nl), 0)
        p = _masked_softmax_cols(s, n_io < ((t_row + 1) >> 6))
        ocw_ref[0] = pv(lambda g: vc_ref[0, g], p)
        li = lax.broadcasted_iota(jnp.int32, (nl, nl), 0)
        lj = lax.broadcasted_iota(jnp.int32, (nl, nl), 1)
        same = ((li // lg) == (lj // lg)) & ((li % ts) == (lj % ts))
        mm = same.astype(F32).astype(BF16)
        hi = p.astype(BF16)
        r1 = p - hi.astype(F32)
        mid = r1.astype(BF16)
        lo = (r1 - mid.astype(F32)).astype(BF16)
        score = (jnp.dot(hi, mm, preferred_element_type=F32) + jnp.dot(mid, mm, preferred_element_type=F32)
                 + jnp.dot(lo, mm, preferred_element_type=F32))
        sel_ref[...] = _select_blocks(score, t_row >> 6, n_keep)
        w = kwb_ref.shape[4]
        s_buf = scores(lambda g: kwb_ref[0, 0, g].astype(BF16), qz_ref, True)
        s_new = scores(lambda g: pad_rows(kwn_ref[:, g * DKP:(g + 1) * DKP]), qzp_ref)
        sw = jnp.concatenate([s_buf, s_new], axis=0)
        r_io = lax.broadcasted_iota(jnp.int32, (w + 16, nl), 0)
        spos = past - w + r_io
        dt = t_row - spos
        pw = _masked_softmax_cols(sw, (dt >= 0) & (dt <= NSA_WINDOW) & (spos >= 0) & (r_io < w + ts))
        ocw_ref[1] = (pv(lambda g: group_rows(vwb_ref, g, w).astype(BF16), pw[:w])
                      + pv(lambda g: pad_rows(vwn_ref[:, g * NSA_DV:(g + 1) * NSA_DV]), pw[w:]))
        acc_ref[...] = jnp.zeros_like(acc_ref)
        ml_ref[...] = jnp.zeros_like(ml_ref)
        ml_ref[0:1, :] = jnp.full((1, nl), NEG, F32)

    nbk = 2 * npg
    sc = scores(lambda g: jnp.concatenate([kp[k][0, 0, g] for k in range(npg)], axis=1).astype(BF16),
                qz_ref, True)
    selt = sel_ref[pl.ds(pl.multiple_of(c * nbk, nbk), nbk), :]
    mask = jnp.broadcast_to(selt[:, None, :], (nbk, NSA_BLOCK, nl)).reshape(nbk * NSA_BLOCK, nl) > 0.0
    update(sc, mask,
           lambda g: jnp.concatenate([group_rows(vp[k], g, PAGE_SIZE) for k in range(npg)], axis=0).astype(BF16))

    @pl.when(c == pl.num_programs(1) - 1)
    def _():
        sn = scores(lambda g: pad_rows(ksn_ref[:, g * DKP:(g + 1) * DKP]), qzp_ref)
        u = lax.broadcasted_iota(jnp.int32, (16, nl), 0)
        update(sn, (u <= tt) & (u < ts), lambda g: pad_rows(vsn_ref[:, g * NSA_DV:(g + 1) * NSA_DV]))
        m, l = ml_ref[0:1, :], ml_ref[1:2, :]
        o_slc = jnp.where(m > 0.5 * NEG, acc_ref[...] / l, 0.0)
        o_ref[0] = gate_ref[0, 0:1, :] * ocw_ref[0] + gate_ref[0, 1:2, :] * o_slc + gate_ref[0, 2:3, :] * ocw_ref[1]


def _nsa_sample(page_table, cache_k, cache_v, qz, qzp, kc, vc, ksn, vsn, kwb, vwb, kwn, vwn, gates, ts):
    db, n_pages = page_table.shape
    npg = _PAGES_PER_STEP
    steps = n_pages // npg
    past = n_pages * PAGE_SIZE
    nb = past // NSA_BLOCK
    w = kwb.shape[4]
    assert NSA_HPG * ts * NSA_GROUPS == 128 and ts <= 16

    def kpage_spec(k):
        return pl.BlockSpec((1, 1, NSA_GROUPS, NSA_DK, PAGE_SIZE), lambda b, c, pt: (0, pt[b, c * npg + k], 0, 0, 0))

    def vpage_spec(k):
        return pl.BlockSpec((1, PAGE_SIZE * NSA_GROUPS, NSA_DV), lambda b, c, pt: (pt[b, c * npg + k], 0, 0))

    per_seq = lambda *shape: pl.BlockSpec((1,) + shape, lambda b, c, pt: (b,) + (0,) * len(shape))
    rows = lambda width: pl.BlockSpec((ts, width), lambda b, c, pt: (b, 0))
    kwin = pl.BlockSpec((1, 1, NSA_GROUPS, NSA_DK, w), lambda b, c, pt: (0, b, 0, 0, 0))
    vwin = pl.BlockSpec((1, w * NSA_GROUPS, NSA_DV), lambda b, c, pt: (b, 0, 0))
    grid_spec = pltpu.PrefetchScalarGridSpec(
        num_scalar_prefetch=1,
        grid=(db, steps),
        in_specs=([kpage_spec(k) for k in range(npg)] + [vpage_spec(k) for k in range(npg)] + [
            per_seq(NSA_GROUPS, NSA_DK, 128), per_seq(NSA_GROUPS, DKP, 128),
            per_seq(NSA_GROUPS, nb, DKP), per_seq(NSA_GROUPS, nb, NSA_DV),
            rows(NSA_GROUPS * DKP), rows(NSA_GROUPS * NSA_DV),
            kwin, vwin,
            rows(NSA_GROUPS * DKP), rows(NSA_GROUPS * NSA_DV),
            per_seq(8, 128)]),
        out_specs=per_seq(NSA_DV, 128),
        scratch_shapes=[pltpu.VMEM((nb, 128), F32), pltpu.VMEM((NSA_DV, 128), F32),
                        pltpu.VMEM((8, 128), F32), pltpu.VMEM((2, NSA_DV, 128), F32)])
    return pl.pallas_call(
        functools.partial(_nsa_sample_kernel, past=past, ts=ts, n_keep=min(NSA_N_SEL - 1, nb)),
        grid_spec=grid_spec,
        out_shape=jax.ShapeDtypeStruct((db, NSA_DV, 128), F32),
        compiler_params=_cparams(("arbitrary", "arbitrary")),
        name="nsa_sample",
    )(page_table, *([cache_k] * npg), *([cache_v] * npg), qz, qzp, kc, vc, ksn, vsn, kwb, vwb, kwn, vwn, gates)


def _split3(x):
    hi = x.astype(BF16)
    r1 = x - hi.astype(F32)
    mid = r1.astype(BF16)
    lo = (r1 - mid.astype(F32)).astype(BF16)
    return jnp.concatenate([hi, mid, lo], axis=1)


_GLA_HEADS_PER_STEP = 2


def _gla_kernel(q_ref, k_ref, v_ref, a_ref, wa_ref, ba_ref, s0_ref, o_ref, sout_ref, st_ref,
                *, c, n_sub, n_valid):
    ci = pl.program_id(2)
    levels = int(math.log2(c))
    hp = _GLA_HEADS_PER_STEP
    dk, dv = GLA_DK, GLA_DV

    @pl.when(ci == 0)
    def _():
        for hh in range(hp):
            st_ref[hh] = s0_ref[0, hh].T

    t_io = lax.broadcasted_iota(jnp.int32, (c, c), 0)
    s_io = lax.broadcasted_iota(jnp.int32, (c, c), 1)
    row = lax.broadcasted_iota(jnp.int32, (c, 1), 0)
    sels = [s_io <= t_io]
    for lv in range(1, levels + 1):
        sels.append(s_io <= ((t_io >> lv) << lv) + (1 << (lv - 1)) - 1)
    sel_all = jnp.concatenate(sels, axis=0).astype(F32).astype(BF16)
    for u in range(n_sub):
        rows = slice(u * c, (u + 1) * c)
        a_bf = a_ref[rows, :].astype(BF16)
        for hh in range(hp):
            kcols = slice(hh * dk, (hh + 1) * dk)
            vcols = slice(hh * dv, (hh + 1) * dv)
            q = q_ref[rows, kcols] * (GLA_DK ** -0.5)
            k = k_ref[rows, kcols]
            v = v_ref[rows, vcols].astype(BF16)
            z = jnp.dot(a_bf, wa_ref[:, kcols], preferred_element_type=F32) + ba_ref[:, kcols]
            la = (jnp.minimum(z, 0.0) - jnp.log1p(jnp.exp(-jnp.abs(z)))) * (1.0 / GLA_TAU)
            if n_valid < c:
                la = jnp.where(row < n_valid, la, 0.0)
            big = jnp.dot(sel_all, _split3(la), preferred_element_type=F32)
            big = big[:, 0:dk] + big[:, dk:2 * dk] + big[:, 2 * dk:3 * dk]
            b = big[0:c]
            st = st_ref[hh]
            o = lax.dot_general((q * jnp.exp(b)).astype(BF16), st.astype(BF16),
                                (((1,), (1,)), ((), ())), preferred_element_type=F32)
            a_mat = jnp.where(t_io == s_io, jnp.sum(q * k, axis=1, keepdims=True), 0.0)
            for lv in range(1, levels + 1):
                bref = big[lv * c:(lv + 1) * c]
                upper = ((row >> (lv - 1)) & 1) == 1
                ql = jnp.where(upper, q * jnp.exp(b - bref), 0.0).astype(BF16)
                kl = jnp.where(upper, 0.0, k * jnp.exp(bref - b)).astype(BF16)
                al = lax.dot_general(ql, kl, (((1,), (1,)), ((), ())), preferred_element_type=F32)
                a_mat = a_mat + jnp.where((t_io >> lv) == (s_io >> lv), al, 0.0)
            o = o + jnp.dot(a_mat.astype(BF16), v, preferred_element_type=F32)
            o_ref[rows, vcols] = o
            b_last = b[c - 1:c, :]
            kd = (k * jnp.exp(b_last - b)).astype(BF16)
            st_ref[hh] = st * jnp.exp(b_last) + lax.dot_general(
                v, kd, (((0,), (0,)), ((), ())), preferred_element_type=F32)

    @pl.when(ci == pl.num_programs(2) - 1)
    def _():
        for hh in range(hp):
            sout_ref[0, hh] = st_ref[hh].T


def _gla(p, n_batch, t_len, s0, wa_pad, ba, c, n_sub, n_valid):
    step = c * n_sub
    ncs = t_len // step
    hp = _GLA_HEADS_PER_STEP
    qb, kb = _SEG["q_g"] // (hp * GLA_DK), _SEG["k_g"] // (hp * GLA_DK)
    vb, ab = _SEG["v_g"] // (hp * GLA_DV), _SEG["a_g"] // 256
    return pl.pallas_call(
        functools.partial(_gla_kernel, c=c, n_sub=n_sub, n_valid=n_valid),
        grid=(n_batch, GLA_HEADS // hp, ncs),
        in_specs=[pl.BlockSpec((step, hp * GLA_DK), lambda b, h, ci: (b * ncs + ci, qb + h)),
                  pl.BlockSpec((step, hp * GLA_DK), lambda b, h, ci: (b * ncs + ci, kb + h)),
                  pl.BlockSpec((step, hp * GLA_DV), lambda b, h, ci: (b * ncs + ci, vb + h)),
                  pl.BlockSpec((step, 256), lambda b, h, ci: (b * ncs + ci, ab)),
                  pl.BlockSpec((256, hp * GLA_DK), lambda b, h, ci: (0, h)),
                  pl.BlockSpec((1, hp * GLA_DK), lambda b, h, ci: (0, h)),
                  pl.BlockSpec((1, hp, GLA_DK, GLA_DV), lambda b, h, ci: (b, h, 0, 0))],
        out_specs=[pl.BlockSpec((step, hp * GLA_DV), lambda b, h, ci: (b * ncs + ci, h)),
                   pl.BlockSpec((1, hp, GLA_DK, GLA_DV), lambda b, h, ci: (b, h, 0, 0))],
        out_shape=[jax.ShapeDtypeStruct((n_batch * t_len, GLA_WIDTH), F32),
                   jax.ShapeDtypeStruct((n_batch, GLA_HEADS, GLA_DK, GLA_DV), F32)],
        scratch_shapes=[pltpu.VMEM((hp, GLA_DV, GLA_DK), F32)],
        compiler_params=_cparams(("arbitrary", "arbitrary", "arbitrary")),
        name="gla",
    )(p, p, p, p, wa_pad, ba, s0)


def _merge1_kernel(on_ref, zn_ref, og_ref, zg_ref, gg_ref, wa_ref, wb_ref, ma_ref, mb_ref, o_ref,
                   a_ref, b_ref):
    @pl.when(pl.program_id(1) == 0)
    def _():
        a_ref[...] = (on_ref[...] * _silu(zn_ref[...])).astype(BF16)
        og = og_ref[...]
        zg = zg_ref[...]
        for h in range(GLA_HEADS):
            sl = slice(h * GLA_DV, (h + 1) * GLA_DV)
            x = og[:, sl]
            y = x * lax.rsqrt(jnp.mean(x * x, axis=-1, keepdims=True) + NORM_EPS) * gg_ref[...]
            b_ref[:, sl] = (y * _silu(zg[:, sl])).astype(BF16)

    ua = jnp.dot(a_ref[...], wa_ref[...], preferred_element_type=F32)
    ub = jnp.dot(b_ref[...], wb_ref[...], preferred_element_type=F32)
    o_ref[...] = (jax.nn.sigmoid(ma_ref[...]) * ua + jax.nn.sigmoid(mb_ref[...]) * ub).astype(BF16)


def _merge1(o_nsa, o_gla, p, gla_g, wb_a, wb_b, tm):
    m = p.shape[0]
    tn = 512
    zn, zg, mg = _SEG["z_nsa"] // NSA_WIDTH, _SEG["z_g"] // GLA_WIDTH, _SEG["m_gate"] // tn
    row = lambda i, j: (i, 0)
    return pl.pallas_call(
        _merge1_kernel,
        grid=(m // tm, D_MODEL // tn),
        in_specs=[pl.BlockSpec((tm, NSA_WIDTH), row),
                  pl.BlockSpec((tm, NSA_WIDTH), lambda i, j: (i, zn)),
                  pl.BlockSpec((tm, GLA_WIDTH), row),
                  pl.BlockSpec((tm, GLA_WIDTH), lambda i, j: (i, zg)),
                  pl.BlockSpec((1, GLA_DV), lambda i, j: (0, 0)),
                  pl.BlockSpec((NSA_WIDTH, tn), lambda i, j: (0, j)),
                  pl.BlockSpec((GLA_WIDTH, tn), lambda i, j: (0, j)),
                  pl.BlockSpec((tm, tn), lambda i, j: (i, mg + j)),
                  pl.BlockSpec((tm, tn), lambda i, j: (i, mg + D_MODEL // tn + j))],
        out_specs=pl.BlockSpec((tm, tn), lambda i, j: (i, j)),
        out_shape=jax.ShapeDtypeStruct((m, D_MODEL), BF16),
        scratch_shapes=[pltpu.VMEM((tm, NSA_WIDTH), BF16), pltpu.VMEM((tm, GLA_WIDTH), BF16)],
        compiler_params=_cparams(("arbitrary", "arbitrary")),
        name="merge1",
    )(o_nsa, p, o_gla, p, gla_g, wb_a, wb_b, p, p)


def _merge2_kernel(x_ref, gate_ref, mg_ref, w_ref, o_ref):
    o_ref[...] = x_ref[...] + gate_ref[...] * jnp.dot(mg_ref[...], w_ref[...], preferred_element_type=F32)


def _merge2(x, gate, merged, w_out, tm):
    m, d = x.shape
    tn = 512
    per_row = gate.shape[0] != 1
    gate_spec = (pl.BlockSpec((tm, tn), lambda i, j: (i, j)) if per_row
                 else pl.BlockSpec((1, tn), lambda i, j: (0, j)))
    return pl.pallas_call(
        _merge2_kernel,
        grid=(m // tm, d // tn),
        in_specs=[pl.BlockSpec((tm, tn), lambda i, j: (i, j)), gate_spec,
                  pl.BlockSpec((tm, d), lambda i, j: (i, 0)),
                  pl.BlockSpec((d, tn), lambda i, j: (0, j))],
        out_specs=pl.BlockSpec((tm, tn), lambda i, j: (i, j)),
        out_shape=jax.ShapeDtypeStruct((m, d), F32),
        compiler_params=_cparams(("arbitrary", "arbitrary")),
        name="merge2",
    )(x, gate, merged, w_out)


def _pad_last(a, width):
    return jnp.pad(a, [(0, 0)] * (a.ndim - 1) + [(0, width - a.shape[-1])])


def _pad_heads(w, n, src, dst):
    return _pad_last(w.reshape(w.shape[0], n, src), dst).reshape(w.shape[0], n * dst)


def _pad_w_in(w_in):
    sizes = (NSA_HEADS * NSA_DK, NSA_GROUPS * NSA_DK, NSA_GROUPS * NSA_DV, NSA_GROUPS * NSA_DK,
             NSA_GROUPS * NSA_DV, NSA_GROUPS * NSA_DK, NSA_GROUPS * NSA_DV, 3 * NSA_HEADS, NSA_WIDTH,
             GLA_HEADS * GLA_DK, GLA_HEADS * GLA_DK, GLA_WIDTH, GLA_WIDTH, GLA_RANK, 2 * D_MODEL)
    cuts = [0]
    for s in sizes:
        cuts.append(cuts[-1] + s)
    names = ("q", "k_cmp", "v_cmp", "k_slc", "v_slc", "k_win", "v_win", "g_nsa", "z_nsa",
             "q_g", "k_g", "v_g", "z_g", "a_g", "m_gate")
    parts = {nm: w_in[:, cuts[n]:cuts[n + 1]] for n, nm in enumerate(names)}
    parts["q"] = _pad_heads(parts["q"], NSA_HEADS, NSA_DK, DKP)
    for nm in ("k_cmp", "k_slc", "k_win"):
        parts[nm] = _pad_heads(parts[nm], NSA_GROUPS, NSA_DK, DKP)
    parts["g_nsa"] = _pad_last(_pad_heads(parts["g_nsa"], NSA_GROUPS, 3 * NSA_HPG, 16), 256)
    parts["a_g"] = _pad_last(parts["a_g"], 256)
    return jnp.concatenate([parts[nm] for nm in _SEG], axis=1).astype(BF16)


def _unpad_heads(a, n):
    return a.reshape(a.shape[0], n, DKP)[:, :, :NSA_DK]


def kernel(x_prompt, x_sample, cache_k_cmp, cache_v_cmp, cache_k_slc, cache_v_slc, state_k_win, state_v_win, state_gla, page_table, c_prompt, c_sample, ln_g, w_ada, b_ada, w_in, q_norm_g, k_cmp_norm_g, k_slc_norm_g, k_win_norm_g, pe_k, w1_k, w2_k, pe_v, w1_v, w2_v, w_a2, b_a, gla_norm_g, w_branch, w_out):
    assert ln_g.shape[0] == 1, "single layer"
    bp, t, d = x_prompt.shape
    db, ts, _ = x_sample.shape
    assert bp == 1 and d == D_MODEL
    G = NSA_GROUPS

    w_pad = _pad_w_in(w_in[0])
    g_q = _pad_last(q_norm_g, DKP)
    g_kc = _pad_last(k_cmp_norm_g, DKP)
    g_ks = _pad_last(k_slc_norm_g, DKP)
    g_kw = _pad_last(k_win_norm_g, DKP)
    pe_k_p = _pad_last(pe_k[0], DKP)[:, None, :]
    w1_k_p = jnp.pad(w1_k[0], ((0, 0), (0, DKP - NSA_DK), (0, 0))).astype(BF16)
    w2_k_p = _pad_last(w2_k[0], DKP).astype(BF16)
    pe_v_p = pe_v[0][:, None, :]
    w1_v_p = w1_v[0].astype(BF16)
    w2_v_p = w2_v[0].astype(BF16)
    wa_pad = jnp.pad(w_a2[0], ((0, 256 - GLA_RANK), (0, 0))).astype(BF16)
    wb_a = w_branch[0, :NSA_WIDTH].astype(BF16)
    wb_b = w_branch[0, NSA_WIDTH:].astype(BF16)
    w_o = w_out[0].astype(BF16)
    ones_v = jnp.ones((1, NSA_DV), F32)

    n_c = bp + db
    c_all = jnp.pad(jnp.concatenate([c_prompt, c_sample], axis=0), ((0, (-n_c) % 8), (0, 0)))
    mod = _ada(c_all, w_ada[0], b_ada)
    shift, scale, gate = mod[:, :d], mod[:, d:2 * d], mod[:, 2 * d:]

    tm = min(1024, t)
    xp = x_prompt.reshape(t, d)
    pp = _proj(xp, scale[0:1], shift[0:1], ln_g, w_pad, tm)
    tq = min(256, t)
    tk = min(512, t)
    qt = _prep_q(pp, g_q, min(2048, t))
    ks_f, ks_b = _prep_k(pp, "k_slc", g_ks, min(2048, t))
    kw_f, kw_b = _prep_k(pp, "k_win", g_kw, min(2048, t))
    vst = _prep_vt(pp, "v_slc", tk, ONES_ROWS)
    vwt = _prep_vt(pp, "v_win", tq)
    gt = _prep_gate(pp, min(512, t))
    kc = _compress_prompt(pp, "k_cmp", DKP, pe_k_p, w1_k_p, w2_k_p, g_kc, True, False)
    vct = _compress_prompt(pp, "v_cmp", NSA_DV, pe_v_p, w1_v_p, w2_v_p, ones_v, False, True)
    o_nsa_p = _nsa_prompt(qt, kc, vct, ks_b, vst, kw_b, vwt, gt, tq, tk)
    s0_p = jnp.zeros((bp, GLA_HEADS, GLA_DK, GLA_DV), F32)
    o_gla_p, gla_p = _gla(pp, bp, t, s0_p, wa_pad, b_a, GLA_CHUNK, 4, GLA_CHUNK)
    merged_p = _merge1(o_nsa_p, o_gla_p, pp, gla_norm_g, wb_a, wb_b, min(512, t))
    y_p = _merge2(xp, gate[0:1], merged_p, w_o, min(512, t)).reshape(bp, t, d)

    wp = min(NSA_WINDOW, t)
    k_cmp_p = _unpad_heads(pp[:, _SEG["k_cmp"]:_SEG["k_cmp"] + G * DKP], G)
    v_cmp_p = pp[:, _SEG["v_cmp"]:_SEG["v_cmp"] + G * NSA_DV].reshape(t, G, NSA_DV)
    k_slc_p = _unpad_heads(ks_f, G)
    v_slc_p = pp[:, _SEG["v_slc"]:_SEG["v_slc"] + G * NSA_DV].reshape(t, G, NSA_DV)
    k_win_p = _unpad_heads(kw_f, G)[t - wp:]
    v_win_p = pp[t - wp:, _SEG["v_win"]:_SEG["v_win"] + G * NSA_DV].reshape(wp, G, NSA_DV)

    ms = db * ts
    xs = x_sample.reshape(ms, d)
    rep = lambda a: jnp.repeat(a[bp:bp + db], ts, axis=0)
    ps = _proj(xs, rep(scale), rep(shift), ln_g, w_pad, ms)
    seg = lambda name, w: ps[:, _SEG[name]:_SEG[name] + w]
    qt_s = _prep_q(ps, g_q, ms)
    kss_f, _ = _prep_k(ps, "k_slc", g_ks, ms)
    kws_f, _ = _prep_k(ps, "k_win", g_kw, ms)
    gt_s = _prep_gate(ps, ms)
    v_slc_new = seg("v_slc", G * NSA_DV)
    v_win_new = seg("v_win", G * NSA_DV)
    k_cmp_s = _unpad_heads(seg("k_cmp", G * DKP), G).reshape(db, ts, G, NSA_DK)
    v_cmp_s = seg("v_cmp", G * NSA_DV).reshape(db, ts, G, NSA_DV)
    k_slc_s = _unpad_heads(kss_f, G).reshape(db, ts, G, NSA_DK)
    v_slc_s = v_slc_new.reshape(db, ts, G, NSA_DV)
    k_win_s_new = _unpad_heads(kws_f, G).reshape(db, ts, G, NSA_DK)
    v_win_s_new = v_win_new.reshape(db, ts, G, NSA_DV)
    q4 = qt_s.reshape(G, NSA_HPG, DKP, db, ts).transpose(3, 0, 2, 1, 4).reshape(db, G, DKP, NSA_HPG * ts)
    own = (jnp.arange(G)[:, None] == jnp.arange(G)[None, :])[None, :, None, :, None]
    qzp = jnp.where(own, q4[:, :, :, None, :], jnp.zeros((), BF16)).reshape(db, G, DKP, G * NSA_HPG * ts)
    qz = qzp[:, :, :NSA_DK, :]
    gates = gt_s.reshape(G, 16, db, ts)[:, :3 * NSA_HPG].reshape(G, NSA_HPG, 3, db, ts)
    gates = gates.transpose(3, 2, 0, 1, 4).reshape(db, 3, G * NSA_HPG * ts)
    gates = jnp.pad(gates, ((0, 0), (0, 5), (0, 0)))
    keys_t = lambda a: jnp.transpose(a, (0, 1, 3, 4, 2))
    vals_2d = lambda a: a.reshape(a.shape[1], a.shape[2] * G, NSA_DV)
    kc_s = _compress_paged(keys_t(cache_k_cmp), page_table, pe_k_p[:, 0, :], w1_k_p, w2_k_p, g_kc, True)
    vc_s = _compress_paged(vals_2d(cache_v_cmp), page_table, pe_v[0], w1_v_p, w2_v_p, ones_v, False)
    o_t = _nsa_sample(page_table, keys_t(cache_k_slc), vals_2d(cache_v_slc), qz, qzp, kc_s, vc_s, kss_f,
                      v_slc_new, keys_t(state_k_win), vals_2d(state_v_win), kws_f, v_win_new, gates, ts)
    o_nsa_s = o_t.reshape(db, NSA_DV, G, NSA_HPG, ts).transpose(0, 4, 2, 3, 1)
    kw_s = jnp.concatenate([state_k_win[0][:, ts:], k_win_s_new], axis=1)
    vw_s = jnp.concatenate([state_v_win[0][:, ts:], v_win_s_new], axis=1)
    c_s = 16
    ps_pad = jnp.pad(ps.reshape(db, ts, D_PAD), ((0, 0), (0, c_s - ts), (0, 0))).reshape(db * c_s, D_PAD)
    o_gla_s, gla_s = _gla(ps_pad, db, c_s, state_gla[0], wa_pad, b_a, c_s, 1, ts)
    o_gla_s = o_gla_s.reshape(db, c_s, GLA_WIDTH)[:, :ts].reshape(ms, GLA_WIDTH)
    merged_s = _merge1(o_nsa_s.reshape(ms, NSA_WIDTH), o_gla_s, ps, gla_norm_g, wb_a, wb_b, ms)
    y_s = _merge2(xs, rep(gate), merged_s, w_o, ms).reshape(db, ts, d)

    L1 = lambda a: a[None, None]
    return (y_p, y_s,
            L1(k_cmp_p), L1(v_cmp_p), L1(k_slc_p), L1(v_slc_p), L1(k_win_p), L1(v_win_p), gla_p[None],
            k_cmp_s[None], v_cmp_s[None], k_slc_s[None], v_slc_s[None], kw_s[None], vw_s[None], gla_s[None])
```
